```python
import math
import jax, jax.numpy as jnp
from jax import lax
import numpy as np

D_MODEL = 2048
BATCH = 4
SEQ = 2048
DEPTH = 1
DEC_BATCH = 128
DEC_SEQ = 8
PAST_LEN = 16384
PAGE_SIZE = 128

MIX_POOL = D_MODEL // 2
MIX_SSM = D_MODEL - MIX_POOL
POOL_WINDOWS = (2, 4, 8, 16)
POOL_GROUP = MIX_POOL // len(POOL_WINDOWS)
POOL_BUF = max(POOL_WINDOWS) - 1
SSM_P = 16
SSM_G = MIX_SSM // SSM_P
SSM_N = 64
N_EXPERTS = 256
TOP_K = 8
N_EXPERT_GROUPS = 8
TOPK_GROUPS = 4
D_EXPERT = D_MODEL // 4
D_SHARED = D_MODEL // 4
ROUTED_SCALE = 2.5
MOE_BLOCK = 128
LN_EPS = 1e-5
DN_ALPHA = (2.0 * DEPTH) ** 0.25
DN_BETA = (8.0 * DEPTH) ** -0.25

kernel_name = 'hymba_pool_s5_moe_adaln_step'

F32 = jnp.float32


def _ln(x):
    xf = x.astype(F32)
    xc = xf - xf.mean(-1, keepdims=True)
    return xc * lax.rsqrt((xc * xc).mean(-1, keepdims=True) + LN_EPS)


def _post_ln(r, g, b, dtype):
    return (_ln(r) * g.astype(F32) + b.astype(F32)).astype(dtype)


def _modulate(x, shift, scale):
    y = _ln(x) * (1.0 + scale[:, None, :]) + shift[:, None, :]
    return y.astype(x.dtype)


def _rms(x, g):
    xf = x.astype(F32)
    y = xf * lax.rsqrt((xf * xf).mean(-1, keepdims=True) + LN_EPS) * g.astype(F32)
    return y.astype(x.dtype)


def _adaln(c, w_ada, b_ada):
    mod = jax.nn.silu(c.astype(F32)) @ w_ada.astype(F32) + b_ada.astype(F32)
    return jnp.split(mod, 6, axis=-1)


def _pool_mixer(za, prefix, start_pos, w_pool, pool_scale):
    B, L, C = za.shape
    zf = za.astype(F32)
    xp = jnp.concatenate([prefix.astype(F32), zf], axis=1)
    S = jnp.concatenate([jnp.zeros((B, 1, C), F32), jnp.cumsum(xp, axis=1)], axis=1)
    pos = start_pos + jnp.arange(L, dtype=jnp.int32)
    outs = []
    for gi, w in enumerate(POOL_WINDOWS):
        sl = slice(gi * POOL_GROUP, (gi + 1) * POOL_GROUP)
        win = S[:, POOL_BUF + 1:POOL_BUF + 1 + L, sl] - S[:, POOL_BUF + 1 - w:POOL_BUF + 1 - w + L, sl]
        cnt = jnp.minimum(pos + 1, w).astype(F32)[None, :, None]
        d = (win / cnt - zf[..., sl]).astype(za.dtype)
        outs.append(d @ w_pool[gi])
    y = jnp.concatenate(outs, axis=-1) * pool_scale
    new_buf = jnp.concatenate([prefix.astype(za.dtype), za], axis=1)[:, -POOL_BUF:]
    return y, new_buf


def _cplx_combine(e1, e2):
    ar1, ai1, br1, bi1 = e1
    ar2, ai2, br2, bi2 = e2
    return (ar2 * ar1 - ai2 * ai1,
            ar2 * ai1 + ai2 * ar1,
            ar2 * br1 - ai2 * bi1 + br2,
            ar2 * bi1 + ai2 * br1 + bi2)


def _ssm_mixer(zs, h0_re, h0_im, A_re, A_im, log_dt, B_re, B_im, C_re, C_im, D_skip, w_glu, b_glu):
    Bsz, L, _ = zs.shape
    u = zs.astype(F32).reshape(Bsz, L, SSM_G, SSM_P)
    Ar, Ai = A_re.astype(F32), A_im.astype(F32)
    dt = jnp.exp(log_dt.astype(F32))[:, None]
    mag = jnp.exp(Ar * dt)
    ab_re, ab_im = mag * jnp.cos(Ai * dt), mag * jnp.sin(Ai * dt)
    den = Ar * Ar + Ai * Ai
    nr = ab_re - 1.0
    f_re = (nr * Ar + ab_im * Ai) / den
    f_im = (ab_im * Ar - nr * Ai) / den
    Br, Bi = B_re.astype(F32), B_im.astype(F32)
    bb_re = f_re[..., None] * Br - f_im[..., None] * Bi
    bb_im = f_re[..., None] * Bi + f_im[..., None] * Br
    bu_re = jnp.einsum('blgp,gnp->blgn', u, bb_re)
    bu_im = jnp.einsum('blgp,gnp->blgn', u, bb_im)
    h0r, h0i = h0_re.astype(F32), h0_im.astype(F32)
    bu_re = bu_re.at[:, 0].add(ab_re * h0r - ab_im * h0i)
    bu_im = bu_im.at[:, 0].add(ab_re * h0i + ab_im * h0r)
    a_re = jnp.broadcast_to(ab_re, bu_re.shape)
    a_im = jnp.broadcast_to(ab_im, bu_im.shape)
    _, _, h_re, h_im = lax.associative_scan(_cplx_combine, (a_re, a_im, bu_re, bu_im), axis=1)
    y = (jnp.einsum('gpn,blgn->blgp', C_re.astype(F32), h_re)
         - jnp.einsum('gpn,blgn->blgp', C_im.astype(F32), h_im)
         + D_skip.astype(F32) * u).reshape(Bsz, L, MIX_SSM)
    g = jax.nn.gelu(y)
    out = g * jax.nn.sigmoid(g @ w_glu.astype(F32) + b_glu.astype(F32))
    return out.astype(zs.dtype), h_re[:, -1], h_im[:, -1]


def _mixer(u, prefix, h0_re, h0_im, start_pos, w_in, w_pool, pool_scale, ssm_params, g_pool, g_ssm, w_out):
    z = u @ w_in
    za, zs = z[..., :MIX_POOL], z[..., MIX_POOL:]
    ya, new_buf = _pool_mixer(za, prefix, start_pos, w_pool, pool_scale)
    ys, h_re, h_im = _ssm_mixer(zs, h0_re, h0_im, *ssm_params)
    y = jnp.concatenate([_rms(ya, g_pool), _rms(ys, g_ssm)], axis=-1) @ w_out
    return y, new_buf, h_re, h_im


def _moe(h, w_router, router_bias, w_e_gate, w_e_up, w_e_down, w_sh_gate, w_sh_up, w_sh_down):
    T, D = h.shape
    s = jax.nn.sigmoid((h @ w_router).astype(F32))
    sb = s + router_bias.astype(F32)
    per = N_EXPERTS // N_EXPERT_GROUPS
    grp_score = lax.top_k(sb.reshape(T, N_EXPERT_GROUPS, per), 2)[0].sum(-1)
    _, gidx = lax.top_k(grp_score, TOPK_GROUPS)
    gmask = jax.nn.one_hot(gidx, N_EXPERT_GROUPS, dtype=F32).sum(1)
    emask = jnp.repeat(gmask, per, axis=1) > 0
    _, eidx = lax.top_k(jnp.where(emask, sb, -jnp.inf), TOP_K)
    wsel = jnp.take_along_axis(s, eidx, axis=1)
    wsel = wsel / wsel.sum(-1, keepdims=True) * ROUTED_SCALE
    A = T * TOP_K
    flat_e = eidx.reshape(A).astype(jnp.int32)
    flat_tok = jnp.repeat(jnp.arange(T, dtype=jnp.int32), TOP_K)
    flat_w = wsel.reshape(A)
    order = jnp.argsort(flat_e)
    se, stok, sw = flat_e[order], flat_tok[order], flat_w[order]
    counts = jnp.zeros((N_EXPERTS,), jnp.int32).at[flat_e].add(1)
    starts = jnp.cumsum(counts) - counts
    pcounts = (counts + MOE_BLOCK - 1) // MOE_BLOCK * MOE_BLOCK
    pends = jnp.cumsum(pcounts)
    pstarts = pends - pcounts
    dest = pstarts[se] + jnp.arange(A, dtype=jnp.int32) - starts[se]
    n_blocks = -(-A // MOE_BLOCK) + N_EXPERTS
    P = n_blocks * MOE_BLOCK
    buf_tok = jnp.full((P,), T, jnp.int32).at[dest].set(stok)
    buf_w = jnp.zeros((P,), F32).at[dest].set(sw)
    block_e = jnp.minimum(jnp.searchsorted(pends, jnp.arange(n_blocks, dtype=jnp.int32) * MOE_BLOCK,
                                           side='right'), N_EXPERTS - 1).astype(jnp.int32)
    h_pad = jnp.concatenate([h, jnp.zeros((1, D), h.dtype)], axis=0)

    def run_block(args):
        tok, e = args
        xb = h_pad[tok]
        a = jax.nn.silu(xb @ w_e_gate[e]) * (xb @ w_e_up[e])
        return a @ w_e_down[e]

    yb = lax.map(run_block, (buf_tok.reshape(n_blocks, MOE_BLOCK), block_e))
    routed = jnp.zeros((T + 1, D), F32).at[buf_tok].add(
        yb.reshape(P, D).astype(F32) * buf_w[:, None])[:T]
    shared = (jax.nn.silu(h @ w_sh_gate) * (h @ w_sh_up)) @ w_sh_down
    return (routed + shared.astype(F32)).astype(h.dtype)


def setup_inputs(seed: int = 0) -> dict:
    key = jax.random.key(seed)
    ks = iter(jax.random.split(key, 48))

    def nrm(shape, scale):
        return jax.random.normal(next(ks), shape, F32) * scale

    def gain(shape):
        return 1.0 + nrm(shape, 0.02)

    L = DEPTH
    n_idx = jnp.arange(SSM_N, dtype=F32)
    return {
        'x_prompt': nrm((BATCH, SEQ, D_MODEL), 1.0),
        'x_sample': nrm((DEC_BATCH, DEC_SEQ, D_MODEL), 1.0),
        'state_pool': nrm((L, DEC_BATCH, POOL_BUF, MIX_POOL), 1.0),
        'state_ssm_re': nrm((L, DEC_BATCH, SSM_G, SSM_N), 0.1),
        'state_ssm_im': nrm((L, DEC_BATCH, SSM_G, SSM_N), 0.1),
        'c_prompt': nrm((BATCH, D_MODEL), 1.0),
        'c_sample': nrm((DEC_BATCH, D_MODEL), 1.0),
        'w_ada': nrm((L, D_MODEL, 6 * D_MODEL), 0.1 * D_MODEL ** -0.5),
        'b_ada': nrm((L, 6 * D_MODEL), 0.02),
        'w_in': nrm((L, D_MODEL, D_MODEL), D_MODEL ** -0.5),
        'w_pool': nrm((L, len(POOL_WINDOWS), POOL_GROUP, POOL_GROUP), POOL_GROUP ** -0.5),
        'pool_scale': gain((L, MIX_POOL)),
        'A_re': -0.5 + nrm((L, SSM_G, SSM_N), 0.01),
        'A_im': jnp.pi * n_idx + nrm((L, SSM_G, SSM_N), 0.01),
        'log_dt': jax.random.uniform(next(ks), (L, SSM_G), F32, math.log(1e-3), math.log(1e-1)),
        'B_re': nrm((L, SSM_G, SSM_N, SSM_P), (2.0 * SSM_P) ** -0.5),
        'B_im': nrm((L, SSM_G, SSM_N, SSM_P), (2.0 * SSM_P) ** -0.5),
        'C_re': nrm((L, SSM_G, SSM_P, SSM_N), SSM_N ** -0.5),
        'C_im': nrm((L, SSM_G, SSM_P, SSM_N), SSM_N ** -0.5),
        'D_skip': nrm((L, SSM_G, SSM_P), 1.0),
        'w_glu': nrm((L, MIX_SSM, MIX_SSM), MIX_SSM ** -0.5),
        'b_glu': nrm((L, MIX_SSM), 0.02),
        'g_pool': gain((L, MIX_POOL)),
        'g_ssm': gain((L, MIX_SSM)),
        'w_out': nrm((L, D_MODEL, D_MODEL), DN_BETA * D_MODEL ** -0.5),
        'ln1_g': gain((L, D_MODEL)),
        'ln1_b': nrm((L, D_MODEL), 0.02),
        'w_router': nrm((L, D_MODEL, N_EXPERTS), D_MODEL ** -0.5),
        'router_bias': nrm((L, N_EXPERTS), 0.01),
        'w_e_gate': nrm((L, N_EXPERTS, D_MODEL, D_EXPERT), D_MODEL ** -0.5),
        'w_e_up': nrm((L, N_EXPERTS, D_MODEL, D_EXPERT), D_MODEL ** -0.5),
        'w_e_down': nrm((L, N_EXPERTS, D_EXPERT, D_MODEL), DN_BETA * D_EXPERT ** -0.5),
        'w_sh_gate': nrm((L, D_MODEL, D_SHARED), D_MODEL ** -0.5),
        'w_sh_up': nrm((L, D_MODEL, D_SHARED), D_MODEL ** -0.5),
        'w_sh_down': nrm((L, D_SHARED, D_MODEL), DN_BETA * D_SHARED ** -0.5),
        'ln2_g': gain((L, D_MODEL)),
        'ln2_b': nrm((L, D_MODEL), 0.02),
    }


def reference(x_prompt, x_sample, state_pool, state_ssm_re, state_ssm_im, c_prompt, c_sample,
              w_ada, b_ada, w_in, w_pool, pool_scale, A_re, A_im, log_dt, B_re, B_im, C_re, C_im,
              D_skip, w_glu, b_glu, g_pool, g_ssm, w_out, ln1_g, ln1_b, w_router, router_bias,
              w_e_gate, w_e_up, w_e_down, w_sh_gate, w_sh_up, w_sh_down, ln2_g, ln2_b):
    xp, xs = x_prompt, x_sample
    Bp, Lp = xp.shape[0], xp.shape[1]
    Bs, Ls = xs.shape[0], xs.shape[1]
    pool_p, pool_s, sre_p, sim_p, sre_s, sim_s = [], [], [], [], [], []
    for l in range(DEPTH):
        sh1p, sc1p, g1p, sh2p, sc2p, g2p = _adaln(c_prompt, w_ada[l], b_ada[l])
        sh1s, sc1s, g1s, sh2s, sc2s, g2s = _adaln(c_sample, w_ada[l], b_ada[l])
        ssm_params = (A_re[l], A_im[l], log_dt[l], B_re[l], B_im[l], C_re[l], C_im[l],
                      D_skip[l], w_glu[l], b_glu[l])
        up = _modulate(xp, sh1p, sc1p)
        us = _modulate(xs, sh1s, sc1s)
        mp, bufp, hrp, hip = _mixer(up, jnp.zeros((Bp, POOL_BUF, MIX_POOL), xp.dtype),
                                    jnp.zeros((Bp, SSM_G, SSM_N), F32), jnp.zeros((Bp, SSM_G, SSM_N), F32),
                                    0, w_in[l], w_pool[l], pool_scale[l], ssm_params,
                                    g_pool[l], g_ssm[l], w_out[l])
        ms, bufs, hrs, his = _mixer(us, state_pool[l], state_ssm_re[l], state_ssm_im[l],
                                    PAST_LEN, w_in[l], w_pool[l], pool_scale[l], ssm_params,
                                    g_pool[l], g_ssm[l], w_out[l])
        xp = _post_ln(DN_ALPHA * xp.astype(F32) + (1.0 + g1p)[:, None, :] * mp.astype(F32),
                      ln1_g[l], ln1_b[l], x_prompt.dtype)
        xs = _post_ln(DN_ALPHA * xs.astype(F32) + (1.0 + g1s)[:, None, :] * ms.astype(F32),
                      ln1_g[l], ln1_b[l], x_sample.dtype)
        pool_p.append(bufp); pool_s.append(bufs)
        sre_p.append(hrp); sim_p.append(hip); sre_s.append(hrs); sim_s.append(his)
        vp = _modulate(xp, sh2p, sc2p)
        vs = _modulate(xs, sh2s, sc2s)
        h = jnp.concatenate([vp.reshape(Bp * Lp, D_MODEL), vs.reshape(Bs * Ls, D_MODEL).astype(vp.dtype)], axis=0)
        f = _moe(h, w_router[l], router_bias[l], w_e_gate[l], w_e_up[l], w_e_down[l],
                 w_sh_gate[l], w_sh_up[l], w_sh_down[l])
        fp = f[:Bp * Lp].reshape(Bp, Lp, D_MODEL)
        fs = f[Bp * Lp:].reshape(Bs, Ls, D_MODEL)
        xp = _post_ln(DN_ALPHA * xp.astype(F32) + (1.0 + g2p)[:, None, :] * fp.astype(F32),
                      ln2_g[l], ln2_b[l], x_prompt.dtype)
        xs = _post_ln(DN_ALPHA * xs.astype(F32) + (1.0 + g2s)[:, None, :] * fs.astype(F32),
                      ln2_g[l], ln2_b[l], x_sample.dtype)
    y_prompt, y_sample = xp, xs
    new_pool_prompt = jnp.stack(pool_p)
    new_pool_sample = jnp.stack(pool_s)
    new_ssm_re_prompt = jnp.stack(sre_p)
    new_ssm_im_prompt = jnp.stack(sim_p)
    new_ssm_re_sample = jnp.stack(sre_s)
    new_ssm_im_sample = jnp.stack(sim_s)
    return (y_prompt, y_sample, new_pool_prompt, new_pool_sample,
            new_ssm_re_prompt, new_ssm_im_prompt, new_ssm_re_sample, new_ssm_im_sample)
```

```python
import functools

import jax
import jax.numpy as jnp
from jax import lax
from jax.experimental import pallas as pl
from jax.experimental.pallas import tpu as pltpu

F32 = jnp.float32
BF16 = jnp.bfloat16
I32 = jnp.int32

DEPTH = 1
PAST_LEN = 16384
POOL_WINDOWS = (2, 4, 8, 16)
POOL_HALO = 16
SSM_P = 16
SSM_N = 64
N_EXPERT_GROUPS = 8
TOPK_GROUPS = 4
TOP_K = 8
ROUTED_SCALE = 2.5
LN_EPS = 1e-5
DN_ALPHA = (2.0 * DEPTH) ** 0.25

ROWS = 256
SUBLANES = 8
LANES = 128
SCAN_W = 512
MOE_BM = 256
COMB_ROWS = 128
VMEM_LIMIT = 56 * 1024 * 1024


def _cparams(*sem):
    return pltpu.CompilerParams(dimension_semantics=sem, vmem_limit_bytes=VMEM_LIMIT)


def _ln(x):
    xc = x - jnp.mean(x, axis=-1, keepdims=True)
    return xc * lax.rsqrt(jnp.mean(xc * xc, axis=-1, keepdims=True) + LN_EPS)


def _rows(m_ref, bb, r):
    m = m_ref[...]
    c = m.shape[-1]
    return jnp.broadcast_to(m, (bb, r, c)).reshape(bb * r, c)


def _sigmoid(x):
    return 1.0 / (1.0 + jnp.exp(-x))


def _adaln_body(c_ref, w_ref, b_ref, o_ref):
    c = c_ref[...]
    s = (c * _sigmoid(c)).astype(BF16)
    o_ref[...] = jnp.dot(s, w_ref[...].astype(BF16), preferred_element_type=F32) + b_ref[...]


def _adaln(c_all, w_ada, b_ada):
    bc, d = c_all.shape
    n = w_ada.shape[1]
    tn = 1024
    return pl.pallas_call(
        _adaln_body,
        grid=(n // tn,),
        in_specs=[pl.BlockSpec((bc, d), lambda j: (0, 0)),
                  pl.BlockSpec((d, tn), lambda j: (0, j)),
                  pl.BlockSpec((1, tn), lambda j: (0, j))],
        out_specs=pl.BlockSpec((bc, tn), lambda j: (0, j)),
        out_shape=jax.ShapeDtypeStruct((bc, n), F32),
        compiler_params=_cparams("arbitrary"),
        name="adaln",
    )(c_all, w_ada, b_ada.reshape(1, n))


def _ssm_prep_body(ar_ref, ai_ref, dt_ref, ar16_ref, ai16_ref, dt16_ref, br_ref, bi_ref,
                   pr_ref, pi_ref, bbr_ref, bbi_ref):
    def zoh(a_r, a_i, dt):
        mag = jnp.exp(a_r * dt)
        ab_r, ab_i = mag * jnp.cos(a_i * dt), mag * jnp.sin(a_i * dt)
        den = a_r * a_r + a_i * a_i
        nr = ab_r - 1.0
        return ab_r, ab_i, (nr * a_r + ab_i * a_i) / den, (ab_i * a_r - nr * a_i) / den

    ab_r, ab_i, _, _ = zoh(ar_ref[...], ai_ref[...], jnp.exp(dt_ref[...]))
    p_r, p_i = ab_r, ab_i
    for k in range(SUBLANES):
        pr_ref[k] = p_r
        pi_ref[k] = p_i
        p_r, p_i = p_r * ab_r - p_i * ab_i, p_r * ab_i + p_i * ab_r
    _, _, f_r, f_i = zoh(ar16_ref[...], ai16_ref[...], jnp.exp(dt16_ref[...]))
    b_r, b_i = br_ref[...], bi_ref[...]
    bbr_ref[...] = f_r * b_r - f_i * b_i
    bbi_ref[...] = f_r * b_i + f_i * b_r


def _ssm_prep(a_re, a_im, log_dt, b_re, b_im):
    g, n = a_re.shape
    p = b_re.shape[-1]
    dt = jnp.broadcast_to(log_dt[:, None], (g, n))
    rep = lambda a: jnp.repeat(a, p, axis=-1)
    outs = pl.pallas_call(
        _ssm_prep_body,
        out_shape=(jax.ShapeDtypeStruct((SUBLANES, g, n), F32), jax.ShapeDtypeStruct((SUBLANES, g, n), F32),
                   jax.ShapeDtypeStruct((g, n * p), F32), jax.ShapeDtypeStruct((g, n * p), F32)),
        name="ssm_prep",
    )(a_re, a_im, dt, rep(a_re), rep(a_im), rep(dt), b_re.reshape(g, n * p), b_im.reshape(g, n * p))
    pw_r, pw_i, bb_r, bb_i = outs
    return (pw_r.reshape(SUBLANES, g * n), pw_i.reshape(SUBLANES, g * n),
            bb_r.reshape(g, n, p), bb_i.reshape(g, n, p))


def _block_diag(w, gpt):
    g, a, b = w.shape
    w4 = w.reshape(g // gpt, gpt, a, b)
    eye = jnp.eye(gpt, dtype=w.dtype)
    return jnp.einsum('jgab,gh->jgahb', w4, eye).reshape(g // gpt, gpt * a, gpt * b)


def _mix_in_body(x_ref, sh_ref, sc_ref, w_ref, z_ref, *, bb, r):
    u = _ln(x_ref[...]) * (1.0 + _rows(sc_ref, bb, r)) + _rows(sh_ref, bb, r)
    z_ref[...] = jnp.dot(u.astype(BF16), w_ref[...], preferred_element_type=F32)


def _seq_grid(nb, seq):
    if seq >= ROWS:
        bb, r, nl = 1, ROWS, seq // ROWS
        grid = (nb, nl)
        row_map = lambda b, l: (b * nl + l, 0)
    else:
        bb, r, nl = ROWS // seq, seq, 1
        grid = (nb // bb, 1)
        row_map = lambda b, l: (b, 0)
    seq_map = lambda b, l: (b, 0, 0)
    return bb, r, grid, row_map, seq_map


def _mix_in(x2, sh, sc, w_in_bf, nb, seq):
    t, d = x2.shape
    bb, r, grid, row_map, seq_map = _seq_grid(nb, seq)
    return pl.pallas_call(
        functools.partial(_mix_in_body, bb=bb, r=r),
        grid=grid,
        in_specs=[pl.BlockSpec((ROWS, d), row_map),
                  pl.BlockSpec((bb, 1, d), seq_map),
                  pl.BlockSpec((bb, 1, d), seq_map),
                  pl.BlockSpec((d, d), lambda b, l: (0, 0))],
        out_specs=pl.BlockSpec((ROWS, d), row_map),
        out_shape=jax.ShapeDtypeStruct((t, d), F32),
        compiler_params=_cparams("arbitrary", "arbitrary"),
        name="mix_in",
    )(x2, sh, sc, w_in_bf)


def _pool_body(z_ref, pre_ref, wp_ref, ps_ref, gp_ref, o_ref, carry_ref, *, bb, r, start_pos):
    li = pl.program_id(1)
    c = z_ref.shape[-1]
    gw = c // len(POOL_WINDOWS)
    rp = POOL_HALO + r

    @pl.when(li == 0)
    def _():
        carry_ref[...] = pre_ref[...]

    za = z_ref[...].reshape(bb, r, c)
    xp3 = jnp.concatenate([carry_ref[...], za], axis=1)
    carry_ref[...] = xp3[:, r:, :]
    xp = xp3.reshape(bb * rp, c)
    pos1 = lax.broadcasted_iota(I32, (bb, r, gw), 1) + (start_pos + 1) + li * r
    outs = []
    ssq = jnp.zeros((bb * r, 1), F32)
    for gi, w in enumerate(POOL_WINDOWS):
        cols = slice(gi * gw, (gi + 1) * gw)
        s = xp[:, cols]
        sh = 1
        while sh < w:
            s = s + pltpu.roll(s, sh, 0)
            sh *= 2
        win = s.reshape(bb, rp, gw)[:, POOL_HALO:, :]
        cnt = jnp.minimum(pos1, w).astype(F32)
        d = (win / cnt - za[:, :, cols]).reshape(bb * r, gw)
        y = jnp.dot(d.astype(BF16), wp_ref[gi], preferred_element_type=F32) * ps_ref[:, cols]
        ssq = ssq + jnp.sum(y * y, axis=-1, keepdims=True)
        outs.append(y)
    scale = lax.rsqrt(ssq * (1.0 / c) + LN_EPS)
    for gi, y in enumerate(outs):
        cols = slice(gi * gw, (gi + 1) * gw)
        o_ref[:, cols] = (y * scale * gp_ref[:, cols]).astype(o_ref.dtype)


def _pool(z, prefix16, w_pool_bf, pool_scale, g_pool, nb, seq, start_pos):
    t = z.shape[0]
    c = pool_scale.shape[-1]
    bb, r, grid, row_map, seq_map = _seq_grid(nb, seq)
    const2 = lambda b, l: (0, 0)
    return pl.pallas_call(
        functools.partial(_pool_body, bb=bb, r=r, start_pos=start_pos),
        grid=grid,
        in_specs=[pl.BlockSpec((ROWS, c), row_map),
                  pl.BlockSpec((bb, POOL_HALO, c), seq_map),
                  pl.BlockSpec(w_pool_bf.shape, lambda b, l: (0, 0, 0)),
                  pl.BlockSpec((1, c), const2),
                  pl.BlockSpec((1, c), const2)],
        out_specs=pl.BlockSpec((ROWS, c), row_map),
        out_shape=jax.ShapeDtypeStruct((t, c), BF16),
        scratch_shapes=[pltpu.VMEM((bb, POOL_HALO, c), F32)],
        compiler_params=_cparams("arbitrary", "arbitrary"),
        name="pool",
    )(z, prefix16, w_pool_bf, pool_scale.reshape(1, c), g_pool.reshape(1, c))


def _cmul_add(x_r, x_i, m_r, m_i, y_r, y_i):
    return x_r + m_r * y_r - m_i * y_i, x_i + m_r * y_i + m_i * y_r


def _ssm_body(z_ref, h0r_ref, h0i_ref, bbr_ref, bbi_ref, cr_ref, nci_ref, dsk_ref, pr_ref, pi_ref,
              wg_ref, bg_ref, gs_ref, o_ref, hfr_ref, hfi_ref, hr_s, hi_s, y_s, car_s, cai_s,
              *, bb, r, chained):
    li = pl.program_id(1)
    c = z_ref.shape[-1]
    nstate = hr_s.shape[-1]
    ntile = nstate // SCAN_W
    cw = c // ntile
    u = z_ref[...]
    ub = u.astype(BF16)
    for j in range(ntile):
        sl = slice(j * SCAN_W, (j + 1) * SCAN_W)
        uj = ub[:, j * cw:(j + 1) * cw]
        hr_s[:, sl] = jnp.dot(uj, bbr_ref[j], preferred_element_type=F32)
        hi_s[:, sl] = jnp.dot(uj, bbi_ref[j], preferred_element_type=F32)

    if chained:
        @pl.when(li == 0)
        def _():
            car_s[...] = h0r_ref[0]
            cai_s[...] = h0i_ref[0]

    row = lax.broadcasted_iota(I32, (SUBLANES, SCAN_W), 0)
    bc8 = lambda v: jnp.broadcast_to(v, (SUBLANES, SCAN_W))
    for j in range(ntile):
        sl = slice(j * SCAN_W, (j + 1) * SCAN_W)
        p_r, p_i = pr_ref[:, sl], pi_ref[:, sl]
        steps = []
        k = 1
        while k < SUBLANES:
            keep = row >= k
            steps.append((k, jnp.where(keep, bc8(p_r[k - 1:k]), 0.0), jnp.where(keep, bc8(p_i[k - 1:k]), 0.0)))
            k *= 2

        def tile(i, carry, sl=sl, p_r=p_r, p_i=p_i, steps=steps):
            off = pl.multiple_of(i * SUBLANES, SUBLANES)
            x_r, x_i = hr_s[pl.ds(off, SUBLANES), sl], hi_s[pl.ds(off, SUBLANES), sl]
            for k, m_r, m_i in steps:
                x_r, x_i = _cmul_add(x_r, x_i, m_r, m_i, pltpu.roll(x_r, k, 0), pltpu.roll(x_i, k, 0))
            if chained:
                c_r, c_i = carry
            else:
                c_r, c_i = bc8(h0r_ref[i, :, sl]), bc8(h0i_ref[i, :, sl])
            x_r, x_i = _cmul_add(x_r, x_i, p_r, p_i, c_r, c_i)
            hr_s[pl.ds(off, SUBLANES), sl] = x_r
            hi_s[pl.ds(off, SUBLANES), sl] = x_i
            l_r, l_i = x_r[SUBLANES - 1:SUBLANES], x_i[SUBLANES - 1:SUBLANES]
            if not chained:
                hfr_ref[i, :, sl] = l_r
                hfi_ref[i, :, sl] = l_i
            return bc8(l_r), bc8(l_i)

        if chained:
            init = (bc8(car_s[:, sl]), bc8(cai_s[:, sl]))
        else:
            init = (jnp.zeros((SUBLANES, SCAN_W), F32),) * 2
        e_r, e_i = lax.fori_loop(0, (bb * r) // SUBLANES, tile, init)
        if chained:
            car_s[:, sl] = e_r[0:1]
            cai_s[:, sl] = e_i[0:1]
            hfr_ref[0, :, sl] = e_r[0:1]
            hfi_ref[0, :, sl] = e_i[0:1]

    for j in range(ntile):
        sl = slice(j * SCAN_W, (j + 1) * SCAN_W)
        cs = slice(j * cw, (j + 1) * cw)
        y_s[:, cs] = (jnp.dot(hr_s[:, sl].astype(BF16), cr_ref[j], preferred_element_type=F32)
                      + jnp.dot(hi_s[:, sl].astype(BF16), nci_ref[j], preferred_element_type=F32)
                      + dsk_ref[:, cs] * u[:, cs])
    y = y_s[...]
    g = 0.5 * y * (1.0 + jnp.tanh(0.7978845608028654 * (y + 0.044715 * (y * y * y))))
    gate = jnp.dot(g.astype(BF16), wg_ref[...], preferred_element_type=F32) + bg_ref[...]
    out = g * _sigmoid(gate)
    scale = lax.rsqrt(jnp.mean(out * out, axis=-1, keepdims=True) + LN_EPS)
    o_ref[...] = (out * scale * gs_ref[...]).astype(o_ref.dtype)


def _ssm(z, h0r, h0i, bbr_bd, bbi_bd, cr_bd, nci_bd, d_skip, pw_r, pw_i, w_glu_bf, b_glu, g_ssm, nb, seq):
    t = z.shape[0]
    c = d_skip.shape[-1]
    nstate = pw_r.shape[-1]
    bb, r, grid, row_map, seq_map = _seq_grid(nb, seq)
    chained = bb == 1
    const2 = lambda b, l: (0, 0)
    const3 = lambda b, l: (0, 0, 0)
    full = lambda a: pl.BlockSpec(a.shape, const3 if a.ndim == 3 else const2)
    right_half = (lambda b, l: (row_map(b, l)[0], 1))
    return pl.pallas_call(
        functools.partial(_ssm_body, bb=bb, r=r, chained=chained),
        grid=grid,
        in_specs=[pl.BlockSpec((ROWS, c), right_half),
                  pl.BlockSpec((bb, 1, nstate), seq_map),
                  pl.BlockSpec((bb, 1, nstate), seq_map),
                  full(bbr_bd), full(bbi_bd), full(cr_bd), full(nci_bd),
                  pl.BlockSpec((1, c), const2),
                  full(pw_r), full(pw_i),
                  full(w_glu_bf),
                  pl.BlockSpec((1, c), const2),
                  pl.BlockSpec((1, c), const2)],
        out_specs=[pl.BlockSpec((ROWS, c), row_map),
                   pl.BlockSpec((bb, 1, nstate), seq_map),
                   pl.BlockSpec((bb, 1, nstate), seq_map)],
        out_shape=[jax.ShapeDtypeStruct((t, c), BF16),
                   jax.ShapeDtypeStruct((nb, 1, nstate), F32),
                   jax.ShapeDtypeStruct((nb, 1, nstate), F32)],
        scratch_shapes=[pltpu.VMEM((ROWS, nstate), F32), pltpu.VMEM((ROWS, nstate), F32),
                        pltpu.VMEM((ROWS, c), F32),
                        pltpu.VMEM((1, nstate), F32), pltpu.VMEM((1, nstate), F32)],
        compiler_params=_cparams("arbitrary", "arbitrary"),
        name="ssm",
    )(z, h0r, h0i, bbr_bd, bbi_bd, cr_bd, nci_bd, d_skip.reshape(1, c), pw_r, pw_i,
      w_glu_bf, b_glu.reshape(1, c), g_ssm.reshape(1, c))


def _route(s_t, bias_t):
    e, tn = s_t.shape
    per = e // N_EXPERT_GROUPS
    neg = -jnp.inf
    sb = s_t + bias_t
    rowl = lax.broadcasted_iota(I32, (per, tn), 0)
    gscore = []
    for g in range(N_EXPERT_GROUPS):
        blk = sb[g * per:(g + 1) * per]
        m1 = jnp.max(blk, axis=0, keepdims=True)
        i1 = jnp.min(jnp.where(blk == m1, rowl, per), axis=0, keepdims=True)
        m2 = jnp.max(jnp.where(rowl == i1, neg, blk), axis=0, keepdims=True)
        gscore.append(m1 + m2)
    cur = jnp.concatenate(gscore, axis=0)
    rowg = lax.broadcasted_iota(I32, cur.shape, 0)
    gsel = jnp.zeros(cur.shape, F32)
    for _ in range(TOPK_GROUPS):
        m = jnp.max(cur, axis=0, keepdims=True)
        hit = rowg == jnp.min(jnp.where(cur == m, rowg, N_EXPERT_GROUPS), axis=0, keepdims=True)
        gsel = jnp.where(hit, 1.0, gsel)
        cur = jnp.where(hit, neg, cur)
    cur = jnp.concatenate(
        [jnp.where(jnp.broadcast_to(gsel[g:g + 1], (per, tn)) > 0.0, sb[g * per:(g + 1) * per], neg)
         for g in range(N_EXPERT_GROUPS)], axis=0)
    rowe = lax.broadcasted_iota(I32, (e, tn), 0)
    idxs, vals = [], []
    for _ in range(TOP_K):
        m = jnp.max(cur, axis=0, keepdims=True)
        idx = jnp.min(jnp.where(cur == m, rowe, e), axis=0, keepdims=True)
        hit = rowe == idx
        idxs.append(idx)
        vals.append(jnp.sum(jnp.where(hit, s_t, 0.0), axis=0, keepdims=True))
        cur = jnp.where(hit, neg, cur)
    w = jnp.concatenate(vals, axis=0)
    w = w / jnp.sum(w, axis=0, keepdims=True) * ROUTED_SCALE
    return jnp.concatenate(idxs, axis=0), w


def _out_proj_body(x_ref, ya_ref, ys_ref, g1_ref, sh2_ref, sc2_ref, wo_ref, lg_ref, lb_ref, wrt_ref, rb_ref,
                   x1_ref, h_ref, ei_ref, ew_ref, *, bb, r):
    ca = ya_ref.shape[-1]
    m = (jnp.dot(ya_ref[...], wo_ref[:ca, :], preferred_element_type=F32)
         + jnp.dot(ys_ref[...], wo_ref[ca:, :], preferred_element_type=F32))
    res = DN_ALPHA * x_ref[...] + (1.0 + _rows(g1_ref, bb, r)) * m
    x1 = _ln(res) * lg_ref[...] + lb_ref[...]
    x1_ref[...] = x1
    h = _ln(x1) * (1.0 + _rows(sc2_ref, bb, r)) + _rows(sh2_ref, bb, r)
    h_ref[...] = h
    logit_t = lax.dot_general(wrt_ref[...], h.astype(BF16), (((1,), (1,)), ((), ())),
                              preferred_element_type=F32)
    idx, w = _route(_sigmoid(logit_t), rb_ref[...])
    ei_ref[...] = idx
    ew_ref[...] = w


def _out_proj(x2, ya, ys, g1, sh2, sc2, w_out_bf, ln_g, ln_b, w_router_t_bf, router_bias, nb, seq,
              h_all, t_all, row_off):
    t, d = x2.shape
    ca = ya.shape[-1]
    e = w_router_t_bf.shape[0]
    bb, r, grid, row_map, seq_map = _seq_grid(nb, seq)
    const2 = lambda b, l: (0, 0)
    tok_map = lambda b, l: (0, row_map(b, l)[0])
    blk_off = row_off // ROWS
    h_map = lambda b, l: (row_map(b, l)[0] + blk_off, 0)
    in_specs = [pl.BlockSpec((ROWS, d), row_map),
                pl.BlockSpec((ROWS, ca), row_map),
                pl.BlockSpec((ROWS, ca), row_map),
                pl.BlockSpec((bb, 1, d), seq_map),
                pl.BlockSpec((bb, 1, d), seq_map),
                pl.BlockSpec((bb, 1, d), seq_map),
                pl.BlockSpec((d, d), const2),
                pl.BlockSpec((1, d), const2),
                pl.BlockSpec((1, d), const2),
                pl.BlockSpec((e, d), const2),
                pl.BlockSpec((e, 1), const2)]
    args = [x2, ya, ys, g1, sh2, sc2, w_out_bf, ln_g.reshape(1, d), ln_b.reshape(1, d),
            w_router_t_bf, router_bias.reshape(e, 1)]
    body = functools.partial(_out_proj_body, bb=bb, r=r)
    aliases = {}
    if h_all is not None:
        in_specs.append(pl.BlockSpec(memory_space=pl.ANY))
        args.append(h_all)
        aliases = {len(args) - 1: 1}
        inner = body
        body = lambda *refs: inner(*refs[:11], *refs[12:])
    return pl.pallas_call(
        body,
        grid=grid,
        in_specs=in_specs,
        out_specs=[pl.BlockSpec((ROWS, d), row_map),
                   pl.BlockSpec((ROWS, d), h_map),
                   pl.BlockSpec((TOP_K, ROWS), tok_map),
                   pl.BlockSpec((TOP_K, ROWS), tok_map)],
        out_shape=[jax.ShapeDtypeStruct((t, d), F32),
                   jax.ShapeDtypeStruct((t_all, d), F32),
                   jax.ShapeDtypeStruct((TOP_K, t), I32),
                   jax.ShapeDtypeStruct((TOP_K, t), F32)],
        input_output_aliases=aliases,
        compiler_params=_cparams("arbitrary", "arbitrary"),
        name="out_proj",
    )(*args)


def _experts_body(be_ref, nu_ref, tokc_ref, tokn_ref, dst_ref, h_hbm, wg_ref, wu_ref, wd_ref, y_hbm,
                  xbuf, ybuf, gsem, ssem):
    b = pl.program_id(0)
    n_used = nu_ref[0]
    slot = b % 2
    bm = xbuf.shape[1]

    def gather_start(tok_ref, s):
        for i in range(bm):
            pltpu.make_async_copy(h_hbm.at[pl.ds(tok_ref[0, 0, i], 1), :],
                                  xbuf.at[s, pl.ds(i, 1), :], gsem.at[s]).start()

    def gather_wait(s):
        pltpu.make_async_copy(h_hbm.at[pl.ds(0, bm), :], xbuf.at[s], gsem.at[s]).wait()

    def scatter_wait(s):
        pltpu.make_async_copy(ybuf.at[s], y_hbm.at[pl.ds(0, bm), :], ssem.at[s]).wait()

    @pl.when(b == 0)
    def _():
        gather_start(tokc_ref, 0)

    @pl.when(b + 1 < n_used)
    def _():
        gather_start(tokn_ref, 1 - slot)

    @pl.when(b < n_used)
    def _():
        gather_wait(slot)
        x = xbuf[slot].astype(BF16)
        g = jnp.dot(x, wg_ref[0].astype(BF16), preferred_element_type=F32)
        u = jnp.dot(x, wu_ref[0].astype(BF16), preferred_element_type=F32)
        a = (g * _sigmoid(g) * u).astype(BF16)
        ybuf[slot] = jnp.dot(a, wd_ref[0].astype(BF16), preferred_element_type=F32)
        for i in range(bm):
            pltpu.make_async_copy(ybuf.at[slot, pl.ds(i, 1), :],
                                  y_hbm.at[pl.ds(dst_ref[0, 0, i], 1), :], ssem.at[slot]).start()

        @pl.when(b > 0)
        def _():
            scatter_wait(1 - slot)

        @pl.when(b == n_used - 1)
        def _():
            scatter_wait(slot)


def _experts(h_all, w_gate, w_up, w_down, block_e, n_used, tok_blocks, dst_blocks, n_rows_out):
    t, d = h_all.shape
    e, _, de = w_gate.shape
    nblk = tok_blocks.shape[0]
    bm = MOE_BM
    cur = lambda b, be, nu: (b, 0, 0)
    nxt = lambda b, be, nu: (jnp.minimum(b + 1, nblk - 1), 0, 0)
    wmap = lambda b, be, nu: (be[b], 0, 0)
    smem_blk = lambda m: pl.BlockSpec((1, 1, bm), m, memory_space=pltpu.SMEM)
    grid_spec = pltpu.PrefetchScalarGridSpec(
        num_scalar_prefetch=2,
        grid=(nblk,),
        in_specs=[smem_blk(cur), smem_blk(nxt), smem_blk(cur),
                  pl.BlockSpec(memory_space=pl.ANY),
                  pl.BlockSpec((1, d, de), wmap),
                  pl.BlockSpec((1, d, de), wmap),
                  pl.BlockSpec((1, de, d), wmap)],
        out_specs=pl.BlockSpec(memory_space=pl.ANY),
        scratch_shapes=[pltpu.VMEM((2, bm, d), F32), pltpu.VMEM((2, bm, d), F32),
                        pltpu.SemaphoreType.DMA((2,)), pltpu.SemaphoreType.DMA((2,))],
    )
    return pl.pallas_call(
        _experts_body,
        grid_spec=grid_spec,
        out_shape=jax.ShapeDtypeStruct((n_rows_out, d), F32),
        compiler_params=_cparams("arbitrary"),
        name="experts",
    )(block_e, n_used, tok_blocks, tok_blocks, dst_blocks, h_all, w_gate, w_up, w_down)


def _dispatch(eidx_t, n_experts):
    k, t = eidx_t.shape
    a = t * k
    bm = MOE_BM
    flat_e = eidx_t.T.reshape(a)
    order = jnp.argsort(flat_e).astype(I32)
    se = flat_e[order]
    counts = jnp.zeros((n_experts,), I32).at[flat_e].add(1)
    starts = jnp.cumsum(counts) - counts
    pcounts = (counts + bm - 1) // bm * bm
    pends = jnp.cumsum(pcounts)
    pstarts = pends - pcounts
    dest = pstarts[se] + jnp.arange(a, dtype=I32) - starts[se]
    nblk = -(-a // bm) + n_experts
    p = nblk * bm
    pos = jnp.arange(p, dtype=I32)
    buf_a = (a + pos % (2 * bm)).at[dest].set(order)
    buf_tok = jnp.where(buf_a < a, buf_a // k, 0)
    n_used = (pends[-1] // bm).astype(I32)
    blk = jnp.arange(nblk, dtype=I32)
    block_e = jnp.minimum(jnp.searchsorted(pends, jnp.minimum(blk, n_used - 1) * bm, side='right'),
                          n_experts - 1).astype(I32)
    return (block_e, n_used.reshape(1), buf_tok.reshape(nblk, 1, bm), buf_a.reshape(nblk, 1, bm), a + 2 * bm)


def _combine_body(x1_ref, h_ref, y_ref, w_ref, g2_ref, wsg_ref, wsu_ref, wsd_ref, lg_ref, lb_ref, o_ref,
                  *, bb, r):
    d = x1_ref.shape[-1]
    hb = h_ref[...].astype(BF16)
    g = jnp.dot(hb, wsg_ref[...], preferred_element_type=F32)
    u = jnp.dot(hb, wsu_ref[...], preferred_element_type=F32)
    shared = jnp.dot((g * _sigmoid(g) * u).astype(BF16), wsd_ref[...], preferred_element_type=F32)
    w = w_ref[...]
    routed = w[:, 0:1] * y_ref[:, 0:d]
    for k in range(1, TOP_K):
        routed = routed + w[:, k:k + 1] * y_ref[:, k * d:(k + 1) * d]
    res = DN_ALPHA * x1_ref[...] + (1.0 + _rows(g2_ref, bb, r)) * (routed + shared)
    o_ref[...] = _ln(res) * lg_ref[...] + lb_ref[...]


def _combine(x1, h_all, y_rows8, wsel, g2, wsg_bf, wsu_bf, wsd_bf, ln_g, ln_b, nb, seq, row_off):
    t, d = x1.shape
    ds_ = wsg_bf.shape[1]
    rows = COMB_ROWS
    if seq >= rows:
        bb, r, nl = 1, rows, seq // rows
        grid = (nb, nl)
        row_map = lambda b, l: (b * nl + l, 0)
    else:
        bb, r, nl = rows // seq, seq, 1
        grid = (nb // bb, 1)
        row_map = lambda b, l: (b, 0)
    seq_map = lambda b, l: (b, 0, 0)
    off = row_off // rows
    all_map = lambda b, l: (row_map(b, l)[0] + off, 0)
    const2 = lambda b, l: (0, 0)
    return pl.pallas_call(
        functools.partial(_combine_body, bb=bb, r=r),
        grid=grid,
        in_specs=[pl.BlockSpec((rows, d), row_map),
                  pl.BlockSpec((rows, d), all_map),
                  pl.BlockSpec((rows, TOP_K * d), all_map),
                  pl.BlockSpec((rows, TOP_K), all_map),
                  pl.BlockSpec((bb, 1, d), seq_map),
                  pl.BlockSpec((d, ds_), const2),
                  pl.BlockSpec((d, ds_), const2),
                  pl.BlockSpec((ds_, d), const2),
                  pl.BlockSpec((1, d), const2),
                  pl.BlockSpec((1, d), const2)],
        out_specs=pl.BlockSpec((rows, d), row_map),
        out_shape=jax.ShapeDtypeStruct((t, d), F32),
        compiler_params=_cparams("arbitrary", "arbitrary"),
        name="combine",
    )(x1, h_all, y_rows8, wsel, g2, wsg_bf, wsu_bf, wsd_bf, ln_g.reshape(1, d), ln_b.reshape(1, d))


def kernel(x_prompt, x_sample, state_pool, state_ssm_re, state_ssm_im, c_prompt, c_sample, w_ada, b_ada, w_in, w_pool, pool_scale, A_re, A_im, log_dt, B_re, B_im, C_re, C_im, D_skip, w_glu, b_glu, g_pool, g_ssm, w_out, ln1_g, ln1_b, w_router, router_bias, w_e_gate, w_e_up, w_e_down, w_sh_gate, w_sh_up, w_sh_down, ln2_g, ln2_b):
    bp, lp, d = x_prompt.shape
    bs, ls, _ = x_sample.shape
    depth = w_ada.shape[0]
    assert depth == DEPTH == 1
    tp, ts = bp * lp, bs * ls
    t_all = tp + ts
    l = 0
    c_pool = pool_scale.shape[-1]
    n_groups, n_state = A_re.shape[1], A_re.shape[2]
    nstate = n_groups * n_state
    n_experts = w_router.shape[-1]
    gpt = SCAN_W // n_state

    c_all = jnp.concatenate([c_prompt, c_sample], axis=0)
    pad = (-c_all.shape[0]) % SUBLANES
    c_all = jnp.pad(c_all, ((0, pad), (0, 0)))
    mod = _adaln(c_all, w_ada[l], b_ada[l]).reshape(c_all.shape[0], 6, 1, d)
    mod_p = [mod[:bp, i] for i in range(6)]
    mod_s = [mod[bp:bp + bs, i] for i in range(6)]

    w_in_bf = w_in[l].astype(BF16)
    w_pool_bf = w_pool[l].astype(BF16)
    w_glu_bf = w_glu[l].astype(BF16)
    w_out_bf = w_out[l].astype(BF16)
    w_router_t_bf = w_router[l].T.astype(BF16)
    wsg_bf, wsu_bf, wsd_bf = w_sh_gate[l].astype(BF16), w_sh_up[l].astype(BF16), w_sh_down[l].astype(BF16)
    pw_r, pw_i, bb_r, bb_i = _ssm_prep(A_re[l], A_im[l], log_dt[l], B_re[l], B_im[l])
    bbr_bd = _block_diag(jnp.swapaxes(bb_r, 1, 2), gpt).astype(BF16)
    bbi_bd = _block_diag(jnp.swapaxes(bb_i, 1, 2), gpt).astype(BF16)
    cr_bd = _block_diag(jnp.swapaxes(C_re[l], 1, 2), gpt).astype(BF16)
    nci_bd = _block_diag(jnp.swapaxes(-C_im[l], 1, 2), gpt).astype(BF16)
    d_skip = D_skip[l].reshape(-1)

    groups = [
        dict(x=x_prompt.reshape(tp, d), nb=bp, seq=lp, mod=mod_p, start=0, row_off=0,
             prefix=jnp.zeros((bp, POOL_HALO, c_pool), F32),
             h0r=jnp.zeros((bp, 1, nstate), F32), h0i=jnp.zeros((bp, 1, nstate), F32)),
        dict(x=x_sample.reshape(ts, d), nb=bs, seq=ls, mod=mod_s, start=PAST_LEN, row_off=tp,
             prefix=jnp.pad(state_pool[l], ((0, 0), (POOL_HALO - state_pool.shape[2], 0), (0, 0))),
             h0r=state_ssm_re[l].reshape(bs, 1, nstate), h0i=state_ssm_im[l].reshape(bs, 1, nstate)),
    ]

    h_all = None
    for gr in groups:
        sh1, sc1, g1, sh2, sc2, g2 = gr['mod']
        nb, seq = gr['nb'], gr['seq']
        z = _mix_in(gr['x'], sh1, sc1, w_in_bf, nb, seq)
        ya = _pool(z, gr['prefix'], w_pool_bf, pool_scale[l], g_pool[l], nb, seq, gr['start'])
        ys, hfr, hfi = _ssm(z, gr['h0r'], gr['h0i'], bbr_bd, bbi_bd, cr_bd, nci_bd, d_skip, pw_r, pw_i,
                            w_glu_bf, b_glu[l], g_ssm[l], nb, seq)
        x1, h_all, eidx_t, ew_t = _out_proj(gr['x'], ya, ys, g1, sh2, sc2, w_out_bf, ln1_g[l], ln1_b[l],
                                            w_router_t_bf, router_bias[l], nb, seq,
                                            h_all, t_all, gr['row_off'])
        gr.update(z=z, x1=x1, eidx_t=eidx_t, ew_t=ew_t, hfr=hfr, hfi=hfi)

    eidx_t = jnp.concatenate([gr['eidx_t'] for gr in groups], axis=1)
    wsel = jnp.concatenate([gr['ew_t'] for gr in groups], axis=1).T
    block_e, n_used, tok_blocks, dst_blocks, n_rows_out = _dispatch(eidx_t, n_experts)
    y_rows = _experts(h_all, w_e_gate[l], w_e_up[l], w_e_down[l], block_e, n_used, tok_blocks, dst_blocks,
                      n_rows_out)
    y_rows8 = y_rows.reshape(n_rows_out // TOP_K, TOP_K * d)

    outs = []
    for gr in groups:
        outs.append(_combine(gr['x1'], h_all, y_rows8, wsel, gr['mod'][5], wsg_bf, wsu_bf, wsd_bf,
                             ln2_g[l], ln2_b[l], gr['nb'], gr['seq'], gr['row_off']))
    y_prompt = outs[0].reshape(bp, lp, d)
    y_sample = outs[1].reshape(bs, ls, d)

    nbuf = state_pool.shape[2]
    zp = groups[0]['z'].reshape(bp, lp, d)[:, :, :c_pool]
    zs = groups[1]['z'].reshape(bs, ls, d)[:, :, :c_pool]
    pool_p = zp[:, lp - nbuf:, :][None]
    pool_s = jnp.concatenate([state_pool[l], zs], axis=1)[:, -nbuf:, :][None]
    st = lambda a, nb: a.reshape(nb, n_groups, n_state)[None]
    return (y_prompt, y_sample, pool_p, pool_s,
            st(groups[0]['hfr'], bp), st(groups[0]['hfi'], bp),
            st(groups[1]['hfr'], bs), st(groups[1]['hfi'], bs))
```

```python
import functools

import jax
import jax.numpy as jnp
from jax import lax
from jax.experimental import pallas as pl
from jax.experimental.pallas import tpu as pltpu

F32 = jnp.float32
BF16 = jnp.bfloat16
I32 = jnp.int32

DEPTH = 1
PAST_LEN = 16384
POOL_WINDOWS = (2, 4, 8, 16)
POOL_HALO = 16
SSM_P = 16
SSM_N = 64
N_EXPERT_GROUPS = 8
TOPK_GROUPS = 4
TOP_K = 8
ROUTED_SCALE = 2.5
LN_EPS = 1e-5
DN_ALPHA = (2.0 * DEPTH) ** 0.25

ROWS = 256
SUBLANES = 8
LANES = 128
SCAN_W = 512
MOE_BM = 256
COMB_ROWS = 128
VMEM_LIMIT = 56 * 1024 * 1024


def _cparams(*sem):
    return pltpu.CompilerParams(dimension_semantics=sem, vmem_limit_bytes=VMEM_LIMIT)


def _ln(x):
    xc = x - jnp.mean(x, axis=-1, keepdims=True)
    return xc * lax.rsqrt(jnp.mean(xc * xc, axis=-1, keepdims=True) + LN_EPS)


def _rows(m_ref, bb, r):
    m = m_ref[...]
    c = m.shape[-1]
    return jnp.broadcast_to(m, (bb, r, c)).reshape(bb * r, c)


def _sigmoid(x):
    return 1.0 / (1.0 + jnp.exp(-x))


def _store_slabs(ref, x, lead=()):
    for s in range(x.shape[-1] // LANES):
        ref[lead + (slice(None), s, slice(None))] = x[:, s * LANES:(s + 1) * LANES]


def _load_slabs(ref, lead=()):
    n = ref.shape[-2]
    return jnp.concatenate([ref[lead + (slice(None), s, slice(None))] for s in range(n)], axis=-1)


def _adaln_body(c_ref, w_ref, b_ref, o_ref):
    c = c_ref[...]
    s = (c * _sigmoid(c)).astype(BF16)
    o_ref[...] = jnp.dot(s, w_ref[...].astype(BF16), preferred_element_type=F32) + b_ref[...]


def _adaln(c_all, w_ada, b_ada):
    bc, d = c_all.shape
    n = w_ada.shape[1]
    tn = 1024
    return pl.pallas_call(
        _adaln_body,
        grid=(n // tn,),
        in_specs=[pl.BlockSpec((bc, d), lambda j: (0, 0)),
                  pl.BlockSpec((d, tn), lambda j: (0, j)),
                  pl.BlockSpec((1, tn), lambda j: (0, j))],
        out_specs=pl.BlockSpec((bc, tn), lambda j: (0, j)),
        out_shape=jax.ShapeDtypeStruct((bc, n), F32),
        compiler_params=_cparams("arbitrary"),
        name="adaln",
    )(c_all, w_ada, b_ada.reshape(1, n))


def _ssm_prep_body(ar_ref, ai_ref, dt_ref, ar16_ref, ai16_ref, dt16_ref, br_ref, bi_ref,
                   pr_ref, pi_ref, bbr_ref, bbi_ref):
    def zoh(a_r, a_i, dt):
        mag = jnp.exp(a_r * dt)
        ab_r, ab_i = mag * jnp.cos(a_i * dt), mag * jnp.sin(a_i * dt)
        den = a_r * a_r + a_i * a_i
        nr = ab_r - 1.0
        return ab_r, ab_i, (nr * a_r + ab_i * a_i) / den, (ab_i * a_r - nr * a_i) / den

    ab_r, ab_i, _, _ = zoh(ar_ref[...], ai_ref[...], jnp.exp(dt_ref[...]))
    p_r, p_i = ab_r, ab_i
    for k in range(SUBLANES):
        pr_ref[k] = p_r
        pi_ref[k] = p_i
        p_r, p_i = p_r * ab_r - p_i * ab_i, p_r * ab_i + p_i * ab_r
    _, _, f_r, f_i = zoh(ar16_ref[...], ai16_ref[...], jnp.exp(dt16_ref[...]))
    b_r, b_i = br_ref[...], bi_ref[...]
    bbr_ref[...] = f_r * b_r - f_i * b_i
    bbi_ref[...] = f_r * b_i + f_i * b_r


def _ssm_prep(a_re, a_im, log_dt, b_re, b_im):
    g, n = a_re.shape
    p = b_re.shape[-1]
    dt = jnp.broadcast_to(log_dt[:, None], (g, n))
    rep = lambda a: jnp.repeat(a, p, axis=-1)
    outs = pl.pallas_call(
        _ssm_prep_body,
        out_shape=(jax.ShapeDtypeStruct((SUBLANES, g, n), F32), jax.ShapeDtypeStruct((SUBLANES, g, n), F32),
                   jax.ShapeDtypeStruct((g, n * p), F32), jax.ShapeDtypeStruct((g, n * p), F32)),
        name="ssm_prep",
    )(a_re, a_im, dt, rep(a_re), rep(a_im), rep(dt), b_re.reshape(g, n * p), b_im.reshape(g, n * p))
    pw_r, pw_i, bb_r, bb_i = outs
    return (pw_r.reshape(SUBLANES, g * n), pw_i.reshape(SUBLANES, g * n),
            bb_r.reshape(g, n, p), bb_i.reshape(g, n, p))


def _block_diag(w, gpt):
    g, a, b = w.shape
    w4 = w.reshape(g // gpt, gpt, a, b)
    eye = jnp.eye(gpt, dtype=w.dtype)
    return jnp.einsum('jgab,gh->jgahb', w4, eye).reshape(g // gpt, gpt * a, gpt * b)


def _mix_in_body(x_ref, sh_ref, sc_ref, w_ref, z_ref, *, bb, r):
    u = _ln(x_ref[...]) * (1.0 + _rows(sc_ref, bb, r)) + _rows(sh_ref, bb, r)
    z_ref[...] = jnp.dot(u.astype(BF16), w_ref[...], preferred_element_type=F32)


def _seq_grid(nb, seq):
    if seq >= ROWS:
        bb, r, nl = 1, ROWS, seq // ROWS
        grid = (nb, nl)
        row_map = lambda b, l: (b * nl + l, 0)
    else:
        bb, r, nl = ROWS // seq, seq, 1
        grid = (nb // bb, 1)
        row_map = lambda b, l: (b, 0)
    seq_map = lambda b, l: (b, 0, 0)
    return bb, r, grid, row_map, seq_map


def _mix_in(x2, sh, sc, w_in_bf, nb, seq):
    t, d = x2.shape
    bb, r, grid, row_map, seq_map = _seq_grid(nb, seq)
    return pl.pallas_call(
        functools.partial(_mix_in_body, bb=bb, r=r),
        grid=grid,
        in_specs=[pl.BlockSpec((ROWS, d), row_map),
                  pl.BlockSpec((bb, 1, d), seq_map),
                  pl.BlockSpec((bb, 1, d), seq_map),
                  pl.BlockSpec((d, d), lambda b, l: (0, 0))],
        out_specs=pl.BlockSpec((ROWS, d), row_map),
        out_shape=jax.ShapeDtypeStruct((t, d), F32),
        compiler_params=_cparams("arbitrary", "arbitrary"),
        name="mix_in",
    )(x2, sh, sc, w_in_bf)


def _pool_body(z_ref, pre_ref, wp_ref, ps_ref, gp_ref, o_ref, carry_ref, *, bb, r, start_pos):
    li = pl.program_id(1)
    c = z_ref.shape[-1]
    gw = c // len(POOL_WINDOWS)
    rp = POOL_HALO + r

    @pl.when(li == 0)
    def _():
        carry_ref[...] = pre_ref[...]

    za = z_ref[...].reshape(bb, r, c)
    xp3 = jnp.concatenate([carry_ref[...], za], axis=1)
    carry_ref[...] = xp3[:, r:, :]
    xp = xp3.reshape(bb * rp, c)
    pos1 = lax.broadcasted_iota(I32, (bb, r, gw), 1) + (start_pos + 1) + li * r
    outs = []
    ssq = jnp.zeros((bb * r, 1), F32)
    for gi, w in enumerate(POOL_WINDOWS):
        cols = slice(gi * gw, (gi + 1) * gw)
        s = xp[:, cols]
        sh = 1
        while sh < w:
            s = s + pltpu.roll(s, sh, 0)
            sh *= 2
        win = s.reshape(bb, rp, gw)[:, POOL_HALO:, :]
        cnt = jnp.minimum(pos1, w).astype(F32)
        d = (win / cnt - za[:, :, cols]).reshape(bb * r, gw)
        y = jnp.dot(d.astype(BF16), wp_ref[gi], preferred_element_type=F32) * ps_ref[:, cols]
        ssq = ssq + jnp.sum(y * y, axis=-1, keepdims=True)
        outs.append(y)
    scale = lax.rsqrt(ssq * (1.0 / c) + LN_EPS)
    for gi, y in enumerate(outs):
        cols = slice(gi * gw, (gi + 1) * gw)
        o_ref[:, cols] = (y * scale * gp_ref[:, cols]).astype(o_ref.dtype)


def _pool(z, prefix16, w_pool_bf, pool_scale, g_pool, nb, seq, start_pos):
    t = z.shape[0]
    c = pool_scale.shape[-1]
    bb, r, grid, row_map, seq_map = _seq_grid(nb, seq)
    const2 = lambda b, l: (0, 0)
    return pl.pallas_call(
        functools.partial(_pool_body, bb=bb, r=r, start_pos=start_pos),
        grid=grid,
        in_specs=[pl.BlockSpec((ROWS, c), row_map),
                  pl.BlockSpec((bb, POOL_HALO, c), seq_map),
                  pl.BlockSpec(w_pool_bf.shape, lambda b, l: (0, 0, 0)),
                  pl.BlockSpec((1, c), const2),
                  pl.BlockSpec((1, c), const2)],
        out_specs=pl.BlockSpec((ROWS, c), row_map),
        out_shape=jax.ShapeDtypeStruct((t, c), BF16),
        scratch_shapes=[pltpu.VMEM((bb, POOL_HALO, c), F32)],
        compiler_params=_cparams("arbitrary", "arbitrary"),
        name="pool",
    )(z, prefix16, w_pool_bf, pool_scale.reshape(1, c), g_pool.reshape(1, c))


def _cmul_add(x_r, x_i, m_r, m_i, y_r, y_i):
    return x_r + m_r * y_r - m_i * y_i, x_i + m_r * y_i + m_i * y_r


def _ssm_body(z_ref, h0r_ref, h0i_ref, bbr_ref, bbi_ref, cr_ref, nci_ref, dsk_ref, pr_ref, pi_ref,
              wg_ref, bg_ref, gs_ref, o_ref, hfr_ref, hfi_ref, hr_s, hi_s, y_s, car_s, cai_s,
              *, bb, r, chained):
    li = pl.program_id(1)
    c = z_ref.shape[-1]
    nstate = hr_s.shape[-1]
    ntile = nstate // SCAN_W
    cw = c // ntile
    u = z_ref[...]
    ub = u.astype(BF16)
    for j in range(ntile):
        sl = slice(j * SCAN_W, (j + 1) * SCAN_W)
        uj = ub[:, j * cw:(j + 1) * cw]
        hr_s[:, sl] = jnp.dot(uj, bbr_ref[j], preferred_element_type=F32)
        hi_s[:, sl] = jnp.dot(uj, bbi_ref[j], preferred_element_type=F32)

    if chained:
        @pl.when(li == 0)
        def _():
            car_s[...] = h0r_ref[0]
            cai_s[...] = h0i_ref[0]

    row = lax.broadcasted_iota(I32, (SUBLANES, SCAN_W), 0)
    bc8 = lambda v: jnp.broadcast_to(v, (SUBLANES, SCAN_W))
    for j in range(ntile):
        sl = slice(j * SCAN_W, (j + 1) * SCAN_W)
        p_r, p_i = pr_ref[:, sl], pi_ref[:, sl]
        steps = []
        k = 1
        while k < SUBLANES:
            keep = row >= k
            steps.append((k, jnp.where(keep, bc8(p_r[k - 1:k]), 0.0), jnp.where(keep, bc8(p_i[k - 1:k]), 0.0)))
            k *= 2

        def tile(i, carry, sl=sl, p_r=p_r, p_i=p_i, steps=steps):
            off = pl.multiple_of(i * SUBLANES, SUBLANES)
            x_r, x_i = hr_s[pl.ds(off, SUBLANES), sl], hi_s[pl.ds(off, SUBLANES), sl]
            for k, m_r, m_i in steps:
                x_r, x_i = _cmul_add(x_r, x_i, m_r, m_i, pltpu.roll(x_r, k, 0), pltpu.roll(x_i, k, 0))
            if chained:
                c_r, c_i = carry
            else:
                c_r, c_i = bc8(h0r_ref[i, :, sl]), bc8(h0i_ref[i, :, sl])
            x_r, x_i = _cmul_add(x_r, x_i, p_r, p_i, c_r, c_i)
            hr_s[pl.ds(off, SUBLANES), sl] = x_r
            hi_s[pl.ds(off, SUBLANES), sl] = x_i
            l_r, l_i = x_r[SUBLANES - 1:SUBLANES], x_i[SUBLANES - 1:SUBLANES]
            if not chained:
                hfr_ref[i, :, sl] = l_r
                hfi_ref[i, :, sl] = l_i
            return bc8(l_r), bc8(l_i)

        if chained:
            init = (bc8(car_s[:, sl]), bc8(cai_s[:, sl]))
        else:
            init = (jnp.zeros((SUBLANES, SCAN_W), F32),) * 2
        e_r, e_i = lax.fori_loop(0, (bb * r) // SUBLANES, tile, init)
        if chained:
            car_s[:, sl] = e_r[0:1]
            cai_s[:, sl] = e_i[0:1]
            hfr_ref[0, :, sl] = e_r[0:1]
            hfi_ref[0, :, sl] = e_i[0:1]

    for j in range(ntile):
        sl = slice(j * SCAN_W, (j + 1) * SCAN_W)
        cs = slice(j * cw, (j + 1) * cw)
        y_s[:, cs] = (jnp.dot(hr_s[:, sl].astype(BF16), cr_ref[j], preferred_element_type=F32)
                      + jnp.dot(hi_s[:, sl].astype(BF16), nci_ref[j], preferred_element_type=F32)
                      + dsk_ref[:, cs] * u[:, cs])
    y = y_s[...]
    g = 0.5 * y * (1.0 + jnp.tanh(0.7978845608028654 * (y + 0.044715 * (y * y * y))))
    gate = jnp.dot(g.astype(BF16), wg_ref[...], preferred_element_type=F32) + bg_ref[...]
    out = g * _sigmoid(gate)
    scale = lax.rsqrt(jnp.mean(out * out, axis=-1, keepdims=True) + LN_EPS)
    o_ref[...] = (out * scale * gs_ref[...]).astype(o_ref.dtype)


def _ssm(z, h0r, h0i, bbr_bd, bbi_bd, cr_bd, nci_bd, d_skip, pw_r, pw_i, w_glu_bf, b_glu, g_ssm, nb, seq):
    t = z.shape[0]
    c = d_skip.shape[-1]
    nstate = pw_r.shape[-1]
    bb, r, grid, row_map, seq_map = _seq_grid(nb, seq)
    chained = bb == 1
    const2 = lambda b, l: (0, 0)
    const3 = lambda b, l: (0, 0, 0)
    full = lambda a: pl.BlockSpec(a.shape, const3 if a.ndim == 3 else const2)
    right_half = (lambda b, l: (row_map(b, l)[0], 1))
    return pl.pallas_call(
        functools.partial(_ssm_body, bb=bb, r=r, chained=chained),
        grid=grid,
        in_specs=[pl.BlockSpec((ROWS, c), right_half),
                  pl.BlockSpec((bb, 1, nstate), seq_map),
                  pl.BlockSpec((bb, 1, nstate), seq_map),
                  full(bbr_bd), full(bbi_bd), full(cr_bd), full(nci_bd),
                  pl.BlockSpec((1, c), const2),
                  full(pw_r), full(pw_i),
                  full(w_glu_bf),
                  pl.BlockSpec((1, c), const2),
                  pl.BlockSpec((1, c), const2)],
        out_specs=[pl.BlockSpec((ROWS, c), row_map),
                   pl.BlockSpec((bb, 1, nstate), seq_map),
                   pl.BlockSpec((bb, 1, nstate), seq_map)],
        out_shape=[jax.ShapeDtypeStruct((t, c), BF16),
                   jax.ShapeDtypeStruct((nb, 1, nstate), F32),
                   jax.ShapeDtypeStruct((nb, 1, nstate), F32)],
        scratch_shapes=[pltpu.VMEM((ROWS, nstate), F32), pltpu.VMEM((ROWS, nstate), F32),
                        pltpu.VMEM((ROWS, c), F32),
                        pltpu.VMEM((1, nstate), F32), pltpu.VMEM((1, nstate), F32)],
        compiler_params=_cparams("arbitrary", "arbitrary"),
        name="ssm",
    )(z, h0r, h0i, bbr_bd, bbi_bd, cr_bd, nci_bd, d_skip.reshape(1, c), pw_r, pw_i,
      w_glu_bf, b_glu.reshape(1, c), g_ssm.reshape(1, c))


def _route(s_t, bias_t):
    e, tn = s_t.shape
    per = e // N_EXPERT_GROUPS
    neg = -jnp.inf
    sb = s_t + bias_t
    rowl = lax.broadcasted_iota(I32, (per, tn), 0)
    gscore = []
    for g in range(N_EXPERT_GROUPS):
        blk = sb[g * per:(g + 1) * per]
        m1 = jnp.max(blk, axis=0, keepdims=True)
        i1 = jnp.min(jnp.where(blk == m1, rowl, per), axis=0, keepdims=True)
        m2 = jnp.max(jnp.where(rowl == i1, neg, blk), axis=0, keepdims=True)
        gscore.append(m1 + m2)
    cur = jnp.concatenate(gscore, axis=0)
    rowg = lax.broadcasted_iota(I32, cur.shape, 0)
    gsel = jnp.zeros(cur.shape, F32)
    for _ in range(TOPK_GROUPS):
        m = jnp.max(cur, axis=0, keepdims=True)
        hit = rowg == jnp.min(jnp.where(cur == m, rowg, N_EXPERT_GROUPS), axis=0, keepdims=True)
        gsel = jnp.where(hit, 1.0, gsel)
        cur = jnp.where(hit, neg, cur)
    cur = jnp.concatenate(
        [jnp.where(jnp.broadcast_to(gsel[g:g + 1], (per, tn)) > 0.0, sb[g * per:(g + 1) * per], neg)
         for g in range(N_EXPERT_GROUPS)], axis=0)
    rowe = lax.broadcasted_iota(I32, (e, tn), 0)
    idxs, vals = [], []
    for _ in range(TOP_K):
        m = jnp.max(cur, axis=0, keepdims=True)
        idx = jnp.min(jnp.where(cur == m, rowe, e), axis=0, keepdims=True)
        hit = rowe == idx
        idxs.append(idx)
        vals.append(jnp.sum(jnp.where(hit, s_t, 0.0), axis=0, keepdims=True))
        cur = jnp.where(hit, neg, cur)
    w = jnp.concatenate(vals, axis=0)
    w = w / jnp.sum(w, axis=0, keepdims=True) * ROUTED_SCALE
    return jnp.concatenate(idxs, axis=0), w


def _out_proj_body(x_ref, ya_ref, ys_ref, g1_ref, sh2_ref, sc2_ref, wo_ref, lg_ref, lb_ref, wrt_ref, rb_ref,
                   x1_ref, h_ref, ei_ref, ew_ref, *, bb, r):
    ca = ya_ref.shape[-1]
    m = (jnp.dot(ya_ref[...], wo_ref[:ca, :], preferred_element_type=F32)
         + jnp.dot(ys_ref[...], wo_ref[ca:, :], preferred_element_type=F32))
    res = DN_ALPHA * x_ref[...] + (1.0 + _rows(g1_ref, bb, r)) * m
    x1 = _ln(res) * lg_ref[...] + lb_ref[...]
    x1_ref[...] = x1
    h = _ln(x1) * (1.0 + _rows(sc2_ref, bb, r)) + _rows(sh2_ref, bb, r)
    _store_slabs(h_ref, h)
    logit_t = lax.dot_general(wrt_ref[...], h.astype(BF16), (((1,), (1,)), ((), ())),
                              preferred_element_type=F32)
    idx, w = _route(_sigmoid(logit_t), rb_ref[...])
    ei_ref[...] = idx
    ew_ref[...] = w


def _out_proj(x2, ya, ys, g1, sh2, sc2, w_out_bf, ln_g, ln_b, w_router_t_bf, router_bias, nb, seq,
              h_all, t_all, row_off):
    t, d = x2.shape
    ca = ya.shape[-1]
    e = w_router_t_bf.shape[0]
    bb, r, grid, row_map, seq_map = _seq_grid(nb, seq)
    const2 = lambda b, l: (0, 0)
    tok_map = lambda b, l: (0, row_map(b, l)[0])
    blk_off = row_off // ROWS
    h_map = lambda b, l: (row_map(b, l)[0] + blk_off, 0, 0)
    in_specs = [pl.BlockSpec((ROWS, d), row_map),
                pl.BlockSpec((ROWS, ca), row_map),
                pl.BlockSpec((ROWS, ca), row_map),
                pl.BlockSpec((bb, 1, d), seq_map),
                pl.BlockSpec((bb, 1, d), seq_map),
                pl.BlockSpec((bb, 1, d), seq_map),
                pl.BlockSpec((d, d), const2),
                pl.BlockSpec((1, d), const2),
                pl.BlockSpec((1, d), const2),
                pl.BlockSpec((e, d), const2),
                pl.BlockSpec((e, 1), const2)]
    args = [x2, ya, ys, g1, sh2, sc2, w_out_bf, ln_g.reshape(1, d), ln_b.reshape(1, d),
            w_router_t_bf, router_bias.reshape(e, 1)]
    body = functools.partial(_out_proj_body, bb=bb, r=r)
    aliases = {}
    if h_all is not None:
        in_specs.append(pl.BlockSpec(memory_space=pl.ANY))
        args.append(h_all)
        aliases = {len(args) - 1: 1}
        inner = body
        body = lambda *refs: inner(*refs[:11], *refs[12:])
    return pl.pallas_call(
        body,
        grid=grid,
        in_specs=in_specs,
        out_specs=[pl.BlockSpec((ROWS, d), row_map),
                   pl.BlockSpec((ROWS, d // LANES, LANES), h_map),
                   pl.BlockSpec((TOP_K, ROWS), tok_map),
                   pl.BlockSpec((TOP_K, ROWS), tok_map)],
        out_shape=[jax.ShapeDtypeStruct((t, d), F32),
                   jax.ShapeDtypeStruct((t_all, d // LANES, LANES), F32),
                   jax.ShapeDtypeStruct((TOP_K, t), I32),
                   jax.ShapeDtypeStruct((TOP_K, t), F32)],
        input_output_aliases=aliases,
        compiler_params=_cparams("arbitrary", "arbitrary"),
        name="out_proj",
    )(*args)


def _experts_body(pb_ref, pe_ref, plo_ref, pfl_ref, np_ref, tokc_ref, tokn_ref, dst_ref, h_hbm,
                  wg_ref, wu_ref, wd_ref, y_hbm, xbuf, ybuf, y_s, gsem, ssem, *, nblocks):
    p = pl.program_id(0)
    n_pairs = np_ref[0]
    valid = p < n_pairs
    b = pb_ref[p]
    slot = b % 2
    first = (pfl_ref[p] & 1) == 1
    last = (pfl_ref[p] & 2) == 2
    bm = xbuf.shape[1]

    def gather_start(tok_ref, s):
        for i in range(bm):
            pltpu.make_async_copy(h_hbm.at[tok_ref[0, 0, i]], xbuf.at[s, i], gsem.at[s]).start()

    def gather_wait(s):
        pltpu.make_async_copy(h_hbm.at[pl.ds(0, bm)], xbuf.at[s], gsem.at[s]).wait()

    def scatter_wait(s):
        pltpu.make_async_copy(ybuf.at[s], y_hbm.at[pl.ds(0, bm)], ssem.at[0]).wait()

    @pl.when(p == 0)
    def _():
        gather_start(tokc_ref, 0)

    @pl.when(jnp.logical_and(valid, first))
    def _():
        gather_wait(slot)

        @pl.when(b + 1 < nblocks)
        def _():
            gather_start(tokn_ref, 1 - slot)

    @pl.when(valid)
    def _():
        x = _load_slabs(xbuf, (slot,)).astype(BF16)
        g = jnp.dot(x, wg_ref[0].astype(BF16), preferred_element_type=F32)
        u = jnp.dot(x, wu_ref[0].astype(BF16), preferred_element_type=F32)
        a = (g * _sigmoid(g) * u).astype(BF16)
        y_s[...] = jnp.dot(a, wd_ref[0].astype(BF16), preferred_element_type=F32)

        @pl.when(first)
        def _():
            _store_slabs(ybuf, y_s[...], (slot,))

        @pl.when(jnp.logical_not(first))
        def _():
            mine = lax.broadcasted_iota(I32, (bm, LANES), 0) >= plo_ref[p]
            for s in range(y_s.shape[-1] // LANES):
                ybuf[slot, :, s, :] = jnp.where(mine, y_s[:, s * LANES:(s + 1) * LANES], ybuf[slot, :, s, :])

        @pl.when(last)
        def _():
            @pl.when(b > 0)
            def _():
                scatter_wait(1 - slot)

            for i in range(bm):
                pltpu.make_async_copy(ybuf.at[slot, i], y_hbm.at[dst_ref[0, 0, i]], ssem.at[0]).start()

            @pl.when(p == n_pairs - 1)
            def _():
                scatter_wait(slot)


def _experts(h3, w_gate, w_up, w_down, pairs, tok_blocks, dst_blocks):
    t, ns, _ = h3.shape
    d = ns * LANES
    e, _, de = w_gate.shape
    nblk = tok_blocks.shape[0]
    bm = MOE_BM
    pb, pe, plo, pfl, n_pairs = pairs
    cur = lambda p, pb, pe, plo, pfl, n: (pb[p], 0, 0)
    nxt = lambda p, pb, pe, plo, pfl, n: (jnp.minimum(pb[p] + 1, nblk - 1), 0, 0)
    wmap = lambda p, pb, pe, plo, pfl, n: (pe[p], 0, 0)
    smem_blk = lambda m: pl.BlockSpec((1, 1, bm), m, memory_space=pltpu.SMEM)
    grid_spec = pltpu.PrefetchScalarGridSpec(
        num_scalar_prefetch=5,
        grid=(pb.shape[0],),
        in_specs=[smem_blk(cur), smem_blk(nxt), smem_blk(cur),
                  pl.BlockSpec(memory_space=pl.ANY),
                  pl.BlockSpec((1, d, de), wmap),
                  pl.BlockSpec((1, d, de), wmap),
                  pl.BlockSpec((1, de, d), wmap)],
        out_specs=pl.BlockSpec(memory_space=pl.ANY),
        scratch_shapes=[pltpu.VMEM((2, bm, ns, LANES), F32), pltpu.VMEM((2, bm, ns, LANES), F32),
                        pltpu.VMEM((bm, d), F32),
                        pltpu.SemaphoreType.DMA((2,)), pltpu.SemaphoreType.DMA((1,))],
    )
    return pl.pallas_call(
        functools.partial(_experts_body, nblocks=nblk),
        grid_spec=grid_spec,
        out_shape=jax.ShapeDtypeStruct((nblk * bm, ns, LANES), F32),
        compiler_params=_cparams("arbitrary"),
        name="experts",
    )(pb, pe, plo, pfl, n_pairs, tok_blocks, tok_blocks, dst_blocks, h3, w_gate, w_up, w_down)


def _dispatch(eidx_t, n_experts):
    k, t = eidx_t.shape
    a = t * k
    bm = MOE_BM
    nb = a // bm
    flat_e = eidx_t.reshape(a)
    se, order = lax.sort((flat_e, lax.iota(I32, a)), num_keys=1)
    tok = order % t
    ex = lax.iota(I32, n_experts)
    starts = jnp.sum((se[None, :] < ex[:, None]).astype(I32), axis=1)
    seb = se.reshape(nb, bm)
    e_lo, e_hi = seb[:, 0], seb[:, bm - 1]
    npair = e_hi - e_lo + 1
    cum = jnp.cumsum(npair)
    off = cum - npair
    p = lax.iota(I32, nb + n_experts)
    pb = jnp.minimum(jnp.sum((cum[None, :] <= p[:, None]).astype(I32), axis=1), nb - 1)
    pe = jnp.minimum(e_lo[pb] + p - off[pb], e_hi[pb])
    plo = jnp.clip(starts[pe] - pb * bm, 0, bm)
    pfl = (p == off[pb]).astype(I32) + 2 * (p == cum[pb] - 1).astype(I32)
    pairs = (pb.astype(I32), pe.astype(I32), plo.astype(I32), pfl, cum[-1:].astype(I32))
    return pairs, tok.reshape(nb, 1, bm), order.reshape(nb, 1, bm)


def _combine_body(x1_ref, h_ref, w_ref, g2_ref, wsg_ref, wsu_ref, wsd_ref, lg_ref, lb_ref, *rest, bb, r):
    y_refs, o_ref = rest[:TOP_K], rest[TOP_K]
    hb = _load_slabs(h_ref).astype(BF16)
    g = jnp.dot(hb, wsg_ref[...], preferred_element_type=F32)
    u = jnp.dot(hb, wsu_ref[...], preferred_element_type=F32)
    shared = jnp.dot((g * _sigmoid(g) * u).astype(BF16), wsd_ref[...], preferred_element_type=F32)
    w = w_ref[...]
    pieces = []
    for s in range(h_ref.shape[-2]):
        acc = w[:, 0:1] * y_refs[0][:, s, :]
        for k in range(1, TOP_K):
            acc = acc + w[:, k:k + 1] * y_refs[k][:, s, :]
        pieces.append(acc)
    routed = jnp.concatenate(pieces, axis=-1)
    res = DN_ALPHA * x1_ref[...] + (1.0 + _rows(g2_ref, bb, r)) * (routed + shared)
    o_ref[...] = _ln(res) * lg_ref[...] + lb_ref[...]


def _combine(x1, h3, y3, wsel, g2, wsg_bf, wsu_bf, wsd_bf, ln_g, ln_b, nb, seq, row_off):
    t, d = x1.shape
    t_all, ns, _ = h3.shape
    ds_ = wsg_bf.shape[1]
    rows = COMB_ROWS
    if seq >= rows:
        bb, r, nl = 1, rows, seq // rows
        grid = (nb, nl)
        row_map = lambda b, l: (b * nl + l, 0)
    else:
        bb, r, nl = rows // seq, seq, 1
        grid = (nb // bb, 1)
        row_map = lambda b, l: (b, 0)
    seq_map = lambda b, l: (b, 0, 0)
    off = row_off // rows
    all_map = lambda b, l: (row_map(b, l)[0] + off, 0)
    slab_map = lambda b, l: (row_map(b, l)[0] + off, 0, 0)
    const2 = lambda b, l: (0, 0)

    def slot_map(k):
        return lambda b, l: (row_map(b, l)[0] + off + k * (t_all // rows), 0, 0)

    return pl.pallas_call(
        functools.partial(_combine_body, bb=bb, r=r),
        grid=grid,
        in_specs=[pl.BlockSpec((rows, d), row_map),
                  pl.BlockSpec((rows, ns, LANES), slab_map),
                  pl.BlockSpec((rows, TOP_K), all_map),
                  pl.BlockSpec((bb, 1, d), seq_map),
                  pl.BlockSpec((d, ds_), const2),
                  pl.BlockSpec((d, ds_), const2),
                  pl.BlockSpec((ds_, d), const2),
                  pl.BlockSpec((1, d), const2),
                  pl.BlockSpec((1, d), const2)]
                 + [pl.BlockSpec((rows, ns, LANES), slot_map(k)) for k in range(TOP_K)],
        out_specs=pl.BlockSpec((rows, d), row_map),
        out_shape=jax.ShapeDtypeStruct((t, d), F32),
        compiler_params=_cparams("arbitrary", "arbitrary"),
        name="combine",
    )(x1, h3, wsel, g2, wsg_bf, wsu_bf, wsd_bf, ln_g.reshape(1, d), ln_b.reshape(1, d), *([y3] * TOP_K))


def kernel(x_prompt, x_sample, state_pool, state_ssm_re, state_ssm_im, c_prompt, c_sample, w_ada, b_ada, w_in, w_pool, pool_scale, A_re, A_im, log_dt, B_re, B_im, C_re, C_im, D_skip, w_glu, b_glu, g_pool, g_ssm, w_out, ln1_g, ln1_b, w_router, router_bias, w_e_gate, w_e_up, w_e_down, w_sh_gate, w_sh_up, w_sh_down, ln2_g, ln2_b):
    bp, lp, d = x_prompt.shape
    bs, ls, _ = x_sample.shape
    depth = w_ada.shape[0]
    assert depth == DEPTH == 1
    tp, ts = bp * lp, bs * ls
    t_all = tp + ts
    l = 0
    c_pool = pool_scale.shape[-1]
    n_groups, n_state = A_re.shape[1], A_re.shape[2]
    nstate = n_groups * n_state
    n_experts = w_router.shape[-1]
    gpt = SCAN_W // n_state

    c_all = jnp.concatenate([c_prompt, c_sample], axis=0)
    pad = (-c_all.shape[0]) % SUBLANES
    c_all = jnp.pad(c_all, ((0, pad), (0, 0)))
    mod = _adaln(c_all, w_ada[l], b_ada[l]).reshape(c_all.shape[0], 6, 1, d)
    mod_p = [mod[:bp, i] for i in range(6)]
    mod_s = [mod[bp:bp + bs, i] for i in range(6)]

    w_in_bf = w_in[l].astype(BF16)
    w_pool_bf = w_pool[l].astype(BF16)
    w_glu_bf = w_glu[l].astype(BF16)
    w_out_bf = w_out[l].astype(BF16)
    w_router_t_bf = w_router[l].T.astype(BF16)
    wsg_bf, wsu_bf, wsd_bf = w_sh_gate[l].astype(BF16), w_sh_up[l].astype(BF16), w_sh_down[l].astype(BF16)
    pw_r, pw_i, bb_r, bb_i = _ssm_prep(A_re[l], A_im[l], log_dt[l], B_re[l], B_im[l])
    bbr_bd = _block_diag(jnp.swapaxes(bb_r, 1, 2), gpt).astype(BF16)
    bbi_bd = _block_diag(jnp.swapaxes(bb_i, 1, 2), gpt).astype(BF16)
    cr_bd = _block_diag(jnp.swapaxes(C_re[l], 1, 2), gpt).astype(BF16)
    nci_bd = _block_diag(jnp.swapaxes(-C_im[l], 1, 2), gpt).astype(BF16)
    d_skip = D_skip[l].reshape(-1)

    groups = [
        dict(x=x_prompt.reshape(tp, d), nb=bp, seq=lp, mod=mod_p, start=0, row_off=0,
             prefix=jnp.zeros((bp, POOL_HALO, c_pool), F32),
             h0r=jnp.zeros((bp, 1, nstate), F32), h0i=jnp.zeros((bp, 1, nstate), F32)),
        dict(x=x_sample.reshape(ts, d), nb=bs, seq=ls, mod=mod_s, start=PAST_LEN, row_off=tp,
             prefix=jnp.pad(state_pool[l], ((0, 0), (POOL_HALO - state_pool.shape[2], 0), (0, 0))),
             h0r=state_ssm_re[l].reshape(bs, 1, nstate), h0i=state_ssm_im[l].reshape(bs, 1, nstate)),
    ]

    h_all = None
    for gr in groups:
        sh1, sc1, g1, sh2, sc2, g2 = gr['mod']
        nb, seq = gr['nb'], gr['seq']
        z = _mix_in(gr['x'], sh1, sc1, w_in_bf, nb, seq)
        ya = _pool(z, gr['prefix'], w_pool_bf, pool_scale[l], g_pool[l], nb, seq, gr['start'])
        ys, hfr, hfi = _ssm(z, gr['h0r'], gr['h0i'], bbr_bd, bbi_bd, cr_bd, nci_bd, d_skip, pw_r, pw_i,
                            w_glu_bf, b_glu[l], g_ssm[l], nb, seq)
        x1, h_all, eidx_t, ew_t = _out_proj(gr['x'], ya, ys, g1, sh2, sc2, w_out_bf, ln1_g[l], ln1_b[l],
                                            w_router_t_bf, router_bias[l], nb, seq,
                                            h_all, t_all, gr['row_off'])
        gr.update(z=z, x1=x1, eidx_t=eidx_t, ew_t=ew_t, hfr=hfr, hfi=hfi)

    eidx_t = jnp.concatenate([gr['eidx_t'] for gr in groups], axis=1)
    wsel = jnp.concatenate([gr['ew_t'] for gr in groups], axis=1).T
    pairs, tok_blocks, dst_blocks = _dispatch(eidx_t, n_experts)
    y3 = _experts(h_all, w_e_gate[l], w_e_up[l], w_e_down[l], pairs, tok_blocks, dst_blocks)

    outs = []
    for gr in groups:
        outs.append(_combine(gr['x1'], h_all, y3, wsel, gr['mod'][5], wsg_bf, wsu_bf, wsd_bf,
                             ln2_g[l], ln2_b[l], gr['nb'], gr['seq'], gr['row_off']))
    y_prompt = outs[0].reshape(bp, lp, d)
    y_sample = outs[1].reshape(bs, ls, d)

    nbuf = state_pool.shape[2]
    zp = groups[0]['z'].reshape(bp, lp, d)[:, :, :c_pool]
    zs = groups[1]['z'].reshape(bs, ls, d)[:, :, :c_pool]
    pool_p = zp[:, lp - nbuf:, :][None]
    pool_s = jnp.concatenate([state_pool[l], zs], axis=1)[:, -nbuf:, :][None]
    st = lambda a, nb: a.reshape(nb, n_groups, n_state)[None]
    return (y_prompt, y_sample, pool_p, pool_s,
            st(groups[0]['hfr'], bp), st(groups[0]['hfi'], bp),
            st(groups[1]['hfr'], bs), st(groups[1]['hfi'], bs))
```

```python
import functools

import jax
import jax.numpy as jnp
from jax import lax
from jax.experimental import pallas as pl
from jax.experimental.pallas import tpu as pltpu

F32 = jnp.float32
BF16 = jnp.bfloat16
I32 = jnp.int32
U32 = jnp.uint32

DEPTH = 1
PAST_LEN = 16384
POOL_WINDOWS = (2, 4, 8, 16)
POOL_HALO = 16
SSM_P = 16
SSM_N = 64
N_EXPERT_GROUPS = 8
TOPK_GROUPS = 4
TOP_K = 8
ROUTED_SCALE = 2.5
LN_EPS = 1e-5
DN_ALPHA = (2.0 * DEPTH) ** 0.25

ROWS = 256
SUBLANES = 8
LANES = 128
SCAN_W = 512
MOE_BM = 256
COMB_ROWS = 128
VMEM_LIMIT = 56 * 1024 * 1024


def _cparams(*sem):
    return pltpu.CompilerParams(dimension_semantics=sem, vmem_limit_bytes=VMEM_LIMIT)


def _ln(x):
    xc = x - jnp.mean(x, axis=-1, keepdims=True)
    return xc * lax.rsqrt(jnp.mean(xc * xc, axis=-1, keepdims=True) + LN_EPS)


def _rows(m_ref, bb, r):
    m = m_ref[...]
    c = m.shape[-1]
    return jnp.broadcast_to(m, (bb, r, c)).reshape(bb * r, c)


def _sigmoid(x):
    return 1.0 / (1.0 + jnp.exp(-x))


def _bf16_bits(x):
    b = lax.bitcast_convert_type(x, U32)
    return b + (jnp.uint32(0x7FFF) + ((b >> 16) & jnp.uint32(1)))


def _pack_pairs(x):
    c = x.shape[-1] // 2
    return (_bf16_bits(x[:, :c]) >> 16) | (_bf16_bits(x[:, c:]) & jnp.uint32(0xFFFF0000))


def _unpack_pairs(w):
    return (lax.bitcast_convert_type(w << 16, F32),
            lax.bitcast_convert_type(w & jnp.uint32(0xFFFF0000), F32))


def _adaln_body(c_ref, w_ref, b_ref, o_ref):
    c = c_ref[...]
    s = (c * _sigmoid(c)).astype(BF16)
    o_ref[...] = jnp.dot(s, w_ref[...].astype(BF16), preferred_element_type=F32) + b_ref[...]


def _adaln(c_all, w_ada, b_ada):
    bc, d = c_all.shape
    n = w_ada.shape[1]
    tn = 1024
    return pl.pallas_call(
        _adaln_body,
        grid=(n // tn,),
        in_specs=[pl.BlockSpec((bc, d), lambda j: (0, 0)),
                  pl.BlockSpec((d, tn), lambda j: (0, j)),
                  pl.BlockSpec((1, tn), lambda j: (0, j))],
        out_specs=pl.BlockSpec((bc, tn), lambda j: (0, j)),
        out_shape=jax.ShapeDtypeStruct((bc, n), F32),
        compiler_params=_cparams("arbitrary"),
        name="adaln",
    )(c_all, w_ada, b_ada.reshape(1, n))


def _ssm_prep_body(ar_ref, ai_ref, dt_ref, ar16_ref, ai16_ref, dt16_ref, br_ref, bi_ref,
                   pr_ref, pi_ref, bbr_ref, bbi_ref):
    def zoh(a_r, a_i, dt):
        mag = jnp.exp(a_r * dt)
        ab_r, ab_i = mag * jnp.cos(a_i * dt), mag * jnp.sin(a_i * dt)
        den = a_r * a_r + a_i * a_i
        nr = ab_r - 1.0
        return ab_r, ab_i, (nr * a_r + ab_i * a_i) / den, (ab_i * a_r - nr * a_i) / den

    ab_r, ab_i, _, _ = zoh(ar_ref[...], ai_ref[...], jnp.exp(dt_ref[...]))
    p_r, p_i = ab_r, ab_i
    for k in range(SUBLANES):
        pr_ref[k] = p_r
        pi_ref[k] = p_i
        p_r, p_i = p_r * ab_r - p_i * ab_i, p_r * ab_i + p_i * ab_r
    _, _, f_r, f_i = zoh(ar16_ref[...], ai16_ref[...], jnp.exp(dt16_ref[...]))
    b_r, b_i = br_ref[...], bi_ref[...]
    bbr_ref[...] = f_r * b_r - f_i * b_i
    bbi_ref[...] = f_r * b_i + f_i * b_r


def _ssm_prep(a_re, a_im, log_dt, b_re, b_im):
    g, n = a_re.shape
    p = b_re.shape[-1]
    dt = jnp.broadcast_to(log_dt[:, None], (g, n))
    rep = lambda a: jnp.repeat(a, p, axis=-1)
    outs = pl.pallas_call(
        _ssm_prep_body,
        out_shape=(jax.ShapeDtypeStruct((SUBLANES, g, n), F32), jax.ShapeDtypeStruct((SUBLANES, g, n), F32),
                   jax.ShapeDtypeStruct((g, n * p), F32), jax.ShapeDtypeStruct((g, n * p), F32)),
        name="ssm_prep",
    )(a_re, a_im, dt, rep(a_re), rep(a_im), rep(dt), b_re.reshape(g, n * p), b_im.reshape(g, n * p))
    pw_r, pw_i, bb_r, bb_i = outs
    return (pw_r.reshape(SUBLANES, g * n), pw_i.reshape(SUBLANES, g * n),
            bb_r.reshape(g, n, p), bb_i.reshape(g, n, p))


def _block_diag(w, gpt):
    g, a, b = w.shape
    w4 = w.reshape(g // gpt, gpt, a, b)
    eye = jnp.eye(gpt, dtype=w.dtype)
    return jnp.einsum('jgab,gh->jgahb', w4, eye).reshape(g // gpt, gpt * a, gpt * b)


def _mix_in_body(x_ref, sh_ref, sc_ref, w_ref, z_ref, *, bb, r):
    u = _ln(x_ref[...]) * (1.0 + _rows(sc_ref, bb, r)) + _rows(sh_ref, bb, r)
    z_ref[...] = jnp.dot(u.astype(BF16), w_ref[...], preferred_element_type=F32)


def _seq_grid(nb, seq):
    if seq >= ROWS:
        bb, r, nl = 1, ROWS, seq // ROWS
        grid = (nb, nl)
        row_map = lambda b, l: (b * nl + l, 0)
    else:
        bb, r, nl = ROWS // seq, seq, 1
        grid = (nb // bb, 1)
        row_map = lambda b, l: (b, 0)
    seq_map = lambda b, l: (b, 0, 0)
    return bb, r, grid, row_map, seq_map


def _mix_in(x2, sh, sc, w_in_bf, nb, seq):
    t, d = x2.shape
    bb, r, grid, row_map, seq_map = _seq_grid(nb, seq)
    return pl.pallas_call(
        functools.partial(_mix_in_body, bb=bb, r=r),
        grid=grid,
        in_specs=[pl.BlockSpec((ROWS, d), row_map),
                  pl.BlockSpec((bb, 1, d), seq_map),
                  pl.BlockSpec((bb, 1, d), seq_map),
                  pl.BlockSpec((d, d), lambda b, l: (0, 0))],
        out_specs=pl.BlockSpec((ROWS, d), row_map),
        out_shape=jax.ShapeDtypeStruct((t, d), F32),
        compiler_params=_cparams("arbitrary", "arbitrary"),
        name="mix_in",
    )(x2, sh, sc, w_in_bf)


def _pool_body(z_ref, pre_ref, wp_ref, ps_ref, gp_ref, o_ref, carry_ref, *, bb, r, start_pos):
    li = pl.program_id(1)
    c = z_ref.shape[-1]
    gw = c // len(POOL_WINDOWS)
    rp = POOL_HALO + r

    @pl.when(li == 0)
    def _():
        carry_ref[...] = pre_ref[...]

    za = z_ref[...].reshape(bb, r, c)
    xp3 = jnp.concatenate([carry_ref[...], za], axis=1)
    carry_ref[...] = xp3[:, r:, :]
    xp = xp3.reshape(bb * rp, c)
    pos1 = lax.broadcasted_iota(I32, (bb, r, gw), 1) + (start_pos + 1) + li * r
    outs = []
    ssq = jnp.zeros((bb * r, 1), F32)
    for gi, w in enumerate(POOL_WINDOWS):
        cols = slice(gi * gw, (gi + 1) * gw)
        s = xp[:, cols]
        sh = 1
        while sh < w:
            s = s + pltpu.roll(s, sh, 0)
            sh *= 2
        win = s.reshape(bb, rp, gw)[:, POOL_HALO:, :]
        cnt = jnp.minimum(pos1, w).astype(F32)
        d = (win / cnt - za[:, :, cols]).reshape(bb * r, gw)
        y = jnp.dot(d.astype(BF16), wp_ref[gi], preferred_element_type=F32) * ps_ref[:, cols]
        ssq = ssq + jnp.sum(y * y, axis=-1, keepdims=True)
        outs.append(y)
    scale = lax.rsqrt(ssq * (1.0 / c) + LN_EPS)
    for gi, y in enumerate(outs):
        cols = slice(gi * gw, (gi + 1) * gw)
        o_ref[:, cols] = (y * scale * gp_ref[:, cols]).astype(o_ref.dtype)


def _pool(z, prefix16, w_pool_bf, pool_scale, g_pool, nb, seq, start_pos):
    t = z.shape[0]
    c = pool_scale.shape[-1]
    bb, r, grid, row_map, seq_map = _seq_grid(nb, seq)
    const2 = lambda b, l: (0, 0)
    return pl.pallas_call(
        functools.partial(_pool_body, bb=bb, r=r, start_pos=start_pos),
        grid=grid,
        in_specs=[pl.BlockSpec((ROWS, c), row_map),
                  pl.BlockSpec((bb, POOL_HALO, c), seq_map),
                  pl.BlockSpec(w_pool_bf.shape, lambda b, l: (0, 0, 0)),
                  pl.BlockSpec((1, c), const2),
                  pl.BlockSpec((1, c), const2)],
        out_specs=pl.BlockSpec((ROWS, c), row_map),
        out_shape=jax.ShapeDtypeStruct((t, c), BF16),
        scratch_shapes=[pltpu.VMEM((bb, POOL_HALO, c), F32)],
        compiler_params=_cparams("arbitrary", "arbitrary"),
        name="pool",
    )(z, prefix16, w_pool_bf, pool_scale.reshape(1, c), g_pool.reshape(1, c))


def _cmul_add(x_r, x_i, m_r, m_i, y_r, y_i):
    return x_r + m_r * y_r - m_i * y_i, x_i + m_r * y_i + m_i * y_r


def _ssm_body(z_ref, h0r_ref, h0i_ref, bbr_ref, bbi_ref, cr_ref, nci_ref, dsk_ref, pr_ref, pi_ref,
              wg_ref, bg_ref, gs_ref, o_ref, hfr_ref, hfi_ref, hr_s, hi_s, y_s, car_s, cai_s,
              *, bb, r, chained):
    li = pl.program_id(1)
    c = z_ref.shape[-1]
    nstate = hr_s.shape[-1]
    ntile = nstate // SCAN_W
    cw = c // ntile
    u = z_ref[...]
    ub = u.astype(BF16)
    for j in range(ntile):
        sl = slice(j * SCAN_W, (j + 1) * SCAN_W)
        uj = ub[:, j * cw:(j + 1) * cw]
        hr_s[:, sl] = jnp.dot(uj, bbr_ref[j], preferred_element_type=F32)
        hi_s[:, sl] = jnp.dot(uj, bbi_ref[j], preferred_element_type=F32)

    if chained:
        @pl.when(li == 0)
        def _():
            car_s[...] = h0r_ref[0]
            cai_s[...] = h0i_ref[0]

    row = lax.broadcasted_iota(I32, (SUBLANES, SCAN_W), 0)
    bc8 = lambda v: jnp.broadcast_to(v, (SUBLANES, SCAN_W))
    for j in range(ntile):
        sl = slice(j * SCAN_W, (j + 1) * SCAN_W)
        p_r, p_i = pr_ref[:, sl], pi_ref[:, sl]
        steps = []
        k = 1
        while k < SUBLANES:
            keep = row >= k
            steps.append((k, jnp.where(keep, bc8(p_r[k - 1:k]), 0.0), jnp.where(keep, bc8(p_i[k - 1:k]), 0.0)))
            k *= 2

        def tile(i, carry, sl=sl, p_r=p_r, p_i=p_i, steps=steps):
            off = pl.multiple_of(i * SUBLANES, SUBLANES)
            x_r, x_i = hr_s[pl.ds(off, SUBLANES), sl], hi_s[pl.ds(off, SUBLANES), sl]
            for k, m_r, m_i in steps:
                x_r, x_i = _cmul_add(x_r, x_i, m_r, m_i, pltpu.roll(x_r, k, 0), pltpu.roll(x_i, k, 0))
            if chained:
                c_r, c_i = carry
            else:
                c_r, c_i = bc8(h0r_ref[i, :, sl]), bc8(h0i_ref[i, :, sl])
            x_r, x_i = _cmul_add(x_r, x_i, p_r, p_i, c_r, c_i)
            hr_s[pl.ds(off, SUBLANES), sl] = x_r
            hi_s[pl.ds(off, SUBLANES), sl] = x_i
            l_r, l_i = x_r[SUBLANES - 1:SUBLANES], x_i[SUBLANES - 1:SUBLANES]
            if not chained:
                hfr_ref[i, :, sl] = l_r
                hfi_ref[i, :, sl] = l_i
            return bc8(l_r), bc8(l_i)

        if chained:
            init = (bc8(car_s[:, sl]), bc8(cai_s[:, sl]))
        else:
            init = (jnp.zeros((SUBLANES, SCAN_W), F32),) * 2
        e_r, e_i = lax.fori_loop(0, (bb * r) // SUBLANES, tile, init)
        if chained:
            car_s[:, sl] = e_r[0:1]
            cai_s[:, sl] = e_i[0:1]
            hfr_ref[0, :, sl] = e_r[0:1]
            hfi_ref[0, :, sl] = e_i[0:1]

    for j in range(ntile):
        sl = slice(j * SCAN_W, (j + 1) * SCAN_W)
        cs = slice(j * cw, (j + 1) * cw)
        y_s[:, cs] = (jnp.dot(hr_s[:, sl].astype(BF16), cr_ref[j], preferred_element_type=F32)
                      + jnp.dot(hi_s[:, sl].astype(BF16), nci_ref[j], preferred_element_type=F32)
                      + dsk_ref[:, cs] * u[:, cs])
    y = y_s[...]
    g = 0.5 * y * (1.0 + jnp.tanh(0.7978845608028654 * (y + 0.044715 * (y * y * y))))
    gate = jnp.dot(g.astype(BF16), wg_ref[...], preferred_element_type=F32) + bg_ref[...]
    out = g * _sigmoid(gate)
    scale = lax.rsqrt(jnp.mean(out * out, axis=-1, keepdims=True) + LN_EPS)
    o_ref[...] = (out * scale * gs_ref[...]).astype(o_ref.dtype)


def _ssm(z, h0r, h0i, bbr_bd, bbi_bd, cr_bd, nci_bd, d_skip, pw_r, pw_i, w_glu_bf, b_glu, g_ssm, nb, seq):
    t = z.shape[0]
    c = d_skip.shape[-1]
    nstate = pw_r.shape[-1]
    bb, r, grid, row_map, seq_map = _seq_grid(nb, seq)
    chained = bb == 1
    const2 = lambda b, l: (0, 0)
    const3 = lambda b, l: (0, 0, 0)
    full = lambda a: pl.BlockSpec(a.shape, const3 if a.ndim == 3 else const2)
    right_half = (lambda b, l: (row_map(b, l)[0], 1))
    return pl.pallas_call(
        functools.partial(_ssm_body, bb=bb, r=r, chained=chained),
        grid=grid,
        in_specs=[pl.BlockSpec((ROWS, c), right_half),
                  pl.BlockSpec((bb, 1, nstate), seq_map),
                  pl.BlockSpec((bb, 1, nstate), seq_map),
                  full(bbr_bd), full(bbi_bd), full(cr_bd), full(nci_bd),
                  pl.BlockSpec((1, c), const2),
                  full(pw_r), full(pw_i),
                  full(w_glu_bf),
                  pl.BlockSpec((1, c), const2),
                  pl.BlockSpec((1, c), const2)],
        out_specs=[pl.BlockSpec((ROWS, c), row_map),
                   pl.BlockSpec((bb, 1, nstate), seq_map),
                   pl.BlockSpec((bb, 1, nstate), seq_map)],
        out_shape=[jax.ShapeDtypeStruct((t, c), BF16),
                   jax.ShapeDtypeStruct((nb, 1, nstate), F32),
                   jax.ShapeDtypeStruct((nb, 1, nstate), F32)],
        scratch_shapes=[pltpu.VMEM((ROWS, nstate), F32), pltpu.VMEM((ROWS, nstate), F32),
                        pltpu.VMEM((ROWS, c), F32),
                        pltpu.VMEM((1, nstate), F32), pltpu.VMEM((1, nstate), F32)],
        compiler_params=_cparams("arbitrary", "arbitrary"),
        name="ssm",
    )(z, h0r, h0i, bbr_bd, bbi_bd, cr_bd, nci_bd, d_skip.reshape(1, c), pw_r, pw_i,
      w_glu_bf, b_glu.reshape(1, c), g_ssm.reshape(1, c))


def _route(s_t, bias_t):
    e, tn = s_t.shape
    per = e // N_EXPERT_GROUPS
    neg = -jnp.inf
    sb = s_t + bias_t
    rowl = lax.broadcasted_iota(I32, (per, tn), 0)
    gscore = []
    for g in range(N_EXPERT_GROUPS):
        blk = sb[g * per:(g + 1) * per]
        m1 = jnp.max(blk, axis=0, keepdims=True)
        i1 = jnp.min(jnp.where(blk == m1, rowl, per), axis=0, keepdims=True)
        m2 = jnp.max(jnp.where(rowl == i1, neg, blk), axis=0, keepdims=True)
        gscore.append(m1 + m2)
    cur = jnp.concatenate(gscore, axis=0)
    rowg = lax.broadcasted_iota(I32, cur.shape, 0)
    gsel = jnp.zeros(cur.shape, F32)
    for _ in range(TOPK_GROUPS):
        m = jnp.max(cur, axis=0, keepdims=True)
        hit = rowg == jnp.min(jnp.where(cur == m, rowg, N_EXPERT_GROUPS), axis=0, keepdims=True)
        gsel = jnp.where(hit, 1.0, gsel)
        cur = jnp.where(hit, neg, cur)
    cur = jnp.concatenate(
        [jnp.where(jnp.broadcast_to(gsel[g:g + 1], (per, tn)) > 0.0, sb[g * per:(g + 1) * per], neg)
         for g in range(N_EXPERT_GROUPS)], axis=0)
    rowe = lax.broadcasted_iota(I32, (e, tn), 0)
    idxs, vals = [], []
    for _ in range(TOP_K):
        m = jnp.max(cur, axis=0, keepdims=True)
        idx = jnp.min(jnp.where(cur == m, rowe, e), axis=0, keepdims=True)
        hit = rowe == idx
        idxs.append(idx)
        vals.append(jnp.sum(jnp.where(hit, s_t, 0.0), axis=0, keepdims=True))
        cur = jnp.where(hit, neg, cur)
    w = jnp.concatenate(vals, axis=0)
    w = w / jnp.sum(w, axis=0, keepdims=True) * ROUTED_SCALE
    return jnp.concatenate(idxs, axis=0), w


def _out_proj_body(x_ref, ya_ref, ys_ref, g1_ref, sh2_ref, sc2_ref, wo_ref, lg_ref, lb_ref, wrt_ref, rb_ref,
                   x1_ref, h_ref, ei_ref, ew_ref, *, bb, r):
    ca = ya_ref.shape[-1]
    m = (jnp.dot(ya_ref[...], wo_ref[:ca, :], preferred_element_type=F32)
         + jnp.dot(ys_ref[...], wo_ref[ca:, :], preferred_element_type=F32))
    res = DN_ALPHA * x_ref[...] + (1.0 + _rows(g1_ref, bb, r)) * m
    x1 = _ln(res) * lg_ref[...] + lb_ref[...]
    x1_ref[...] = x1
    h = _ln(x1) * (1.0 + _rows(sc2_ref, bb, r)) + _rows(sh2_ref, bb, r)
    h_ref[...] = _pack_pairs(h)
    logit_t = lax.dot_general(wrt_ref[...], h.astype(BF16), (((1,), (1,)), ((), ())),
                              preferred_element_type=F32)
    idx, w = _route(_sigmoid(logit_t), rb_ref[...])
    ei_ref[...] = idx
    ew_ref[...] = w


def _out_proj(x2, ya, ys, g1, sh2, sc2, w_out_bf, ln_g, ln_b, w_router_t_bf, router_bias, nb, seq,
              h_all, t_all, row_off):
    t, d = x2.shape
    ca = ya.shape[-1]
    e = w_router_t_bf.shape[0]
    bb, r, grid, row_map, seq_map = _seq_grid(nb, seq)
    const2 = lambda b, l: (0, 0)
    tok_map = lambda b, l: (0, row_map(b, l)[0])
    blk_off = row_off // ROWS
    h_map = lambda b, l: (row_map(b, l)[0] + blk_off, 0)
    in_specs = [pl.BlockSpec((ROWS, d), row_map),
                pl.BlockSpec((ROWS, ca), row_map),
                pl.BlockSpec((ROWS, ca), row_map),
                pl.BlockSpec((bb, 1, d), seq_map),
                pl.BlockSpec((bb, 1, d), seq_map),
                pl.BlockSpec((bb, 1, d), seq_map),
                pl.BlockSpec((d, d), const2),
                pl.BlockSpec((1, d), const2),
                pl.BlockSpec((1, d), const2),
                pl.BlockSpec((e, d), const2),
                pl.BlockSpec((e, 1), const2)]
    args = [x2, ya, ys, g1, sh2, sc2, w_out_bf, ln_g.reshape(1, d), ln_b.reshape(1, d),
            w_router_t_bf, router_bias.reshape(e, 1)]
    body = functools.partial(_out_proj_body, bb=bb, r=r)
    aliases = {}
    if h_all is not None:
        in_specs.append(pl.BlockSpec(memory_space=pl.ANY))
        args.append(h_all)
        aliases = {len(args) - 1: 1}
        inner = body
        body = lambda *refs: inner(*refs[:11], *refs[12:])
    return pl.pallas_call(
        body,
        grid=grid,
        in_specs=in_specs,
        out_specs=[pl.BlockSpec((ROWS, d), row_map),
                   pl.BlockSpec((ROWS, d // 2), h_map),
                   pl.BlockSpec((TOP_K, ROWS), tok_map),
                   pl.BlockSpec((TOP_K, ROWS), tok_map)],
        out_shape=[jax.ShapeDtypeStruct((t, d), F32),
                   jax.ShapeDtypeStruct((t_all, d // 2), U32),
                   jax.ShapeDtypeStruct((TOP_K, t), I32),
                   jax.ShapeDtypeStruct((TOP_K, t), F32)],
        input_output_aliases=aliases,
        compiler_params=_cparams("arbitrary", "arbitrary"),
        name="out_proj",
    )(*args)


def _experts_body(pb_ref, pe_ref, plo_ref, pfl_ref, np_ref, tokc_ref, tokn_ref, dst_ref, h_hbm,
                  wg_ref, wu_ref, wd_ref, y_hbm, xbuf, ybuf, y_s, gsem, ssem, *, nblocks):
    p = pl.program_id(0)
    n_pairs = np_ref[0]
    valid = p < n_pairs
    b = pb_ref[p]
    slot = b % 2
    first = (pfl_ref[p] & 1) == 1
    last = (pfl_ref[p] & 2) == 2
    bm = xbuf.shape[1]

    def gather_start(tok_ref, s):
        for i in range(bm):
            pltpu.make_async_copy(h_hbm.at[pl.ds(tok_ref[0, 0, i], 1), :],
                                  xbuf.at[s, pl.ds(i, 1), :], gsem.at[s]).start()

    def gather_wait(s):
        pltpu.make_async_copy(h_hbm.at[pl.ds(0, bm), :], xbuf.at[s], gsem.at[s]).wait()

    def scatter_wait(s):
        pltpu.make_async_copy(ybuf.at[s], y_hbm.at[pl.ds(0, bm), :], ssem.at[0]).wait()

    @pl.when(p == 0)
    def _():
        gather_start(tokc_ref, 0)

    @pl.when(jnp.logical_and(valid, first))
    def _():
        gather_wait(slot)

        @pl.when(b + 1 < nblocks)
        def _():
            gather_start(tokn_ref, 1 - slot)

    @pl.when(valid)
    def _():
        c = xbuf.shape[-1]
        x_lo, x_hi = _unpack_pairs(xbuf[slot])
        x_lo, x_hi = x_lo.astype(BF16), x_hi.astype(BF16)

        def proj(w_ref):
            return (jnp.dot(x_lo, w_ref[0, :c, :].astype(BF16), preferred_element_type=F32)
                    + jnp.dot(x_hi, w_ref[0, c:, :].astype(BF16), preferred_element_type=F32))

        g = proj(wg_ref)
        u = proj(wu_ref)
        a = (g * _sigmoid(g) * u).astype(BF16)
        y_s[...] = _pack_pairs(jnp.dot(a, wd_ref[0].astype(BF16), preferred_element_type=F32))

        @pl.when(first)
        def _():
            ybuf[slot] = y_s[...]

        @pl.when(jnp.logical_not(first))
        def _():
            mine = lax.broadcasted_iota(I32, y_s.shape, 0) >= plo_ref[p]
            ybuf[slot] = jnp.where(mine, y_s[...], ybuf[slot])

        @pl.when(last)
        def _():
            @pl.when(b > 0)
            def _():
                scatter_wait(1 - slot)

            for i in range(bm):
                pltpu.make_async_copy(ybuf.at[slot, pl.ds(i, 1), :],
                                      y_hbm.at[pl.ds(dst_ref[0, 0, i], 1), :], ssem.at[0]).start()

            @pl.when(p == n_pairs - 1)
            def _():
                scatter_wait(slot)


def _experts(h_pack, w_gate, w_up, w_down, pairs, tok_blocks, dst_blocks):
    t, c = h_pack.shape
    d = 2 * c
    e, _, de = w_gate.shape
    nblk = tok_blocks.shape[0]
    bm = MOE_BM
    pb, pe, plo, pfl, n_pairs = pairs
    cur = lambda p, pb, pe, plo, pfl, n: (pb[p], 0, 0)
    nxt = lambda p, pb, pe, plo, pfl, n: (jnp.minimum(pb[p] + 1, nblk - 1), 0, 0)
    wmap = lambda p, pb, pe, plo, pfl, n: (pe[p], 0, 0)
    smem_blk = lambda m: pl.BlockSpec((1, 1, bm), m, memory_space=pltpu.SMEM)
    grid_spec = pltpu.PrefetchScalarGridSpec(
        num_scalar_prefetch=5,
        grid=(pb.shape[0],),
        in_specs=[smem_blk(cur), smem_blk(nxt), smem_blk(cur),
                  pl.BlockSpec(memory_space=pl.ANY),
                  pl.BlockSpec((1, d, de), wmap),
                  pl.BlockSpec((1, d, de), wmap),
                  pl.BlockSpec((1, de, d), wmap)],
        out_specs=pl.BlockSpec(memory_space=pl.ANY),
        scratch_shapes=[pltpu.VMEM((2, bm, c), U32), pltpu.VMEM((2, bm, c), U32),
                        pltpu.VMEM((bm, c), U32),
                        pltpu.SemaphoreType.DMA((2,)), pltpu.SemaphoreType.DMA((1,))],
    )
    return pl.pallas_call(
        functools.partial(_experts_body, nblocks=nblk),
        grid_spec=grid_spec,
        out_shape=jax.ShapeDtypeStruct((nblk * bm, c), U32),
        compiler_params=_cparams("arbitrary"),
        name="experts",
    )(pb, pe, plo, pfl, n_pairs, tok_blocks, tok_blocks, dst_blocks, h_pack, w_gate, w_up, w_down)


def _dispatch(eidx_t, n_experts):
    k, t = eidx_t.shape
    a = t * k
    bm = MOE_BM
    nb = a // bm
    flat_e = eidx_t.reshape(a)
    se, order = lax.sort((flat_e, lax.iota(I32, a)), num_keys=1)
    tok = order % t
    ex = lax.iota(I32, n_experts)
    starts = jnp.sum((se[None, :] < ex[:, None]).astype(I32), axis=1)
    seb = se.reshape(nb, bm)
    e_lo, e_hi = seb[:, 0], seb[:, bm - 1]
    npair = e_hi - e_lo + 1
    cum = jnp.cumsum(npair)
    off = cum - npair
    p = lax.iota(I32, nb + n_experts)
    pb = jnp.minimum(jnp.sum((cum[None, :] <= p[:, None]).astype(I32), axis=1), nb - 1)
    pe = jnp.minimum(e_lo[pb] + p - off[pb], e_hi[pb])
    plo = jnp.clip(starts[pe] - pb * bm, 0, bm)
    pfl = (p == off[pb]).astype(I32) + 2 * (p == cum[pb] - 1).astype(I32)
    pairs = (pb.astype(I32), pe.astype(I32), plo.astype(I32), pfl, cum[-1:].astype(I32))
    return pairs, tok.reshape(nb, 1, bm), order.reshape(nb, 1, bm)


def _combine_body(x1_ref, h_ref, w_ref, g2_ref, wsg_ref, wsu_ref, wsd_ref, lg_ref, lb_ref, *rest, bb, r):
    y_refs, o_ref = rest[:TOP_K], rest[TOP_K]
    c = h_ref.shape[-1]
    h_lo, h_hi = _unpack_pairs(h_ref[...])
    h_lo, h_hi = h_lo.astype(BF16), h_hi.astype(BF16)

    def proj(w_ref):
        return (jnp.dot(h_lo, w_ref[:c, :], preferred_element_type=F32)
                + jnp.dot(h_hi, w_ref[c:, :], preferred_element_type=F32))

    g = proj(wsg_ref)
    u = proj(wsu_ref)
    shared = jnp.dot((g * _sigmoid(g) * u).astype(BF16), wsd_ref[...], preferred_element_type=F32)
    w = w_ref[...]
    r_lo = r_hi = None
    for k in range(TOP_K):
        y_lo, y_hi = _unpack_pairs(y_refs[k][...])
        wk = w[:, k:k + 1]
        r_lo = wk * y_lo if r_lo is None else r_lo + wk * y_lo
        r_hi = wk * y_hi if r_hi is None else r_hi + wk * y_hi
    routed = jnp.concatenate([r_lo, r_hi], axis=-1)
    res = DN_ALPHA * x1_ref[...] + (1.0 + _rows(g2_ref, bb, r)) * (routed + shared)
    o_ref[...] = _ln(res) * lg_ref[...] + lb_ref[...]


def _combine(x1, h_pack, y_pack, wsel, g2, wsg_bf, wsu_bf, wsd_bf, ln_g, ln_b, nb, seq, row_off):
    t, d = x1.shape
    t_all, c = h_pack.shape
    ds_ = wsg_bf.shape[1]
    rows = COMB_ROWS
    if seq >= rows:
        bb, r, nl = 1, rows, seq // rows
        grid = (nb, nl)
        row_map = lambda b, l: (b * nl + l, 0)
    else:
        bb, r, nl = rows // seq, seq, 1
        grid = (nb // bb, 1)
        row_map = lambda b, l: (b, 0)
    seq_map = lambda b, l: (b, 0, 0)
    off = row_off // rows
    all_map = lambda b, l: (row_map(b, l)[0] + off, 0)
    const2 = lambda b, l: (0, 0)

    def slot_map(k):
        return lambda b, l: (row_map(b, l)[0] + off + k * (t_all // rows), 0)

    return pl.pallas_call(
        functools.partial(_combine_body, bb=bb, r=r),
        grid=grid,
        in_specs=[pl.BlockSpec((rows, d), row_map),
                  pl.BlockSpec((rows, c), all_map),
                  pl.BlockSpec((rows, TOP_K), all_map),
                  pl.BlockSpec((bb, 1, d), seq_map),
                  pl.BlockSpec((d, ds_), const2),
                  pl.BlockSpec((d, ds_), const2),
                  pl.BlockSpec((ds_, d), const2),
                  pl.BlockSpec((1, d), const2),
                  pl.BlockSpec((1, d), const2)]
                 + [pl.BlockSpec((rows, c), slot_map(k)) for k in range(TOP_K)],
        out_specs=pl.BlockSpec((rows, d), row_map),
        out_shape=jax.ShapeDtypeStruct((t, d), F32),
        compiler_params=_cparams("arbitrary", "arbitrary"),
        name="combine",
    )(x1, h_pack, wsel, g2, wsg_bf, wsu_bf, wsd_bf, ln_g.reshape(1, d), ln_b.reshape(1, d),
      *([y_pack] * TOP_K))


def kernel(x_prompt, x_sample, state_pool, state_ssm_re, state_ssm_im, c_prompt, c_sample, w_ada, b_ada, w_in, w_pool, pool_scale, A_re, A_im, log_dt, B_re, B_im, C_re, C_im, D_skip, w_glu, b_glu, g_pool, g_ssm, w_out, ln1_g, ln1_b, w_router, router_bias, w_e_gate, w_e_up, w_e_down, w_sh_gate, w_sh_up, w_sh_down, ln2_g, ln2_b):
    bp, lp, d = x_prompt.shape
    bs, ls, _ = x_sample.shape
    depth = w_ada.shape[0]
    assert depth == DEPTH == 1
    tp, ts = bp * lp, bs * ls
    t_all = tp + ts
    l = 0
    c_pool = pool_scale.shape[-1]
    n_groups, n_state = A_re.shape[1], A_re.shape[2]
    nstate = n_groups * n_state
    n_experts = w_router.shape[-1]
    gpt = SCAN_W // n_state

    c_all = jnp.concatenate([c_prompt, c_sample], axis=0)
    pad = (-c_all.shape[0]) % SUBLANES
    c_all = jnp.pad(c_all, ((0, pad), (0, 0)))
    mod = _adaln(c_all, w_ada[l], b_ada[l]).reshape(c_all.shape[0], 6, 1, d)
    mod_p = [mod[:bp, i] for i in range(6)]
    mod_s = [mod[bp:bp + bs, i] for i in range(6)]

    w_in_bf = w_in[l].astype(BF16)
    w_pool_bf = w_pool[l].astype(BF16)
    w_glu_bf = w_glu[l].astype(BF16)
    w_out_bf = w_out[l].astype(BF16)
    w_router_t_bf = w_router[l].T.astype(BF16)
    wsg_bf, wsu_bf, wsd_bf = w_sh_gate[l].astype(BF16), w_sh_up[l].astype(BF16), w_sh_down[l].astype(BF16)
    pw_r, pw_i, bb_r, bb_i = _ssm_prep(A_re[l], A_im[l], log_dt[l], B_re[l], B_im[l])
    bbr_bd = _block_diag(jnp.swapaxes(bb_r, 1, 2), gpt).astype(BF16)
    bbi_bd = _block_diag(jnp.swapaxes(bb_i, 1, 2), gpt).astype(BF16)
    cr_bd = _block_diag(jnp.swapaxes(C_re[l], 1, 2), gpt).astype(BF16)
    nci_bd = _block_diag(jnp.swapaxes(-C_im[l], 1, 2), gpt).astype(BF16)
    d_skip = D_skip[l].reshape(-1)

    groups = [
        dict(x=x_prompt.reshape(tp, d), nb=bp, seq=lp, mod=mod_p, start=0, row_off=0,
             prefix=jnp.zeros((bp, POOL_HALO, c_pool), F32),
             h0r=jnp.zeros((bp, 1, nstate), F32), h0i=jnp.zeros((bp, 1, nstate), F32)),
        dict(x=x_sample.reshape(ts, d), nb=bs, seq=ls, mod=mod_s, start=PAST_LEN, row_off=tp,
             prefix=jnp.pad(state_pool[l], ((0, 0), (POOL_HALO - state_pool.shape[2], 0), (0, 0))),
             h0r=state_ssm_re[l].reshape(bs, 1, nstate), h0i=state_ssm_im[l].reshape(bs, 1, nstate)),
    ]

    h_all = None
    for gr in groups:
        sh1, sc1, g1, sh2, sc2, g2 = gr['mod']
        nb, seq = gr['nb'], gr['seq']
        z = _mix_in(gr['x'], sh1, sc1, w_in_bf, nb, seq)
        ya = _pool(z, gr['prefix'], w_pool_bf, pool_scale[l], g_pool[l], nb, seq, gr['start'])
        ys, hfr, hfi = _ssm(z, gr['h0r'], gr['h0i'], bbr_bd, bbi_bd, cr_bd, nci_bd, d_skip, pw_r, pw_i,
                            w_glu_bf, b_glu[l], g_ssm[l], nb, seq)
        x1, h_all, eidx_t, ew_t = _out_proj(gr['x'], ya, ys, g1, sh2, sc2, w_out_bf, ln1_g[l], ln1_b[l],
                                            w_router_t_bf, router_bias[l], nb, seq,
                                            h_all, t_all, gr['row_off'])
        gr.update(z=z, x1=x1, eidx_t=eidx_t, ew_t=ew_t, hfr=hfr, hfi=hfi)

    eidx_t = jnp.concatenate([gr['eidx_t'] for gr in groups], axis=1)
    wsel = jnp.concatenate([gr['ew_t'] for gr in groups], axis=1).T
    pairs, tok_blocks, dst_blocks = _dispatch(eidx_t, n_experts)
    y3 = _experts(h_all, w_e_gate[l], w_e_up[l], w_e_down[l], pairs, tok_blocks, dst_blocks)

    outs = []
    for gr in groups:
        outs.append(_combine(gr['x1'], h_all, y3, wsel, gr['mod'][5], wsg_bf, wsu_bf, wsd_bf,
                             ln2_g[l], ln2_b[l], gr['nb'], gr['seq'], gr['row_off']))
    y_prompt = outs[0].reshape(bp, lp, d)
    y_sample = outs[1].reshape(bs, ls, d)

    nbuf = state_pool.shape[2]
    zp = groups[0]['z'].reshape(bp, lp, d)[:, :, :c_pool]
    zs = groups[1]['z'].reshape(bs, ls, d)[:, :, :c_pool]
    pool_p = zp[:, lp - nbuf:, :][None]
    pool_s = jnp.concatenate([state_pool[l], zs], axis=1)[:, -nbuf:, :][None]
    st = lambda a, nb: a.reshape(nb, n_groups, n_state)[None]
    return (y_prompt, y_sample, pool_p, pool_s,
            st(groups[0]['hfr'], bp), st(groups[0]['hfi'], bp),
            st(groups[1]['hfr'], bs), st(groups[1]['hfi'], bs))
```

```python
import functools

import jax
import jax.numpy as jnp
from jax import lax
from jax.experimental import pallas as pl
from jax.experimental.pallas import tpu as pltpu

F32 = jnp.float32
BF16 = jnp.bfloat16
I32 = jnp.int32
U32 = jnp.uint32

DEPTH = 1
PAST_LEN = 16384
POOL_WINDOWS = (2, 4, 8, 16)
POOL_HALO = 16
SSM_P = 16
SSM_N = 64
N_EXPERT_GROUPS = 8
TOPK_GROUPS = 4
TOP_K = 8
ROUTED_SCALE = 2.5
LN_EPS = 1e-5
DN_ALPHA = (2.0 * DEPTH) ** 0.25

ROWS = 256
SUBLANES = 8
LANES = 128
SCAN_W = 512
MOE_BM = 256
COMB_ROWS = 128
VMEM_LIMIT = 56 * 1024 * 1024


def _cparams(*sem):
    return pltpu.CompilerParams(dimension_semantics=sem, vmem_limit_bytes=VMEM_LIMIT)


def _ln(x):
    xc = x - jnp.mean(x, axis=-1, keepdims=True)
    return xc * lax.rsqrt(jnp.mean(xc * xc, axis=-1, keepdims=True) + LN_EPS)


def _rows(m_ref, bb, r):
    m = m_ref[...]
    c = m.shape[-1]
    return jnp.broadcast_to(m, (bb, r, c)).reshape(bb * r, c)


def _sigmoid(x):
    return 1.0 / (1.0 + jnp.exp(-x))


def _bf16_bits(x):
    return lax.bitcast_convert_type(x.astype(BF16).astype(F32), U32)


def _pack_pairs(x):
    c = x.shape[-1] // 2
    return (_bf16_bits(x[:, :c]) >> 16) | _bf16_bits(x[:, c:])


def _unpack_pairs(w):
    return (lax.bitcast_convert_type(w << 16, F32),
            lax.bitcast_convert_type(w & jnp.uint32(0xFFFF0000), F32))


def _adaln_body(c_ref, w_ref, b_ref, o_ref):
    c = c_ref[...]
    s = (c * _sigmoid(c)).astype(BF16)
    o_ref[...] = jnp.dot(s, w_ref[...].astype(BF16), preferred_element_type=F32) + b_ref[...]


def _adaln(c_all, w_ada, b_ada):
    bc, d = c_all.shape
    n = w_ada.shape[1]
    tn = 1024
    return pl.pallas_call(
        _adaln_body,
        grid=(n // tn,),
        in_specs=[pl.BlockSpec((bc, d), lambda j: (0, 0)),
                  pl.BlockSpec((d, tn), lambda j: (0, j)),
                  pl.BlockSpec((1, tn), lambda j: (0, j))],
        out_specs=pl.BlockSpec((bc, tn), lambda j: (0, j)),
        out_shape=jax.ShapeDtypeStruct((bc, n), F32),
        compiler_params=_cparams("arbitrary"),
        name="adaln",
    )(c_all, w_ada, b_ada.reshape(1, n))


def _ssm_prep_body(ar_ref, ai_ref, dt_ref, ar16_ref, ai16_ref, dt16_ref, br_ref, bi_ref,
                   pr_ref, pi_ref, bbr_ref, bbi_ref):
    def zoh(a_r, a_i, dt):
        mag = jnp.exp(a_r * dt)
        ab_r, ab_i = mag * jnp.cos(a_i * dt), mag * jnp.sin(a_i * dt)
        den = a_r * a_r + a_i * a_i
        nr = ab_r - 1.0
        return ab_r, ab_i, (nr * a_r + ab_i * a_i) / den, (ab_i * a_r - nr * a_i) / den

    ab_r, ab_i, _, _ = zoh(ar_ref[...], ai_ref[...], jnp.exp(dt_ref[...]))
    p_r, p_i = ab_r, ab_i
    for k in range(pr_ref.shape[0]):
        pr_ref[k] = p_r
        pi_ref[k] = p_i
        p_r, p_i = p_r * ab_r - p_i * ab_i, p_r * ab_i + p_i * ab_r
    _, _, f_r, f_i = zoh(ar16_ref[...], ai16_ref[...], jnp.exp(dt16_ref[...]))
    b_r, b_i = br_ref[...], bi_ref[...]
    bbr_ref[...] = f_r * b_r - f_i * b_i
    bbi_ref[...] = f_r * b_i + f_i * b_r


def _ssm_prep(a_re, a_im, log_dt, b_re, b_im, npow):
    g, n = a_re.shape
    p = b_re.shape[-1]
    dt = jnp.broadcast_to(log_dt[:, None], (g, n))
    rep = lambda a: jnp.repeat(a, p, axis=-1)
    outs = pl.pallas_call(
        _ssm_prep_body,
        out_shape=(jax.ShapeDtypeStruct((npow, g, n), F32), jax.ShapeDtypeStruct((npow, g, n), F32),
                   jax.ShapeDtypeStruct((g, n * p), F32), jax.ShapeDtypeStruct((g, n * p), F32)),
        name="ssm_prep",
    )(a_re, a_im, dt, rep(a_re), rep(a_im), rep(dt), b_re.reshape(g, n * p), b_im.reshape(g, n * p))
    pw_r, pw_i, bb_r, bb_i = outs
    return (pw_r.reshape(npow, g * n), pw_i.reshape(npow, g * n),
            bb_r.reshape(g, n, p), bb_i.reshape(g, n, p))


def _block_diag(w, gpt):
    g, a, b = w.shape
    w4 = w.reshape(g // gpt, gpt, a, b)
    eye = jnp.eye(gpt, dtype=w.dtype)
    return jnp.einsum('jgab,gh->jgahb', w4, eye).reshape(g // gpt, gpt * a, gpt * b)


def _scan_perm(seglen):
    new = jnp.arange(ROWS)
    grp, rem = new // (SUBLANES * seglen), new % (SUBLANES * seglen)
    old = grp * (SUBLANES * seglen) + (rem % SUBLANES) * seglen + rem // SUBLANES
    return (old[:, None] == jnp.arange(ROWS)[None, :]).astype(BF16)


def _mix_in_body(x_ref, sh_ref, sc_ref, w_ref, perm_ref, z_ref, *, bb, r):
    c = z_ref.shape[-1] // 2
    u = (_ln(x_ref[...]) * (1.0 + _rows(sc_ref, bb, r)) + _rows(sh_ref, bb, r)).astype(BF16)
    z_ref[:, :c] = jnp.dot(u, w_ref[:, :c], preferred_element_type=F32)
    up = jnp.dot(perm_ref[...], u, preferred_element_type=F32).astype(BF16)
    z_ref[:, c:] = jnp.dot(up, w_ref[:, c:], preferred_element_type=F32)


def _seq_grid(nb, seq):
    if seq >= ROWS:
        bb, r, nl = 1, ROWS, seq // ROWS
        grid = (nb, nl)
        row_map = lambda b, l: (b * nl + l, 0)
    else:
        bb, r, nl = ROWS // seq, seq, 1
        grid = (nb // bb, 1)
        row_map = lambda b, l: (b, 0)
    seq_map = lambda b, l: (b, 0, 0)
    return bb, r, grid, row_map, seq_map


def _mix_in(x2, sh, sc, w_in_bf, perm, nb, seq):
    t, d = x2.shape
    bb, r, grid, row_map, seq_map = _seq_grid(nb, seq)
    const2 = lambda b, l: (0, 0)
    return pl.pallas_call(
        functools.partial(_mix_in_body, bb=bb, r=r),
        grid=grid,
        in_specs=[pl.BlockSpec((ROWS, d), row_map),
                  pl.BlockSpec((bb, 1, d), seq_map),
                  pl.BlockSpec((bb, 1, d), seq_map),
                  pl.BlockSpec((d, d), const2),
                  pl.BlockSpec((ROWS, ROWS), const2)],
        out_specs=pl.BlockSpec((ROWS, d), row_map),
        out_shape=jax.ShapeDtypeStruct((t, d), F32),
        compiler_params=_cparams("arbitrary", "arbitrary"),
        name="mix_in",
    )(x2, sh, sc, w_in_bf, perm)


def _pool_body(z_ref, pre_ref, wp_ref, ps_ref, gp_ref, o_ref, carry_ref, *, bb, r, start_pos):
    li = pl.program_id(1)
    c = z_ref.shape[-1]
    gw = c // len(POOL_WINDOWS)
    rp = POOL_HALO + r

    @pl.when(li == 0)
    def _():
        carry_ref[...] = pre_ref[...]

    za = z_ref[...].reshape(bb, r, c)
    xp3 = jnp.concatenate([carry_ref[...], za], axis=1)
    carry_ref[...] = xp3[:, r:, :]
    xp = xp3.reshape(bb * rp, c)
    pos1 = lax.broadcasted_iota(I32, (bb, r, gw), 1) + (start_pos + 1) + li * r
    outs = []
    ssq = jnp.zeros((bb * r, 1), F32)
    for gi, w in enumerate(POOL_WINDOWS):
        cols = slice(gi * gw, (gi + 1) * gw)
        s = xp[:, cols]
        sh = 1
        while sh < w:
            s = s + pltpu.roll(s, sh, 0)
            sh *= 2
        win = s.reshape(bb, rp, gw)[:, POOL_HALO:, :]
        cnt = jnp.minimum(pos1, w).astype(F32)
        d = (win / cnt - za[:, :, cols]).reshape(bb * r, gw)
        y = jnp.dot(d.astype(BF16), wp_ref[gi], preferred_element_type=F32) * ps_ref[:, cols]
        ssq = ssq + jnp.sum(y * y, axis=-1, keepdims=True)
        outs.append(y)
    scale = lax.rsqrt(ssq * (1.0 / c) + LN_EPS)
    for gi, y in enumerate(outs):
        cols = slice(gi * gw, (gi + 1) * gw)
        o_ref[:, cols] = (y * scale * gp_ref[:, cols]).astype(o_ref.dtype)


def _pool(z, prefix16, w_pool_bf, pool_scale, g_pool, nb, seq, start_pos):
    t = z.shape[0]
    c = pool_scale.shape[-1]
    bb, r, grid, row_map, seq_map = _seq_grid(nb, seq)
    const2 = lambda b, l: (0, 0)
    return pl.pallas_call(
        functools.partial(_pool_body, bb=bb, r=r, start_pos=start_pos),
        grid=grid,
        in_specs=[pl.BlockSpec((ROWS, c), row_map),
                  pl.BlockSpec((bb, POOL_HALO, c), seq_map),
                  pl.BlockSpec(w_pool_bf.shape, lambda b, l: (0, 0, 0)),
                  pl.BlockSpec((1, c), const2),
                  pl.BlockSpec((1, c), const2)],
        out_specs=pl.BlockSpec((ROWS, c), row_map),
        out_shape=jax.ShapeDtypeStruct((t, c), BF16),
        scratch_shapes=[pltpu.VMEM((bb, POOL_HALO, c), F32)],
        compiler_params=_cparams("arbitrary", "arbitrary"),
        name="pool",
    )(z, prefix16, w_pool_bf, pool_scale.reshape(1, c), g_pool.reshape(1, c))


def _cmul_add(x_r, x_i, m_r, m_i, y_r, y_i):
    return x_r + m_r * y_r - m_i * y_i, x_i + m_r * y_i + m_i * y_r


def _ssm_body(z_ref, h0r_ref, h0i_ref, bbr_ref, bbi_ref, cr_ref, nci_ref, dsk_ref, pr_ref, pi_ref,
              wg_ref, bg_ref, gs_ref, unperm_ref, o_ref, hfr_ref, hfi_ref, hr_s, hi_s, y_s, car_s, cai_s,
              *, seglen, chained):
    li = pl.program_id(1)
    c = z_ref.shape[-1]
    nstate = hr_s.shape[-1]
    ntile = nstate // SCAN_W
    cw = c // ntile
    u = z_ref[...]
    ub = u.astype(BF16)
    for j in range(ntile):
        sl = slice(j * SCAN_W, (j + 1) * SCAN_W)
        uj = ub[:, j * cw:(j + 1) * cw]
        hr_s[:, sl] = jnp.dot(uj, bbr_ref[j], preferred_element_type=F32)
        hi_s[:, sl] = jnp.dot(uj, bbi_ref[j], preferred_element_type=F32)

    if chained:
        @pl.when(li == 0)
        def _():
            car_s[...] = h0r_ref[0]
            cai_s[...] = h0i_ref[0]

    grp_rows = SUBLANES * seglen
    bc8 = lambda v: jnp.broadcast_to(v, (SUBLANES, SCAN_W))
    for j in range(ntile):
        sl = slice(j * SCAN_W, (j + 1) * SCAN_W)
        a_r, a_i = bc8(pr_ref[0:1, sl]), bc8(pi_ref[0:1, sl])

        def step(t, h, base, sl=sl, a_r=a_r, a_i=a_i):
            off = pl.multiple_of(base + t * SUBLANES, SUBLANES)
            h_r, h_i = _cmul_add(hr_s[pl.ds(off, SUBLANES), sl], hi_s[pl.ds(off, SUBLANES), sl],
                                 a_r, a_i, h[0], h[1])
            hr_s[pl.ds(off, SUBLANES), sl] = h_r
            hi_s[pl.ds(off, SUBLANES), sl] = h_i
            return h_r, h_i

        if chained:
            zero = jnp.zeros((SUBLANES, SCAN_W), F32)
            e_r, e_i = lax.fori_loop(0, seglen, functools.partial(step, base=0), (zero, zero), unroll=4)
            al_r, al_i = pr_ref[seglen - 1:seglen, sl], pi_ref[seglen - 1:seglen, sl]
            s_r, s_i = car_s[:, sl], cai_s[:, sl]
            ent_r, ent_i = [s_r], [s_i]
            for i in range(SUBLANES):
                s_r, s_i = _cmul_add(e_r[i:i + 1], e_i[i:i + 1], al_r, al_i, s_r, s_i)
                if i + 1 < SUBLANES:
                    ent_r.append(s_r)
                    ent_i.append(s_i)
            car_s[:, sl] = s_r
            cai_s[:, sl] = s_i
            hfr_ref[0, :, sl] = s_r
            hfi_ref[0, :, sl] = s_i
            ent_r, ent_i = jnp.concatenate(ent_r, axis=0), jnp.concatenate(ent_i, axis=0)

            def fix(t, _, sl=sl, ent_r=ent_r, ent_i=ent_i):
                off = pl.multiple_of(t * SUBLANES, SUBLANES)
                x_r, x_i = _cmul_add(hr_s[pl.ds(off, SUBLANES), sl], hi_s[pl.ds(off, SUBLANES), sl],
                                     bc8(pr_ref[pl.ds(t, 1), sl]), bc8(pi_ref[pl.ds(t, 1), sl]), ent_r, ent_i)
                hr_s[pl.ds(off, SUBLANES), sl] = x_r
                hi_s[pl.ds(off, SUBLANES), sl] = x_i
                return 0

            lax.fori_loop(0, seglen, fix, 0, unroll=4)
        else:
            for g in range(z_ref.shape[0] // grp_rows):
                rs = slice(g * SUBLANES, (g + 1) * SUBLANES)
                e_r, e_i = lax.fori_loop(0, seglen, functools.partial(step, base=g * grp_rows),
                                         (h0r_ref[rs, sl], h0i_ref[rs, sl]), unroll=True)
                hfr_ref[rs, sl] = e_r
                hfi_ref[rs, sl] = e_i

    for j in range(ntile):
        sl = slice(j * SCAN_W, (j + 1) * SCAN_W)
        cs = slice(j * cw, (j + 1) * cw)
        y_s[:, cs] = (jnp.dot(hr_s[:, sl].astype(BF16), cr_ref[j], preferred_element_type=F32)
                      + jnp.dot(hi_s[:, sl].astype(BF16), nci_ref[j], preferred_element_type=F32)
                      + dsk_ref[:, cs] * u[:, cs])
    y = y_s[...]
    g = 0.5 * y * (1.0 + jnp.tanh(0.7978845608028654 * (y + 0.044715 * (y * y * y))))
    gate = jnp.dot(g.astype(BF16), wg_ref[...], preferred_element_type=F32) + bg_ref[...]
    out = g * _sigmoid(gate)
    scale = lax.rsqrt(jnp.mean(out * out, axis=-1, keepdims=True) + LN_EPS)
    outp = (out * scale * gs_ref[...]).astype(BF16)
    o_ref[...] = jnp.dot(unperm_ref[...], outp, preferred_element_type=F32).astype(o_ref.dtype)


def _ssm(z, h0r, h0i, bbr_bd, bbi_bd, cr_bd, nci_bd, d_skip, pw_r, pw_i, w_glu_bf, b_glu, g_ssm, unperm,
         nb, seq):
    t = z.shape[0]
    c = d_skip.shape[-1]
    nstate = pw_r.shape[-1]
    bb, r, grid, row_map, seq_map = _seq_grid(nb, seq)
    chained = bb == 1
    seglen = r // SUBLANES if chained else r
    const2 = lambda b, l: (0, 0)
    const3 = lambda b, l: (0, 0, 0)
    full = lambda a: pl.BlockSpec(a.shape, const3 if a.ndim == 3 else const2)
    right_half = (lambda b, l: (row_map(b, l)[0], 1))
    if chained:
        st_spec = pl.BlockSpec((1, 1, nstate), seq_map)
        st_shape = jax.ShapeDtypeStruct((nb, 1, nstate), F32)
    else:
        st_spec = pl.BlockSpec((bb, nstate), lambda b, l: (b, 0))
        st_shape = jax.ShapeDtypeStruct((nb, nstate), F32)
    return pl.pallas_call(
        functools.partial(_ssm_body, seglen=seglen, chained=chained),
        grid=grid,
        in_specs=[pl.BlockSpec((ROWS, c), right_half),
                  st_spec, st_spec,
                  full(bbr_bd), full(bbi_bd), full(cr_bd), full(nci_bd),
                  pl.BlockSpec((1, c), const2),
                  full(pw_r), full(pw_i),
                  full(w_glu_bf),
                  pl.BlockSpec((1, c), const2),
                  pl.BlockSpec((1, c), const2),
                  pl.BlockSpec((ROWS, ROWS), const2)],
        out_specs=[pl.BlockSpec((ROWS, c), row_map), st_spec, st_spec],
        out_shape=[jax.ShapeDtypeStruct((t, c), BF16), st_shape, st_shape],
        scratch_shapes=[pltpu.VMEM((ROWS, nstate), F32), pltpu.VMEM((ROWS, nstate), F32),
                        pltpu.VMEM((ROWS, c), F32),
                        pltpu.VMEM((1, nstate), F32), pltpu.VMEM((1, nstate), F32)],
        compiler_params=_cparams("arbitrary", "arbitrary"),
        name="ssm",
    )(z, h0r, h0i, bbr_bd, bbi_bd, cr_bd, nci_bd, d_skip.reshape(1, c), pw_r, pw_i,
      w_glu_bf, b_glu.reshape(1, c), g_ssm.reshape(1, c), unperm)


def _route(s_t, bias_t):
    e, tn = s_t.shape
    per = e // N_EXPERT_GROUPS
    neg = -jnp.inf
    sb = s_t + bias_t
    rowl = lax.broadcasted_iota(I32, (per, tn), 0)
    gscore = []
    for g in range(N_EXPERT_GROUPS):
        blk = sb[g * per:(g + 1) * per]
        m1 = jnp.max(blk, axis=0, keepdims=True)
        i1 = jnp.min(jnp.where(blk == m1, rowl, per), axis=0, keepdims=True)
        m2 = jnp.max(jnp.where(rowl == i1, neg, blk), axis=0, keepdims=True)
        gscore.append(m1 + m2)
    cur = jnp.concatenate(gscore, axis=0)
    rowg = lax.broadcasted_iota(I32, cur.shape, 0)
    gsel = jnp.zeros(cur.shape, F32)
    for _ in range(TOPK_GROUPS):
        m = jnp.max(cur, axis=0, keepdims=True)
        hit = rowg == jnp.min(jnp.where(cur == m, rowg, N_EXPERT_GROUPS), axis=0, keepdims=True)
        gsel = jnp.where(hit, 1.0, gsel)
        cur = jnp.where(hit, neg, cur)
    cur = jnp.concatenate(
        [jnp.where(jnp.broadcast_to(gsel[g:g + 1], (per, tn)) > 0.0, sb[g * per:(g + 1) * per], neg)
         for g in range(N_EXPERT_GROUPS)], axis=0)
    rowe = lax.broadcasted_iota(I32, (e, tn), 0)
    idxs, vals = [], []
    for _ in range(TOP_K):
        m = jnp.max(cur, axis=0, keepdims=True)
        idx = jnp.min(jnp.where(cur == m, rowe, e), axis=0, keepdims=True)
        hit = rowe == idx
        idxs.append(idx)
        vals.append(jnp.sum(jnp.where(hit, s_t, 0.0), axis=0, keepdims=True))
        cur = jnp.where(hit, neg, cur)
    w = jnp.concatenate(vals, axis=0)
    w = w / jnp.sum(w, axis=0, keepdims=True) * ROUTED_SCALE
    return jnp.concatenate(idxs, axis=0), w


def _out_proj_body(x_ref, ya_ref, ys_ref, g1_ref, sh2_ref, sc2_ref, wo_ref, lg_ref, lb_ref, wrt_ref, rb_ref,
                   x1_ref, h_ref, ei_ref, ew_ref, *, bb, r):
    ca = ya_ref.shape[-1]
    m = (jnp.dot(ya_ref[...], wo_ref[:ca, :], preferred_element_type=F32)
         + jnp.dot(ys_ref[...], wo_ref[ca:, :], preferred_element_type=F32))
    res = DN_ALPHA * x_ref[...] + (1.0 + _rows(g1_ref, bb, r)) * m
    x1 = _ln(res) * lg_ref[...] + lb_ref[...]
    x1_ref[...] = x1
    h = _ln(x1) * (1.0 + _rows(sc2_ref, bb, r)) + _rows(sh2_ref, bb, r)
    h_ref[...] = _pack_pairs(h)
    logit_t = lax.dot_general(wrt_ref[...], h.astype(BF16), (((1,), (1,)), ((), ())),
                              preferred_element_type=F32)
    idx, w = _route(_sigmoid(logit_t), rb_ref[...])
    ei_ref[...] = idx
    ew_ref[...] = w


def _out_proj(x2, ya, ys, g1, sh2, sc2, w_out_bf, ln_g, ln_b, w_router_t_bf, router_bias, nb, seq,
              h_all, t_all, row_off):
    t, d = x2.shape
    ca = ya.shape[-1]
    e = w_router_t_bf.shape[0]
    bb, r, grid, row_map, seq_map = _seq_grid(nb, seq)
    const2 = lambda b, l: (0, 0)
    tok_map = lambda b, l: (0, row_map(b, l)[0])
    blk_off = row_off // ROWS
    h_map = lambda b, l: (row_map(b, l)[0] + blk_off, 0)
    in_specs = [pl.BlockSpec((ROWS, d), row_map),
                pl.BlockSpec((ROWS, ca), row_map),
                pl.BlockSpec((ROWS, ca), row_map),
                pl.BlockSpec((bb, 1, d), seq_map),
                pl.BlockSpec((bb, 1, d), seq_map),
                pl.BlockSpec((bb, 1, d), seq_map),
                pl.BlockSpec((d, d), const2),
                pl.BlockSpec((1, d), const2),
                pl.BlockSpec((1, d), const2),
                pl.BlockSpec((e, d), const2),
                pl.BlockSpec((e, 1), const2)]
    args = [x2, ya, ys, g1, sh2, sc2, w_out_bf, ln_g.reshape(1, d), ln_b.reshape(1, d),
            w_router_t_bf, router_bias.reshape(e, 1)]
    body = functools.partial(_out_proj_body, bb=bb, r=r)
    aliases = {}
    if h_all is not None:
        in_specs.append(pl.BlockSpec(memory_space=pl.ANY))
        args.append(h_all)
        aliases = {len(args) - 1: 1}
        inner = body
        body = lambda *refs: inner(*refs[:11], *refs[12:])
    return pl.pallas_call(
        body,
        grid=grid,
        in_specs=in_specs,
        out_specs=[pl.BlockSpec((ROWS, d), row_map),
                   pl.BlockSpec((ROWS, d // 2), h_map),
                   pl.BlockSpec((TOP_K, ROWS), tok_map),
                   pl.BlockSpec((TOP_K, ROWS), tok_map)],
        out_shape=[jax.ShapeDtypeStruct((t, d), F32),
                   jax.ShapeDtypeStruct((t_all, d // 2), U32),
                   jax.ShapeDtypeStruct((TOP_K, t), I32),
                   jax.ShapeDtypeStruct((TOP_K, t), F32)],
        input_output_aliases=aliases,
        compiler_params=_cparams("arbitrary", "arbitrary"),
        name="out_proj",
    )(*args)


def _experts_body(pb_ref, pe_ref, plo_ref, pfl_ref, np_ref, tokc_ref, tokn_ref, dstp_ref, dstc_ref, h_hbm,
                  wg_ref, wu_ref, wd_ref, y_hbm, xbuf, ybuf, gsem, ssem):
    p = pl.program_id(0)
    n_pairs = np_ref[0]
    valid = p < n_pairs
    b = pb_ref[p]
    slot = b % 2
    first = (pfl_ref[p] & 1) == 1
    bm, c = xbuf.shape[1], xbuf.shape[2]

    def gather_start(tok_ref, s):
        for i in range(bm):
            pltpu.make_async_copy(h_hbm.at[pl.ds(tok_ref[0, 0, i], 1), :],
                                  xbuf.at[s, pl.ds(i, 1), :], gsem.at[s]).start()

    def gather_wait(s):
        pltpu.make_async_copy(h_hbm.at[pl.ds(0, bm), :], xbuf.at[s], gsem.at[s]).wait()

    def scatter_start(dst_ref, s):
        for i in range(bm):
            pltpu.make_async_copy(ybuf.at[s, pl.ds(i, 1), :],
                                  y_hbm.at[pl.ds(dst_ref[0, 0, i], 1), :], ssem.at[0]).start()

    def scatter_wait():
        pltpu.make_async_copy(ybuf.at[0], y_hbm.at[pl.ds(0, bm), :], ssem.at[0]).wait()

    def load_x():
        x_lo, x_hi = _unpack_pairs(xbuf[slot])
        return x_lo.astype(BF16), x_hi.astype(BF16)

    def expert(x_lo, x_hi):
        def proj(w_ref):
            return (jnp.dot(x_lo, w_ref[0, :c, :].astype(BF16), preferred_element_type=F32)
                    + jnp.dot(x_hi, w_ref[0, c:, :].astype(BF16), preferred_element_type=F32))

        g = proj(wg_ref)
        u = proj(wu_ref)
        a = (g * _sigmoid(g) * u).astype(BF16)
        return _pack_pairs(jnp.dot(a, wd_ref[0].astype(BF16), preferred_element_type=F32))

    @pl.when(p == 0)
    def _():
        gather_start(tokc_ref, 0)
        ybuf[1] = jnp.zeros(ybuf.shape[1:], ybuf.dtype)

    @pl.when(jnp.logical_and(valid, first))
    def _():
        @pl.when(b > 0)
        def _():
            scatter_wait()

        gather_wait(slot)
        x_lo, x_hi = load_x()
        gather_start(tokn_ref, 1 - slot)
        scatter_start(dstp_ref, 1 - slot)
        ybuf[slot] = expert(x_lo, x_hi)

    @pl.when(jnp.logical_and(valid, jnp.logical_not(first)))
    def _():
        y = expert(*load_x())
        mine = lax.broadcasted_iota(I32, y.shape, 0) >= plo_ref[p]
        ybuf[slot] = jnp.where(mine, y, ybuf[slot])

    @pl.when(p == n_pairs - 1)
    def _():
        scatter_wait()
        scatter_start(dstc_ref, slot)
        scatter_wait()
        gather_wait(1 - slot)


def _experts(h_pack, w_gate, w_up, w_down, pairs, tok_blocks, dst_blocks):
    t, c = h_pack.shape
    d = 2 * c
    e, _, de = w_gate.shape
    nblk = tok_blocks.shape[0]
    bm = MOE_BM
    pb, pe, plo, pfl, n_pairs = pairs
    cur = lambda p, pb, pe, plo, pfl, n: (pb[p], 0, 0)
    nxt = lambda p, pb, pe, plo, pfl, n: (jnp.minimum(pb[p] + 1, nblk - 1), 0, 0)
    prv = lambda p, pb, pe, plo, pfl, n: (jnp.maximum(pb[p] - 1, 0), 0, 0)
    wmap = lambda p, pb, pe, plo, pfl, n: (pe[p], 0, 0)
    smem_blk = lambda m: pl.BlockSpec((1, 1, bm), m, memory_space=pltpu.SMEM)
    grid_spec = pltpu.PrefetchScalarGridSpec(
        num_scalar_prefetch=5,
        grid=(pb.shape[0],),
        in_specs=[smem_blk(cur), smem_blk(nxt), smem_blk(prv), smem_blk(cur),
                  pl.BlockSpec(memory_space=pl.ANY),
                  pl.BlockSpec((1, d, de), wmap),
                  pl.BlockSpec((1, d, de), wmap),
                  pl.BlockSpec((1, de, d), wmap)],
        out_specs=pl.BlockSpec(memory_space=pl.ANY),
        scratch_shapes=[pltpu.VMEM((2, bm, c), U32), pltpu.VMEM((2, bm, c), U32),
                        pltpu.SemaphoreType.DMA((2,)), pltpu.SemaphoreType.DMA((1,))],
    )
    return pl.pallas_call(
        _experts_body,
        grid_spec=grid_spec,
        out_shape=jax.ShapeDtypeStruct((nblk * bm, c), U32),
        compiler_params=_cparams("arbitrary"),
        name="experts",
    )(pb, pe, plo, pfl, n_pairs, tok_blocks, tok_blocks, dst_blocks, dst_blocks, h_pack, w_gate, w_up, w_down)


def _dispatch(eidx_t, n_experts):
    k, t = eidx_t.shape
    a = t * k
    bm = MOE_BM
    nb = a // bm
    flat_e = eidx_t.reshape(a)
    se, order = lax.sort((flat_e, lax.iota(I32, a)), num_keys=1)
    tok = order % t
    ex = lax.iota(I32, n_experts)
    starts = jnp.sum((se[None, :] < ex[:, None]).astype(I32), axis=1)
    seb = se.reshape(nb, bm)
    e_lo, e_hi = seb[:, 0], seb[:, bm - 1]
    npair = e_hi - e_lo + 1
    cum = jnp.cumsum(npair)
    off = cum - npair
    p = lax.iota(I32, nb + n_experts)
    pb = jnp.minimum(jnp.sum((cum[None, :] <= p[:, None]).astype(I32), axis=1), nb - 1)
    pe = jnp.minimum(e_lo[pb] + p - off[pb], e_hi[pb])
    plo = jnp.clip(starts[pe] - pb * bm, 0, bm)
    pfl = (p == off[pb]).astype(I32) + 2 * (p == cum[pb] - 1).astype(I32)
    pairs = (pb.astype(I32), pe.astype(I32), plo.astype(I32), pfl, cum[-1:].astype(I32))
    return pairs, tok.reshape(nb, 1, bm), order.reshape(nb, 1, bm)


def _combine_body(x1_ref, h_ref, w_ref, g2_ref, wsg_ref, wsu_ref, wsd_ref, lg_ref, lb_ref, *rest, bb, r):
    y_refs, o_ref = rest[:TOP_K], rest[TOP_K]
    c = h_ref.shape[-1]
    h_lo, h_hi = _unpack_pairs(h_ref[...])
    h_lo, h_hi = h_lo.astype(BF16), h_hi.astype(BF16)

    def proj(w_ref):
        return (jnp.dot(h_lo, w_ref[:c, :], preferred_element_type=F32)
                + jnp.dot(h_hi, w_ref[c:, :], preferred_element_type=F32))

    g = proj(wsg_ref)
    u = proj(wsu_ref)
    shared = jnp.dot((g * _sigmoid(g) * u).astype(BF16), wsd_ref[...], preferred_element_type=F32)
    w = w_ref[...]
    r_lo = r_hi = None
    for k in range(TOP_K):
        y_lo, y_hi = _unpack_pairs(y_refs[k][...])
        wk = w[:, k:k + 1]
        r_lo = wk * y_lo if r_lo is None else r_lo + wk * y_lo
        r_hi = wk * y_hi if r_hi is None else r_hi + wk * y_hi
    routed = jnp.concatenate([r_lo, r_hi], axis=-1)
    res = DN_ALPHA * x1_ref[...] + (1.0 + _rows(g2_ref, bb, r)) * (routed + shared)
    o_ref[...] = _ln(res) * lg_ref[...] + lb_ref[...]


def _combine(x1, h_pack, y_pack, wsel, g2, wsg_bf, wsu_bf, wsd_bf, ln_g, ln_b, nb, seq, row_off):
    t, d = x1.shape
    t_all, c = h_pack.shape
    ds_ = wsg_bf.shape[1]
    rows = COMB_ROWS
    if seq >= rows:
        bb, r, nl = 1, rows, seq // rows
        grid = (nb, nl)
        row_map = lambda b, l: (b * nl + l, 0)
    else:
        bb, r, nl = rows // seq, seq, 1
        grid = (nb // bb, 1)
        row_map = lambda b, l: (b, 0)
    seq_map = lambda b, l: (b, 0, 0)
    off = row_off // rows
    all_map = lambda b, l: (row_map(b, l)[0] + off, 0)
    const2 = lambda b, l: (0, 0)

    def slot_map(k):
        return lambda b, l: (row_map(b, l)[0] + off + k * (t_all // rows), 0)

    return pl.pallas_call(
        functools.partial(_combine_body, bb=bb, r=r),
        grid=grid,
        in_specs=[pl.BlockSpec((rows, d), row_map),
                  pl.BlockSpec((rows, c), all_map),
                  pl.BlockSpec((rows, TOP_K), all_map),
                  pl.BlockSpec((bb, 1, d), seq_map),
                  pl.BlockSpec((d, ds_), const2),
                  pl.BlockSpec((d, ds_), const2),
                  pl.BlockSpec((ds_, d), const2),
                  pl.BlockSpec((1, d), const2),
                  pl.BlockSpec((1, d), const2)]
                 + [pl.BlockSpec((rows, c), slot_map(k)) for k in range(TOP_K)],
        out_specs=pl.BlockSpec((rows, d), row_map),
        out_shape=jax.ShapeDtypeStruct((t, d), F32),
        compiler_params=_cparams("arbitrary", "arbitrary"),
        name="combine",
    )(x1, h_pack, wsel, g2, wsg_bf, wsu_bf, wsd_bf, ln_g.reshape(1, d), ln_b.reshape(1, d),
      *([y_pack] * TOP_K))


def kernel(x_prompt, x_sample, state_pool, state_ssm_re, state_ssm_im, c_prompt, c_sample, w_ada, b_ada, w_in, w_pool, pool_scale, A_re, A_im, log_dt, B_re, B_im, C_re, C_im, D_skip, w_glu, b_glu, g_pool, g_ssm, w_out, ln1_g, ln1_b, w_router, router_bias, w_e_gate, w_e_up, w_e_down, w_sh_gate, w_sh_up, w_sh_down, ln2_g, ln2_b):
    bp, lp, d = x_prompt.shape
    bs, ls, _ = x_sample.shape
    depth = w_ada.shape[0]
    assert depth == DEPTH == 1
    tp, ts = bp * lp, bs * ls
    t_all = tp + ts
    l = 0
    c_pool = pool_scale.shape[-1]
    n_groups, n_state = A_re.shape[1], A_re.shape[2]
    nstate = n_groups * n_state
    n_experts = w_router.shape[-1]
    gpt = SCAN_W // n_state

    c_all = jnp.concatenate([c_prompt, c_sample], axis=0)
    pad = (-c_all.shape[0]) % SUBLANES
    c_all = jnp.pad(c_all, ((0, pad), (0, 0)))
    mod = _adaln(c_all, w_ada[l], b_ada[l]).reshape(c_all.shape[0], 6, 1, d)
    mod_p = [mod[:bp, i] for i in range(6)]
    mod_s = [mod[bp:bp + bs, i] for i in range(6)]

    w_in_bf = w_in[l].astype(BF16)
    w_pool_bf = w_pool[l].astype(BF16)
    w_glu_bf = w_glu[l].astype(BF16)
    w_out_bf = w_out[l].astype(BF16)
    w_router_t_bf = w_router[l].T.astype(BF16)
    wsg_bf, wsu_bf, wsd_bf = w_sh_gate[l].astype(BF16), w_sh_up[l].astype(BF16), w_sh_down[l].astype(BF16)
    pw_r, pw_i, bb_r, bb_i = _ssm_prep(A_re[l], A_im[l], log_dt[l], B_re[l], B_im[l], ROWS // SUBLANES)
    bbr_bd = _block_diag(jnp.swapaxes(bb_r, 1, 2), gpt).astype(BF16)
    bbi_bd = _block_diag(jnp.swapaxes(bb_i, 1, 2), gpt).astype(BF16)
    cr_bd = _block_diag(jnp.swapaxes(C_re[l], 1, 2), gpt).astype(BF16)
    nci_bd = _block_diag(jnp.swapaxes(-C_im[l], 1, 2), gpt).astype(BF16)
    d_skip = D_skip[l].reshape(-1)

    groups = [
        dict(x=x_prompt.reshape(tp, d), nb=bp, seq=lp, mod=mod_p, start=0, row_off=0,
             prefix=jnp.zeros((bp, POOL_HALO, c_pool), F32),
             h0r=jnp.zeros((bp, 1, nstate), F32), h0i=jnp.zeros((bp, 1, nstate), F32)),
        dict(x=x_sample.reshape(ts, d), nb=bs, seq=ls, mod=mod_s, start=PAST_LEN, row_off=tp,
             prefix=jnp.pad(state_pool[l], ((0, 0), (POOL_HALO - state_pool.shape[2], 0), (0, 0))),
             h0r=state_ssm_re[l].reshape(bs, nstate), h0i=state_ssm_im[l].reshape(bs, nstate)),
    ]

    h_all = None
    for gr in groups:
        sh1, sc1, g1, sh2, sc2, g2 = gr['mod']
        nb, seq = gr['nb'], gr['seq']
        perm = _scan_perm(ROWS // SUBLANES if seq >= ROWS else seq)
        z = _mix_in(gr['x'], sh1, sc1, w_in_bf, perm, nb, seq)
        ya = _pool(z, gr['prefix'], w_pool_bf, pool_scale[l], g_pool[l], nb, seq, gr['start'])
        ys, hfr, hfi = _ssm(z, gr['h0r'], gr['h0i'], bbr_bd, bbi_bd, cr_bd, nci_bd, d_skip, pw_r, pw_i,
                            w_glu_bf, b_glu[l], g_ssm[l], perm.T, nb, seq)
        x1, h_all, eidx_t, ew_t = _out_proj(gr['x'], ya, ys, g1, sh2, sc2, w_out_bf, ln1_g[l], ln1_b[l],
                                            w_router_t_bf, router_bias[l], nb, seq,
                                            h_all, t_all, gr['row_off'])
        gr.update(z=z, x1=x1, eidx_t=eidx_t, ew_t=ew_t, hfr=hfr, hfi=hfi)

    eidx_t = jnp.concatenate([gr['eidx_t'] for gr in groups], axis=1)
    wsel = jnp.concatenate([gr['ew_t'] for gr in groups], axis=1).T
    pairs, tok_blocks, dst_blocks = _dispatch(eidx_t, n_experts)
    y3 = _experts(h_all, w_e_gate[l], w_e_up[l], w_e_down[l], pairs, tok_blocks, dst_blocks)

    outs = []
    for gr in groups:
        outs.append(_combine(gr['x1'], h_all, y3, wsel, gr['mod'][5], wsg_bf, wsu_bf, wsd_bf,
                             ln2_g[l], ln2_b[l], gr['nb'], gr['seq'], gr['row_off']))
    y_prompt = outs[0].reshape(bp, lp, d)
    y_sample = outs[1].reshape(bs, ls, d)

    nbuf = state_pool.shape[2]
    zp = groups[0]['z'].reshape(bp, lp, d)[:, :, :c_pool]
    zs = groups[1]['z'].reshape(bs, ls, d)[:, :, :c_pool]
    pool_p = zp[:, lp - nbuf:, :][None]
    pool_s = jnp.concatenate([state_pool[l], zs], axis=1)[:, -nbuf:, :][None]
    st = lambda a, nb: a.reshape(nb, n_groups, n_state)[None]
    return (y_prompt, y_sample, pool_p, pool_s,
            st(groups[0]['hfr'], bp), st(groups[0]['hfi'], bp),
            st(groups[1]['hfr'], bs), st(groups[1]['hfi'], bs))
```

```python
import functools

import jax
import jax.numpy as jnp
from jax import lax
from jax.experimental import pallas as pl
from jax.experimental.pallas import tpu as pltpu

F32 = jnp.float32
BF16 = jnp.bfloat16
I32 = jnp.int32
U32 = jnp.uint32

DEPTH = 1
PAST_LEN = 16384
POOL_WINDOWS = (2, 4, 8, 16)
POOL_HALO = 16
SSM_P = 16
SSM_N = 64
N_EXPERT_GROUPS = 8
TOPK_GROUPS = 4
TOP_K = 8
ROUTED_SCALE = 2.5
LN_EPS = 1e-5
DN_ALPHA = (2.0 * DEPTH) ** 0.25

ROWS = 256
SUBLANES = 8
LANES = 128
SCAN_W = 512
MOE_BM = 256
PAIR_FIRST_OF_BLOCK, PAIR_NEW_EXPERT, PAIR_HAS_NEXT_EXPERT, PAIR_WEIGHT_SLOT = 1, 2, 4, 8
COMB_ROWS = 128
VMEM_LIMIT = 56 * 1024 * 1024


def _cparams(*sem):
    return pltpu.CompilerParams(dimension_semantics=sem, vmem_limit_bytes=VMEM_LIMIT)


def _ln(x):
    xc = x - jnp.mean(x, axis=-1, keepdims=True)
    return xc * lax.rsqrt(jnp.mean(xc * xc, axis=-1, keepdims=True) + LN_EPS)


def _rows(m_ref, bb, r):
    m = m_ref[...]
    c = m.shape[-1]
    return jnp.broadcast_to(m, (bb, r, c)).reshape(bb * r, c)


def _sigmoid(x):
    return 1.0 / (1.0 + jnp.exp(-x))


def _bf16_bits(x):
    return lax.bitcast_convert_type(x.astype(BF16).astype(F32), U32)


def _pack_pairs(x):
    c = x.shape[-1] // 2
    return (_bf16_bits(x[:, :c]) >> 16) | _bf16_bits(x[:, c:])


def _unpack_pairs(w):
    return (lax.bitcast_convert_type(w << 16, F32),
            lax.bitcast_convert_type(w & jnp.uint32(0xFFFF0000), F32))


def _adaln_body(c_ref, w_ref, b_ref, o_ref):
    c = c_ref[...]
    s = (c * _sigmoid(c)).astype(BF16)
    o_ref[...] = jnp.dot(s, w_ref[...].astype(BF16), preferred_element_type=F32) + b_ref[...]


def _adaln(c_all, w_ada, b_ada):
    bc, d = c_all.shape
    n = w_ada.shape[1]
    tn = 1024
    return pl.pallas_call(
        _adaln_body,
        grid=(n // tn,),
        in_specs=[pl.BlockSpec((bc, d), lambda j: (0, 0)),
                  pl.BlockSpec((d, tn), lambda j: (0, j)),
                  pl.BlockSpec((1, tn), lambda j: (0, j))],
        out_specs=pl.BlockSpec((bc, tn), lambda j: (0, j)),
        out_shape=jax.ShapeDtypeStruct((bc, n), F32),
        compiler_params=_cparams("arbitrary"),
        name="adaln",
    )(c_all, w_ada, b_ada.reshape(1, n))


def _ssm_prep_body(ar_ref, ai_ref, dt_ref, ar16_ref, ai16_ref, dt16_ref, br_ref, bi_ref,
                   pr_ref, pi_ref, bbr_ref, bbi_ref):
    def zoh(a_r, a_i, dt):
        mag = jnp.exp(a_r * dt)
        ab_r, ab_i = mag * jnp.cos(a_i * dt), mag * jnp.sin(a_i * dt)
        den = a_r * a_r + a_i * a_i
        nr = ab_r - 1.0
        return ab_r, ab_i, (nr * a_r + ab_i * a_i) / den, (ab_i * a_r - nr * a_i) / den

    ab_r, ab_i, _, _ = zoh(ar_ref[...], ai_ref[...], jnp.exp(dt_ref[...]))
    p_r, p_i = ab_r, ab_i
    for k in range(pr_ref.shape[0]):
        pr_ref[k] = p_r
        pi_ref[k] = p_i
        p_r, p_i = p_r * ab_r - p_i * ab_i, p_r * ab_i + p_i * ab_r
    _, _, f_r, f_i = zoh(ar16_ref[...], ai16_ref[...], jnp.exp(dt16_ref[...]))
    b_r, b_i = br_ref[...], bi_ref[...]
    bbr_ref[...] = f_r * b_r - f_i * b_i
    bbi_ref[...] = f_r * b_i + f_i * b_r


def _ssm_prep(a_re, a_im, log_dt, b_re, b_im, npow):
    g, n = a_re.shape
    p = b_re.shape[-1]
    dt = jnp.broadcast_to(log_dt[:, None], (g, n))
    rep = lambda a: jnp.repeat(a, p, axis=-1)
    outs = pl.pallas_call(
        _ssm_prep_body,
        out_shape=(jax.ShapeDtypeStruct((npow, g, n), F32), jax.ShapeDtypeStruct((npow, g, n), F32),
                   jax.ShapeDtypeStruct((g, n * p), F32), jax.ShapeDtypeStruct((g, n * p), F32)),
        name="ssm_prep",
    )(a_re, a_im, dt, rep(a_re), rep(a_im), rep(dt), b_re.reshape(g, n * p), b_im.reshape(g, n * p))
    pw_r, pw_i, bb_r, bb_i = outs
    return (pw_r.reshape(npow, g * n), pw_i.reshape(npow, g * n),
            bb_r.reshape(g, n, p), bb_i.reshape(g, n, p))


def _block_diag(w, gpt):
    g, a, b = w.shape
    w4 = w.reshape(g // gpt, gpt, a, b)
    eye = jnp.eye(gpt, dtype=w.dtype)
    return jnp.einsum('jgab,gh->jgahb', w4, eye).reshape(g // gpt, gpt * a, gpt * b)


def _scan_perm(seglen):
    new = jnp.arange(ROWS)
    grp, rem = new // (SUBLANES * seglen), new % (SUBLANES * seglen)
    old = grp * (SUBLANES * seglen) + (rem % SUBLANES) * seglen + rem // SUBLANES
    return (old[:, None] == jnp.arange(ROWS)[None, :]).astype(BF16)


def _mix_in_body(x_ref, sh_ref, sc_ref, w_ref, perm_ref, z_ref, *, bb, r):
    c = z_ref.shape[-1] // 2
    u = (_ln(x_ref[...]) * (1.0 + _rows(sc_ref, bb, r)) + _rows(sh_ref, bb, r)).astype(BF16)
    z_ref[:, :c] = jnp.dot(u, w_ref[:, :c], preferred_element_type=F32)
    up = jnp.dot(perm_ref[...], u, preferred_element_type=F32).astype(BF16)
    z_ref[:, c:] = jnp.dot(up, w_ref[:, c:], preferred_element_type=F32)


def _seq_grid(nb, seq):
    if seq >= ROWS:
        bb, r, nl = 1, ROWS, seq // ROWS
        grid = (nb, nl)
        row_map = lambda b, l: (b * nl + l, 0)
    else:
        bb, r, nl = ROWS // seq, seq, 1
        grid = (nb // bb, 1)
        row_map = lambda b, l: (b, 0)
    seq_map = lambda b, l: (b, 0, 0)
    return bb, r, grid, row_map, seq_map


def _mix_in(x2, sh, sc, w_in_bf, perm, nb, seq):
    t, d = x2.shape
    bb, r, grid, row_map, seq_map = _seq_grid(nb, seq)
    const2 = lambda b, l: (0, 0)
    return pl.pallas_call(
        functools.partial(_mix_in_body, bb=bb, r=r),
        grid=grid,
        in_specs=[pl.BlockSpec((ROWS, d), row_map),
                  pl.BlockSpec((bb, 1, d), seq_map),
                  pl.BlockSpec((bb, 1, d), seq_map),
                  pl.BlockSpec((d, d), const2),
                  pl.BlockSpec((ROWS, ROWS), const2)],
        out_specs=pl.BlockSpec((ROWS, d), row_map),
        out_shape=jax.ShapeDtypeStruct((t, d), F32),
        compiler_params=_cparams("arbitrary", "arbitrary"),
        name="mix_in",
    )(x2, sh, sc, w_in_bf, perm)


def _pool_body(z_ref, pre_ref, wp_ref, ps_ref, gp_ref, o_ref, carry_ref, *, bb, r, start_pos):
    li = pl.program_id(1)
    c = z_ref.shape[-1]
    gw = c // len(POOL_WINDOWS)
    rp = POOL_HALO + r

    @pl.when(li == 0)
    def _():
        carry_ref[...] = pre_ref[...]

    za = z_ref[...].reshape(bb, r, c)
    xp3 = jnp.concatenate([carry_ref[...], za], axis=1)
    carry_ref[...] = xp3[:, r:, :]
    xp = xp3.reshape(bb * rp, c)
    pos1 = lax.broadcasted_iota(I32, (bb, r, gw), 1) + (start_pos + 1) + li * r
    outs = []
    ssq = jnp.zeros((bb * r, 1), F32)
    for gi, w in enumerate(POOL_WINDOWS):
        cols = slice(gi * gw, (gi + 1) * gw)
        s = xp[:, cols]
        sh = 1
        while sh < w:
            s = s + pltpu.roll(s, sh, 0)
            sh *= 2
        win = s.reshape(bb, rp, gw)[:, POOL_HALO:, :]
        cnt = jnp.minimum(pos1, w).astype(F32)
        d = (win / cnt - za[:, :, cols]).reshape(bb * r, gw)
        y = jnp.dot(d.astype(BF16), wp_ref[gi], preferred_element_type=F32) * ps_ref[:, cols]
        ssq = ssq + jnp.sum(y * y, axis=-1, keepdims=True)
        outs.append(y)
    scale = lax.rsqrt(ssq * (1.0 / c) + LN_EPS)
    for gi, y in enumerate(outs):
        cols = slice(gi * gw, (gi + 1) * gw)
        o_ref[:, cols] = (y * scale * gp_ref[:, cols]).astype(o_ref.dtype)


def _pool(z, prefix16, w_pool_bf, pool_scale, g_pool, nb, seq, start_pos):
    t = z.shape[0]
    c = pool_scale.shape[-1]
    bb, r, grid, row_map, seq_map = _seq_grid(nb, seq)
    const2 = lambda b, l: (0, 0)
    return pl.pallas_call(
        functools.partial(_pool_body, bb=bb, r=r, start_pos=start_pos),
        grid=grid,
        in_specs=[pl.BlockSpec((ROWS, c), row_map),
                  pl.BlockSpec((bb, POOL_HALO, c), seq_map),
                  pl.BlockSpec(w_pool_bf.shape, lambda b, l: (0, 0, 0)),
                  pl.BlockSpec((1, c), const2),
                  pl.BlockSpec((1, c), const2)],
        out_specs=pl.BlockSpec((ROWS, c), row_map),
        out_shape=jax.ShapeDtypeStruct((t, c), BF16),
        scratch_shapes=[pltpu.VMEM((bb, POOL_HALO, c), F32)],
        compiler_params=_cparams("arbitrary", "arbitrary"),
        name="pool",
    )(z, prefix16, w_pool_bf, pool_scale.reshape(1, c), g_pool.reshape(1, c))


def _cmul_add(x_r, x_i, m_r, m_i, y_r, y_i):
    return x_r + m_r * y_r - m_i * y_i, x_i + m_r * y_i + m_i * y_r


def _ssm_body(z_ref, h0r_ref, h0i_ref, bbr_ref, bbi_ref, cr_ref, nci_ref, dsk_ref, pr_ref, pi_ref,
              wg_ref, bg_ref, gs_ref, unperm_ref, o_ref, hfr_ref, hfi_ref, hr_s, hi_s, y_s, car_s, cai_s,
              *, seglen, chained):
    li = pl.program_id(1)
    c = z_ref.shape[-1]
    nstate = hr_s.shape[-1]
    ntile = nstate // SCAN_W
    cw = c // ntile
    u = z_ref[...]
    ub = u.astype(BF16)
    for j in range(ntile):
        sl = slice(j * SCAN_W, (j + 1) * SCAN_W)
        uj = ub[:, j * cw:(j + 1) * cw]
        hr_s[:, sl] = jnp.dot(uj, bbr_ref[j], preferred_element_type=F32)
        hi_s[:, sl] = jnp.dot(uj, bbi_ref[j], preferred_element_type=F32)

    if chained:
        @pl.when(li == 0)
        def _():
            car_s[...] = h0r_ref[0]
            cai_s[...] = h0i_ref[0]

    grp_rows = SUBLANES * seglen
    bc8 = lambda v: jnp.broadcast_to(v, (SUBLANES, SCAN_W))
    for j in range(ntile):
        sl = slice(j * SCAN_W, (j + 1) * SCAN_W)
        a_r, a_i = bc8(pr_ref[0:1, sl]), bc8(pi_ref[0:1, sl])

        def step(t, h, base, sl=sl, a_r=a_r, a_i=a_i):
            off = pl.multiple_of(base + t * SUBLANES, SUBLANES)
            h_r, h_i = _cmul_add(hr_s[pl.ds(off, SUBLANES), sl], hi_s[pl.ds(off, SUBLANES), sl],
                                 a_r, a_i, h[0], h[1])
            hr_s[pl.ds(off, SUBLANES), sl] = h_r
            hi_s[pl.ds(off, SUBLANES), sl] = h_i
            return h_r, h_i

        if chained:
            zero = jnp.zeros((SUBLANES, SCAN_W), F32)
            e_r, e_i = lax.fori_loop(0, seglen, functools.partial(step, base=0), (zero, zero), unroll=4)
            al_r, al_i = pr_ref[seglen - 1:seglen, sl], pi_ref[seglen - 1:seglen, sl]
            s_r, s_i = car_s[:, sl], cai_s[:, sl]
            ent_r, ent_i = [s_r], [s_i]
            for i in range(SUBLANES):
                s_r, s_i = _cmul_add(e_r[i:i + 1], e_i[i:i + 1], al_r, al_i, s_r, s_i)
                if i + 1 < SUBLANES:
                    ent_r.append(s_r)
                    ent_i.append(s_i)
            car_s[:, sl] = s_r
            cai_s[:, sl] = s_i
            hfr_ref[0, :, sl] = s_r
            hfi_ref[0, :, sl] = s_i
            ent_r, ent_i = jnp.concatenate(ent_r, axis=0), jnp.concatenate(ent_i, axis=0)

            def fix(t, _, sl=sl, ent_r=ent_r, ent_i=ent_i):
                off = pl.multiple_of(t * SUBLANES, SUBLANES)
                x_r, x_i = _cmul_add(hr_s[pl.ds(off, SUBLANES), sl], hi_s[pl.ds(off, SUBLANES), sl],
                                     bc8(pr_ref[pl.ds(t, 1), sl]), bc8(pi_ref[pl.ds(t, 1), sl]), ent_r, ent_i)
                hr_s[pl.ds(off, SUBLANES), sl] = x_r
                hi_s[pl.ds(off, SUBLANES), sl] = x_i
                return 0

            lax.fori_loop(0, seglen, fix, 0, unroll=4)
        else:
            for g in range(z_ref.shape[0] // grp_rows):
                rs = slice(g * SUBLANES, (g + 1) * SUBLANES)
                e_r, e_i = lax.fori_loop(0, seglen, functools.partial(step, base=g * grp_rows),
                                         (h0r_ref[rs, sl], h0i_ref[rs, sl]), unroll=True)
                hfr_ref[rs, sl] = e_r
                hfi_ref[rs, sl] = e_i

    for j in range(ntile):
        sl = slice(j * SCAN_W, (j + 1) * SCAN_W)
        cs = slice(j * cw, (j + 1) * cw)
        y_s[:, cs] = (jnp.dot(hr_s[:, sl].astype(BF16), cr_ref[j], preferred_element_type=F32)
                      + jnp.dot(hi_s[:, sl].astype(BF16), nci_ref[j], preferred_element_type=F32)
                      + dsk_ref[:, cs] * u[:, cs])
    y = y_s[...]
    g = 0.5 * y * (1.0 + jnp.tanh(0.7978845608028654 * (y + 0.044715 * (y * y * y))))
    gate = jnp.dot(g.astype(BF16), wg_ref[...], preferred_element_type=F32) + bg_ref[...]
    out = g * _sigmoid(gate)
    scale = lax.rsqrt(jnp.mean(out * out, axis=-1, keepdims=True) + LN_EPS)
    outp = (out * scale * gs_ref[...]).astype(BF16)
    o_ref[...] = jnp.dot(unperm_ref[...], outp, preferred_element_type=F32).astype(o_ref.dtype)


def _ssm(z, h0r, h0i, bbr_bd, bbi_bd, cr_bd, nci_bd, d_skip, pw_r, pw_i, w_glu_bf, b_glu, g_ssm, unperm,
         nb, seq):
    t = z.shape[0]
    c = d_skip.shape[-1]
    nstate = pw_r.shape[-1]
    bb, r, grid, row_map, seq_map = _seq_grid(nb, seq)
    chained = bb == 1
    seglen = r // SUBLANES if chained else r
    const2 = lambda b, l: (0, 0)
    const3 = lambda b, l: (0, 0, 0)
    full = lambda a: pl.BlockSpec(a.shape, const3 if a.ndim == 3 else const2)
    right_half = (lambda b, l: (row_map(b, l)[0], 1))
    if chained:
        st_spec = pl.BlockSpec((1, 1, nstate), seq_map)
        st_shape = jax.ShapeDtypeStruct((nb, 1, nstate), F32)
    else:
        st_spec = pl.BlockSpec((bb, nstate), lambda b, l: (b, 0))
        st_shape = jax.ShapeDtypeStruct((nb, nstate), F32)
    return pl.pallas_call(
        functools.partial(_ssm_body, seglen=seglen, chained=chained),
        grid=grid,
        in_specs=[pl.BlockSpec((ROWS, c), right_half),
                  st_spec, st_spec,
                  full(bbr_bd), full(bbi_bd), full(cr_bd), full(nci_bd),
                  pl.BlockSpec((1, c), const2),
                  full(pw_r), full(pw_i),
                  full(w_glu_bf),
                  pl.BlockSpec((1, c), const2),
                  pl.BlockSpec((1, c), const2),
                  pl.BlockSpec((ROWS, ROWS), const2)],
        out_specs=[pl.BlockSpec((ROWS, c), row_map), st_spec, st_spec],
        out_shape=[jax.ShapeDtypeStruct((t, c), BF16), st_shape, st_shape],
        scratch_shapes=[pltpu.VMEM((ROWS, nstate), F32), pltpu.VMEM((ROWS, nstate), F32),
                        pltpu.VMEM((ROWS, c), F32),
                        pltpu.VMEM((1, nstate), F32), pltpu.VMEM((1, nstate), F32)],
        compiler_params=_cparams("arbitrary", "arbitrary"),
        name="ssm",
    )(z, h0r, h0i, bbr_bd, bbi_bd, cr_bd, nci_bd, d_skip.reshape(1, c), pw_r, pw_i,
      w_glu_bf, b_glu.reshape(1, c), g_ssm.reshape(1, c), unperm)


def _route(s_t, bias_t):
    e, tn = s_t.shape
    per = e // N_EXPERT_GROUPS
    neg = -jnp.inf
    sb = s_t + bias_t
    rowl = lax.broadcasted_iota(I32, (per, tn), 0)
    gscore = []
    for g in range(N_EXPERT_GROUPS):
        blk = sb[g * per:(g + 1) * per]
        m1 = jnp.max(blk, axis=0, keepdims=True)
        i1 = jnp.min(jnp.where(blk == m1, rowl, per), axis=0, keepdims=True)
        m2 = jnp.max(jnp.where(rowl == i1, neg, blk), axis=0, keepdims=True)
        gscore.append(m1 + m2)
    cur = jnp.concatenate(gscore, axis=0)
    rowg = lax.broadcasted_iota(I32, cur.shape, 0)
    gsel = jnp.zeros(cur.shape, F32)
    for _ in range(TOPK_GROUPS):
        m = jnp.max(cur, axis=0, keepdims=True)
        hit = rowg == jnp.min(jnp.where(cur == m, rowg, N_EXPERT_GROUPS), axis=0, keepdims=True)
        gsel = jnp.where(hit, 1.0, gsel)
        cur = jnp.where(hit, neg, cur)
    cur = jnp.concatenate(
        [jnp.where(jnp.broadcast_to(gsel[g:g + 1], (per, tn)) > 0.0, sb[g * per:(g + 1) * per], neg)
         for g in range(N_EXPERT_GROUPS)], axis=0)
    rowe = lax.broadcasted_iota(I32, (e, tn), 0)
    idxs, vals = [], []
    for _ in range(TOP_K):
        m = jnp.max(cur, axis=0, keepdims=True)
        idx = jnp.min(jnp.where(cur == m, rowe, e), axis=0, keepdims=True)
        hit = rowe == idx
        idxs.append(idx)
        vals.append(jnp.sum(jnp.where(hit, s_t, 0.0), axis=0, keepdims=True))
        cur = jnp.where(hit, neg, cur)
    w = jnp.concatenate(vals, axis=0)
    w = w / jnp.sum(w, axis=0, keepdims=True) * ROUTED_SCALE
    return jnp.concatenate(idxs, axis=0), w


def _out_proj_body(x_ref, ya_ref, ys_ref, g1_ref, sh2_ref, sc2_ref, wo_ref, lg_ref, lb_ref, wrt_ref, rb_ref,
                   x1_ref, h_ref, ei_ref, ew_ref, *, bb, r):
    ca = ya_ref.shape[-1]
    m = (jnp.dot(ya_ref[...], wo_ref[:ca, :], preferred_element_type=F32)
         + jnp.dot(ys_ref[...], wo_ref[ca:, :], preferred_element_type=F32))
    res = DN_ALPHA * x_ref[...] + (1.0 + _rows(g1_ref, bb, r)) * m
    x1 = _ln(res) * lg_ref[...] + lb_ref[...]
    x1_ref[...] = x1
    h = _ln(x1) * (1.0 + _rows(sc2_ref, bb, r)) + _rows(sh2_ref, bb, r)
    h_ref[...] = _pack_pairs(h)
    logit_t = lax.dot_general(wrt_ref[...], h.astype(BF16), (((1,), (1,)), ((), ())),
                              preferred_element_type=F32)
    idx, w = _route(_sigmoid(logit_t), rb_ref[...])
    ei_ref[...] = idx
    ew_ref[...] = w


def _out_proj(x2, ya, ys, g1, sh2, sc2, w_out_bf, ln_g, ln_b, w_router_t_bf, router_bias, nb, seq,
              h_all, t_all, row_off):
    t, d = x2.shape
    ca = ya.shape[-1]
    e = w_router_t_bf.shape[0]
    bb, r, grid, row_map, seq_map = _seq_grid(nb, seq)
    const2 = lambda b, l: (0, 0)
    tok_map = lambda b, l: (0, row_map(b, l)[0])
    blk_off = row_off // ROWS
    h_map = lambda b, l: (row_map(b, l)[0] + blk_off, 0)
    in_specs = [pl.BlockSpec((ROWS, d), row_map),
                pl.BlockSpec((ROWS, ca), row_map),
                pl.BlockSpec((ROWS, ca), row_map),
                pl.BlockSpec((bb, 1, d), seq_map),
                pl.BlockSpec((bb, 1, d), seq_map),
                pl.BlockSpec((bb, 1, d), seq_map),
                pl.BlockSpec((d, d), const2),
                pl.BlockSpec((1, d), const2),
                pl.BlockSpec((1, d), const2),
                pl.BlockSpec((e, d), const2),
                pl.BlockSpec((e, 1), const2)]
    args = [x2, ya, ys, g1, sh2, sc2, w_out_bf, ln_g.reshape(1, d), ln_b.reshape(1, d),
            w_router_t_bf, router_bias.reshape(e, 1)]
    body = functools.partial(_out_proj_body, bb=bb, r=r)
    aliases = {}
    if h_all is not None:
        in_specs.append(pl.BlockSpec(memory_space=pl.ANY))
        args.append(h_all)
        aliases = {len(args) - 1: 1}
        inner = body
        body = lambda *refs: inner(*refs[:11], *refs[12:])
    return pl.pallas_call(
        body,
        grid=grid,
        in_specs=in_specs,
        out_specs=[pl.BlockSpec((ROWS, d), row_map),
                   pl.BlockSpec((ROWS, d // 2), h_map),
                   pl.BlockSpec((TOP_K, ROWS), tok_map),
                   pl.BlockSpec((TOP_K, ROWS), tok_map)],
        out_shape=[jax.ShapeDtypeStruct((t, d), F32),
                   jax.ShapeDtypeStruct((t_all, d // 2), U32),
                   jax.ShapeDtypeStruct((TOP_K, t), I32),
                   jax.ShapeDtypeStruct((TOP_K, t), F32)],
        input_output_aliases=aliases,
        compiler_params=_cparams("arbitrary", "arbitrary"),
        name="out_proj",
    )(*args)


def _experts_body(pb_ref, pe_ref, plo_ref, pfl_ref, pne_ref, np_ref, tokc_ref, tokn_ref, dstp_ref, dstc_ref,
                  h_hbm, wg_hbm, wu_hbm, wd_hbm, y_hbm, xbuf, ybuf, xlo_s, xhi_s, wg_s, wu_s, wd_s,
                  gsem, ssem, wsem):
    p = pl.program_id(0)
    n_pairs = np_ref[0]
    valid = p < n_pairs
    b = pb_ref[p]
    slot = b % 2
    flags = pfl_ref[p]
    first = (flags & PAIR_FIRST_OF_BLOCK) != 0
    new_expert = (flags & PAIR_NEW_EXPERT) != 0
    has_next = (flags & PAIR_HAS_NEXT_EXPERT) != 0
    wslot = (flags & PAIR_WEIGHT_SLOT) // PAIR_WEIGHT_SLOT
    bm, c = xbuf.shape[1], xbuf.shape[2]
    weights = ((wg_hbm, wg_s), (wu_hbm, wu_s), (wd_hbm, wd_s))

    def weights_start(e, s):
        for w_hbm, w_s in weights:
            pltpu.make_async_copy(w_hbm.at[e], w_s.at[s], wsem.at[s]).start()

    def weights_wait(s):
        for w_hbm, w_s in weights:
            pltpu.make_async_copy(w_hbm.at[0], w_s.at[s], wsem.at[s]).wait()

    def gather_start(tok_ref, s):
        for i in range(bm):
            pltpu.make_async_copy(h_hbm.at[pl.ds(tok_ref[0, 0, i], 1), :],
                                  xbuf.at[s, pl.ds(i, 1), :], gsem.at[s]).start()

    def gather_wait(s):
        pltpu.make_async_copy(h_hbm.at[pl.ds(0, bm), :], xbuf.at[s], gsem.at[s]).wait()

    def scatter_start(dst_ref, s):
        for i in range(bm):
            pltpu.make_async_copy(ybuf.at[s, pl.ds(i, 1), :],
                                  y_hbm.at[pl.ds(dst_ref[0, 0, i], 1), :], ssem.at[0]).start()

    def scatter_wait():
        pltpu.make_async_copy(ybuf.at[0], y_hbm.at[pl.ds(0, bm), :], ssem.at[0]).wait()

    def expert():
        def proj(w_s):
            return (jnp.dot(xlo_s[...], w_s[wslot, :c, :].astype(BF16), preferred_element_type=F32)
                    + jnp.dot(xhi_s[...], w_s[wslot, c:, :].astype(BF16), preferred_element_type=F32))

        g = proj(wg_s)
        u = proj(wu_s)
        a = (g * _sigmoid(g) * u).astype(BF16)
        return _pack_pairs(jnp.dot(a, wd_s[wslot].astype(BF16), preferred_element_type=F32))

    @pl.when(p == 0)
    def _():
        weights_start(pe_ref[0], 0)
        gather_start(tokc_ref, 0)
        ybuf[1] = jnp.zeros(ybuf.shape[1:], ybuf.dtype)

    @pl.when(jnp.logical_and(valid, new_expert))
    def _():
        weights_wait(wslot)

        @pl.when(has_next)
        def _():
            weights_start(pne_ref[p], 1 - wslot)

    @pl.when(jnp.logical_and(valid, first))
    def _():
        @pl.when(b > 0)
        def _():
            scatter_wait()

        gather_wait(slot)
        x_lo, x_hi = _unpack_pairs(xbuf[slot])
        xlo_s[...] = x_lo.astype(BF16)
        xhi_s[...] = x_hi.astype(BF16)
        gather_start(tokn_ref, 1 - slot)
        scatter_start(dstp_ref, 1 - slot)
        ybuf[slot] = expert()

    @pl.when(jnp.logical_and(valid, jnp.logical_not(first)))
    def _():
        y = expert()
        mine = lax.broadcasted_iota(I32, y.shape, 0) >= plo_ref[p]
        ybuf[slot] = jnp.where(mine, y, ybuf[slot])

    @pl.when(p == n_pairs - 1)
    def _():
        scatter_wait()
        scatter_start(dstc_ref, slot)
        scatter_wait()
        gather_wait(1 - slot)


def _experts(h_pack, w_gate, w_up, w_down, pairs, tok_blocks, dst_blocks):
    t, c = h_pack.shape
    d = 2 * c
    e, _, de = w_gate.shape
    nblk = tok_blocks.shape[0]
    bm = MOE_BM
    pb, pe, plo, pfl, pne, n_pairs = pairs
    cur = lambda p, pb, *_: (pb[p], 0, 0)
    nxt = lambda p, pb, *_: (jnp.minimum(pb[p] + 1, nblk - 1), 0, 0)
    prv = lambda p, pb, *_: (jnp.maximum(pb[p] - 1, 0), 0, 0)
    smem_blk = lambda m: pl.BlockSpec((1, 1, bm), m, memory_space=pltpu.SMEM)
    hbm = pl.BlockSpec(memory_space=pl.ANY)
    grid_spec = pltpu.PrefetchScalarGridSpec(
        num_scalar_prefetch=6,
        grid=(pb.shape[0],),
        in_specs=[smem_blk(cur), smem_blk(nxt), smem_blk(prv), smem_blk(cur), hbm, hbm, hbm, hbm],
        out_specs=hbm,
        scratch_shapes=[pltpu.VMEM((2, bm, c), U32), pltpu.VMEM((2, bm, c), U32),
                        pltpu.VMEM((bm, c), BF16), pltpu.VMEM((bm, c), BF16),
                        pltpu.VMEM((2, d, de), F32), pltpu.VMEM((2, d, de), F32), pltpu.VMEM((2, de, d), F32),
                        pltpu.SemaphoreType.DMA((2,)), pltpu.SemaphoreType.DMA((1,)),
                        pltpu.SemaphoreType.DMA((2,))],
    )
    return pl.pallas_call(
        _experts_body,
        grid_spec=grid_spec,
        out_shape=jax.ShapeDtypeStruct((nblk * bm, c), U32),
        compiler_params=_cparams("arbitrary"),
        name="experts",
    )(pb, pe, plo, pfl, pne, n_pairs, tok_blocks, tok_blocks, dst_blocks, dst_blocks,
      h_pack, w_gate, w_up, w_down)


def _dispatch(eidx_t, n_experts):
    k, t = eidx_t.shape
    a = t * k
    bm = MOE_BM
    nb = a // bm
    flat_e = eidx_t.reshape(a)
    se, order = lax.sort((flat_e, lax.iota(I32, a)), num_keys=1)
    tok = order % t
    ex = lax.iota(I32, n_experts)
    starts = jnp.sum((se[None, :] < ex[:, None]).astype(I32), axis=1)
    seb = se.reshape(nb, bm)
    e_lo, e_hi = seb[:, 0], seb[:, bm - 1]
    npair = e_hi - e_lo + 1
    cum = jnp.cumsum(npair)
    off = cum - npair
    p = lax.iota(I32, nb + n_experts)
    pb = jnp.minimum(jnp.sum((cum[None, :] <= p[:, None]).astype(I32), axis=1), nb - 1)
    pe = jnp.minimum(e_lo[pb] + p - off[pb], e_hi[pb])
    plo = jnp.clip(starts[pe] - pb * bm, 0, bm)
    valid = p < cum[-1]
    new_e = jnp.logical_and(valid, jnp.concatenate([jnp.ones((1,), bool), pe[1:] != pe[:-1]]))
    wslot = (jnp.cumsum(new_e.astype(I32)) - 1) % 2
    later = jnp.logical_and(pe[None, :] > pe[:, None], valid[None, :])
    pne = jnp.min(jnp.where(later, pe[None, :], n_experts), axis=1)
    pfl = (PAIR_FIRST_OF_BLOCK * (p == off[pb]) + PAIR_NEW_EXPERT * new_e
           + PAIR_HAS_NEXT_EXPERT * (pne < n_experts) + PAIR_WEIGHT_SLOT * wslot).astype(I32)
    pne = jnp.minimum(pne, n_experts - 1)
    pairs = (pb.astype(I32), pe.astype(I32), plo.astype(I32), pfl, pne.astype(I32), cum[-1:].astype(I32))
    return pairs, tok.reshape(nb, 1, bm), order.reshape(nb, 1, bm)


def _combine_body(x1_ref, h_ref, w_ref, g2_ref, wsg_ref, wsu_ref, wsd_ref, lg_ref, lb_ref, *rest, bb, r):
    y_refs, o_ref = rest[:TOP_K], rest[TOP_K]
    c = h_ref.shape[-1]
    h_lo, h_hi = _unpack_pairs(h_ref[...])
    h_lo, h_hi = h_lo.astype(BF16), h_hi.astype(BF16)

    def proj(w_ref):
        return (jnp.dot(h_lo, w_ref[:c, :], preferred_element_type=F32)
                + jnp.dot(h_hi, w_ref[c:, :], preferred_element_type=F32))

    g = proj(wsg_ref)
    u = proj(wsu_ref)
    shared = jnp.dot((g * _sigmoid(g) * u).astype(BF16), wsd_ref[...], preferred_element_type=F32)
    w = w_ref[...]
    r_lo = r_hi = None
    for k in range(TOP_K):
        y_lo, y_hi = _unpack_pairs(y_refs[k][...])
        wk = w[:, k:k + 1]
        r_lo = wk * y_lo if r_lo is None else r_lo + wk * y_lo
        r_hi = wk * y_hi if r_hi is None else r_hi + wk * y_hi
    routed = jnp.concatenate([r_lo, r_hi], axis=-1)
    res = DN_ALPHA * x1_ref[...] + (1.0 + _rows(g2_ref, bb, r)) * (routed + shared)
    o_ref[...] = _ln(res) * lg_ref[...] + lb_ref[...]


def _combine(x1, h_pack, y_pack, wsel, g2, wsg_bf, wsu_bf, wsd_bf, ln_g, ln_b, nb, seq, row_off):
    t, d = x1.shape
    t_all, c = h_pack.shape
    ds_ = wsg_bf.shape[1]
    rows = COMB_ROWS
    if seq >= rows:
        bb, r, nl = 1, rows, seq // rows
        grid = (nb, nl)
        row_map = lambda b, l: (b * nl + l, 0)
    else:
        bb, r, nl = rows // seq, seq, 1
        grid = (nb // bb, 1)
        row_map = lambda b, l: (b, 0)
    seq_map = lambda b, l: (b, 0, 0)
    off = row_off // rows
    all_map = lambda b, l: (row_map(b, l)[0] + off, 0)
    const2 = lambda b, l: (0, 0)

    def slot_map(k):
        return lambda b, l: (row_map(b, l)[0] + off + k * (t_all // rows), 0)

    return pl.pallas_call(
        functools.partial(_combine_body, bb=bb, r=r),
        grid=grid,
        in_specs=[pl.BlockSpec((rows, d), row_map),
                  pl.BlockSpec((rows, c), all_map),
                  pl.BlockSpec((rows, TOP_K), all_map),
                  pl.BlockSpec((bb, 1, d), seq_map),
                  pl.BlockSpec((d, ds_), const2),
                  pl.BlockSpec((d, ds_), const2),
                  pl.BlockSpec((ds_, d), const2),
                  pl.BlockSpec((1, d), const2),
                  pl.BlockSpec((1, d), const2)]
                 + [pl.BlockSpec((rows, c), slot_map(k)) for k in range(TOP_K)],
        out_specs=pl.BlockSpec((rows, d), row_map),
        out_shape=jax.ShapeDtypeStruct((t, d), F32),
        compiler_params=_cparams("arbitrary", "arbitrary"),
        name="combine",
    )(x1, h_pack, wsel, g2, wsg_bf, wsu_bf, wsd_bf, ln_g.reshape(1, d), ln_b.reshape(1, d),
      *([y_pack] * TOP_K))


def kernel(x_prompt, x_sample, state_pool, state_ssm_re, state_ssm_im, c_prompt, c_sample, w_ada, b_ada, w_in, w_pool, pool_scale, A_re, A_im, log_dt, B_re, B_im, C_re, C_im, D_skip, w_glu, b_glu, g_pool, g_ssm, w_out, ln1_g, ln1_b, w_router, router_bias, w_e_gate, w_e_up, w_e_down, w_sh_gate, w_sh_up, w_sh_down, ln2_g, ln2_b):
    bp, lp, d = x_prompt.shape
    bs, ls, _ = x_sample.shape
    depth = w_ada.shape[0]
    assert depth == DEPTH == 1
    tp, ts = bp * lp, bs * ls
    t_all = tp + ts
    l = 0
    c_pool = pool_scale.shape[-1]
    n_groups, n_state = A_re.shape[1], A_re.shape[2]
    nstate = n_groups * n_state
    n_experts = w_router.shape[-1]
    gpt = SCAN_W // n_state

    c_all = jnp.concatenate([c_prompt, c_sample], axis=0)
    pad = (-c_all.shape[0]) % SUBLANES
    c_all = jnp.pad(c_all, ((0, pad), (0, 0)))
    mod = _adaln(c_all, w_ada[l], b_ada[l]).reshape(c_all.shape[0], 6, 1, d)
    mod_p = [mod[:bp, i] for i in range(6)]
    mod_s = [mod[bp:bp + bs, i] for i in range(6)]

    w_in_bf = w_in[l].astype(BF16)
    w_pool_bf = w_pool[l].astype(BF16)
    w_glu_bf = w_glu[l].astype(BF16)
    w_out_bf = w_out[l].astype(BF16)
    w_router_t_bf = w_router[l].T.astype(BF16)
    wsg_bf, wsu_bf, wsd_bf = w_sh_gate[l].astype(BF16), w_sh_up[l].astype(BF16), w_sh_down[l].astype(BF16)
    pw_r, pw_i, bb_r, bb_i = _ssm_prep(A_re[l], A_im[l], log_dt[l], B_re[l], B_im[l], ROWS // SUBLANES)
    bbr_bd = _block_diag(jnp.swapaxes(bb_r, 1, 2), gpt).astype(BF16)
    bbi_bd = _block_diag(jnp.swapaxes(bb_i, 1, 2), gpt).astype(BF16)
    cr_bd = _block_diag(jnp.swapaxes(C_re[l], 1, 2), gpt).astype(BF16)
    nci_bd = _block_diag(jnp.swapaxes(-C_im[l], 1, 2), gpt).astype(BF16)
    d_skip = D_skip[l].reshape(-1)

    groups = [
        dict(x=x_prompt.reshape(tp, d), nb=bp, seq=lp, mod=mod_p, start=0, row_off=0,
             prefix=jnp.zeros((bp, POOL_HALO, c_pool), F32),
             h0r=jnp.zeros((bp, 1, nstate), F32), h0i=jnp.zeros((bp, 1, nstate), F32)),
        dict(x=x_sample.reshape(ts, d), nb=bs, seq=ls, mod=mod_s, start=PAST_LEN, row_off=tp,
             prefix=jnp.pad(state_pool[l], ((0, 0), (POOL_HALO - state_pool.shape[2], 0), (0, 0))),
             h0r=state_ssm_re[l].reshape(bs, nstate), h0i=state_ssm_im[l].reshape(bs, nstate)),
    ]

    h_all = None
    for gr in groups:
        sh1, sc1, g1, sh2, sc2, g2 = gr['mod']
        nb, seq = gr['nb'], gr['seq']
        perm = _scan_perm(ROWS // SUBLANES if seq >= ROWS else seq)
        z = _mix_in(gr['x'], sh1, sc1, w_in_bf, perm, nb, seq)
        ya = _pool(z, gr['prefix'], w_pool_bf, pool_scale[l], g_pool[l], nb, seq, gr['start'])
        ys, hfr, hfi = _ssm(z, gr['h0r'], gr['h0i'], bbr_bd, bbi_bd, cr_bd, nci_bd, d_skip, pw_r, pw_i,
                            w_glu_bf, b_glu[l], g_ssm[l], perm.T, nb, seq)
        x1, h_all, eidx_t, ew_t = _out_proj(gr['x'], ya, ys, g1, sh2, sc2, w_out_bf, ln1_g[l], ln1_b[l],
                                            w_router_t_bf, router_bias[l], nb, seq,
                                            h_all, t_all, gr['row_off'])
        gr.update(z=z, x1=x1, eidx_t=eidx_t, ew_t=ew_t, hfr=hfr, hfi=hfi)

    eidx_t = jnp.concatenate([gr['eidx_t'] for gr in groups], axis=1)
    wsel = jnp.concatenate([gr['ew_t'] for gr in groups], axis=1).T
    pairs, tok_blocks, dst_blocks = _dispatch(eidx_t, n_experts)
    y3 = _experts(h_all, w_e_gate[l], w_e_up[l], w_e_down[l], pairs, tok_blocks, dst_blocks)

    outs = []
    for gr in groups:
        outs.append(_combine(gr['x1'], h_all, y3, wsel, gr['mod'][5], wsg_bf, wsu_bf, wsd_bf,
                             ln2_g[l], ln2_b[l], gr['nb'], gr['seq'], gr['row_off']))
    y_prompt = outs[0].reshape(bp, lp, d)
    y_sample = outs[1].reshape(bs, ls, d)

    nbuf = state_pool.shape[2]
    zp = groups[0]['z'].reshape(bp, lp, d)[:, :, :c_pool]
    zs = groups[1]['z'].reshape(bs, ls, d)[:, :, :c_pool]
    pool_p = zp[:, lp - nbuf:, :][None]
    pool_s = jnp.concatenate([state_pool[l], zs], axis=1)[:, -nbuf:, :][None]
    st = lambda a, nb: a.reshape(nb, n_groups, n_state)[None]
    return (y_prompt, y_sample, pool_p, pool_s,
            st(groups[0]['hfr'], bp), st(groups[0]['hfi'], bp),
            st(groups[1]['hfr'], bs), st(groups[1]['hfi'], bs))
```

```python
import functools

import jax
import jax.numpy as jnp
from jax import lax
from jax.experimental import pallas as pl
from jax.experimental.pallas import tpu as pltpu

F32 = jnp.float32
BF16 = jnp.bfloat16
I32 = jnp.int32
U32 = jnp.uint32

DEPTH = 1
PAST_LEN = 16384
POOL_WINDOWS = (2, 4, 8, 16)
POOL_HALO = 16
SSM_P = 16
SSM_N = 64
N_EXPERT_GROUPS = 8
TOPK_GROUPS = 4
TOP_K = 8
ROUTED_SCALE = 2.5
LN_EPS = 1e-5
DN_ALPHA = (2.0 * DEPTH) ** 0.25

ROWS = 256
SUBLANES = 8
LANES = 128
SCAN_W = 512
MOE_BM = 256
PAIR_FIRST_OF_BLOCK, PAIR_NEW_EXPERT, PAIR_HAS_NEXT_EXPERT, PAIR_WEIGHT_SLOT = 1, 2, 4, 8
COMB_ROWS = 256
VMEM_LIMIT = 56 * 1024 * 1024


def _cparams(*sem):
    return pltpu.CompilerParams(dimension_semantics=sem, vmem_limit_bytes=VMEM_LIMIT)


def _ln(x):
    xc = x - jnp.mean(x, axis=-1, keepdims=True)
    return xc * lax.rsqrt(jnp.mean(xc * xc, axis=-1, keepdims=True) + LN_EPS)


def _rows(m_ref, bb, r):
    m = m_ref[...]
    c = m.shape[-1]
    return jnp.broadcast_to(m, (bb, r, c)).reshape(bb * r, c)


def _sigmoid(x):
    return 1.0 / (1.0 + jnp.exp(-x))


def _bf16_bits(x):
    return lax.bitcast_convert_type(x.astype(BF16).astype(F32), U32)


def _pack_pairs(x):
    c = x.shape[-1] // 2
    return (_bf16_bits(x[:, :c]) >> 16) | _bf16_bits(x[:, c:])


def _unpack_pairs(w):
    return (lax.bitcast_convert_type(w << 16, F32),
            lax.bitcast_convert_type(w & jnp.uint32(0xFFFF0000), F32))


def _adaln_body(c_ref, w_ref, b_ref, o_ref):
    c = c_ref[...]
    s = (c * _sigmoid(c)).astype(BF16)
    o_ref[...] = jnp.dot(s, w_ref[...].astype(BF16), preferred_element_type=F32) + b_ref[...]


def _adaln(c_all, w_ada, b_ada):
    bc, d = c_all.shape
    n = w_ada.shape[1]
    tn = 1024
    return pl.pallas_call(
        _adaln_body,
        grid=(n // tn,),
        in_specs=[pl.BlockSpec((bc, d), lambda j: (0, 0)),
                  pl.BlockSpec((d, tn), lambda j: (0, j)),
                  pl.BlockSpec((1, tn), lambda j: (0, j))],
        out_specs=pl.BlockSpec((bc, tn), lambda j: (0, j)),
        out_shape=jax.ShapeDtypeStruct((bc, n), F32),
        compiler_params=_cparams("arbitrary"),
        name="adaln",
    )(c_all, w_ada, b_ada.reshape(1, n))


def _ssm_prep_body(ar_ref, ai_ref, dt_ref, ar16_ref, ai16_ref, dt16_ref, br_ref, bi_ref,
                   pr_ref, pi_ref, bbr_ref, bbi_ref):
    def zoh(a_r, a_i, dt):
        mag = jnp.exp(a_r * dt)
        ab_r, ab_i = mag * jnp.cos(a_i * dt), mag * jnp.sin(a_i * dt)
        den = a_r * a_r + a_i * a_i
        nr = ab_r - 1.0
        return ab_r, ab_i, (nr * a_r + ab_i * a_i) / den, (ab_i * a_r - nr * a_i) / den

    ab_r, ab_i, _, _ = zoh(ar_ref[...], ai_ref[...], jnp.exp(dt_ref[...]))
    p_r, p_i = ab_r, ab_i
    for k in range(pr_ref.shape[0]):
        pr_ref[k] = p_r
        pi_ref[k] = p_i
        p_r, p_i = p_r * ab_r - p_i * ab_i, p_r * ab_i + p_i * ab_r
    _, _, f_r, f_i = zoh(ar16_ref[...], ai16_ref[...], jnp.exp(dt16_ref[...]))
    b_r, b_i = br_ref[...], bi_ref[...]
    bbr_ref[...] = f_r * b_r - f_i * b_i
    bbi_ref[...] = f_r * b_i + f_i * b_r


def _ssm_prep(a_re, a_im, log_dt, b_re, b_im, npow):
    g, n = a_re.shape
    p = b_re.shape[-1]
    dt = jnp.broadcast_to(log_dt[:, None], (g, n))
    rep = lambda a: jnp.repeat(a, p, axis=-1)
    outs = pl.pallas_call(
        _ssm_prep_body,
        out_shape=(jax.ShapeDtypeStruct((npow, g, n), F32), jax.ShapeDtypeStruct((npow, g, n), F32),
                   jax.ShapeDtypeStruct((g, n * p), F32), jax.ShapeDtypeStruct((g, n * p), F32)),
        name="ssm_prep",
    )(a_re, a_im, dt, rep(a_re), rep(a_im), rep(dt), b_re.reshape(g, n * p), b_im.reshape(g, n * p))
    pw_r, pw_i, bb_r, bb_i = outs
    return (pw_r.reshape(npow, g * n), pw_i.reshape(npow, g * n),
            bb_r.reshape(g, n, p), bb_i.reshape(g, n, p))


def _block_diag(w, gpt):
    g, a, b = w.shape
    w4 = w.reshape(g // gpt, gpt, a, b)
    eye = jnp.eye(gpt, dtype=w.dtype)
    return jnp.einsum('jgab,gh->jgahb', w4, eye).reshape(g // gpt, gpt * a, gpt * b)


def _scan_perm(seglen):
    new = jnp.arange(ROWS)
    grp, rem = new // (SUBLANES * seglen), new % (SUBLANES * seglen)
    old = grp * (SUBLANES * seglen) + (rem % SUBLANES) * seglen + rem // SUBLANES
    return (old[:, None] == jnp.arange(ROWS)[None, :]).astype(BF16)


def _mix_in_body(x_ref, sh_ref, sc_ref, w_ref, perm_ref, z_ref, *, bb, r):
    c = z_ref.shape[-1] // 2
    u = (_ln(x_ref[...]) * (1.0 + _rows(sc_ref, bb, r)) + _rows(sh_ref, bb, r)).astype(BF16)
    z_ref[:, :c] = jnp.dot(u, w_ref[:, :c], preferred_element_type=F32)
    up = jnp.dot(perm_ref[...], u, preferred_element_type=F32).astype(BF16)
    z_ref[:, c:] = jnp.dot(up, w_ref[:, c:], preferred_element_type=F32)


def _seq_grid(nb, seq):
    if seq >= ROWS:
        bb, r, nl = 1, ROWS, seq // ROWS
        grid = (nb, nl)
        row_map = lambda b, l: (b * nl + l, 0)
    else:
        bb, r, nl = ROWS // seq, seq, 1
        grid = (nb // bb, 1)
        row_map = lambda b, l: (b, 0)
    seq_map = lambda b, l: (b, 0, 0)
    return bb, r, grid, row_map, seq_map


def _mix_in(x2, sh, sc, w_in_bf, perm, nb, seq):
    t, d = x2.shape
    bb, r, grid, row_map, seq_map = _seq_grid(nb, seq)
    const2 = lambda b, l: (0, 0)
    return pl.pallas_call(
        functools.partial(_mix_in_body, bb=bb, r=r),
        grid=grid,
        in_specs=[pl.BlockSpec((ROWS, d), row_map),
                  pl.BlockSpec((bb, 1, d), seq_map),
                  pl.BlockSpec((bb, 1, d), seq_map),
                  pl.BlockSpec((d, d), const2),
                  pl.BlockSpec((ROWS, ROWS), const2)],
        out_specs=pl.BlockSpec((ROWS, d), row_map),
        out_shape=jax.ShapeDtypeStruct((t, d), F32),
        compiler_params=_cparams("arbitrary", "arbitrary"),
        name="mix_in",
    )(x2, sh, sc, w_in_bf, perm)


def _pool_body(z_ref, pre_ref, wp_ref, ps_ref, gp_ref, o_ref, carry_ref, *, bb, r, start_pos):
    li = pl.program_id(1)
    c = z_ref.shape[-1]
    gw = c // len(POOL_WINDOWS)
    rp = POOL_HALO + r

    @pl.when(li == 0)
    def _():
        carry_ref[...] = pre_ref[...]

    za = z_ref[...].reshape(bb, r, c)
    xp3 = jnp.concatenate([carry_ref[...], za], axis=1)
    carry_ref[...] = xp3[:, r:, :]
    xp = xp3.reshape(bb * rp, c)
    pos1 = lax.broadcasted_iota(I32, (bb, r, gw), 1) + (start_pos + 1) + li * r
    outs = []
    ssq = jnp.zeros((bb * r, 1), F32)
    for gi, w in enumerate(POOL_WINDOWS):
        cols = slice(gi * gw, (gi + 1) * gw)
        s = xp[:, cols]
        sh = 1
        while sh < w:
            s = s + pltpu.roll(s, sh, 0)
            sh *= 2
        win = s.reshape(bb, rp, gw)[:, POOL_HALO:, :]
        cnt = jnp.minimum(pos1, w).astype(F32)
        d = (win / cnt - za[:, :, cols]).reshape(bb * r, gw)
        y = jnp.dot(d.astype(BF16), wp_ref[gi], preferred_element_type=F32) * ps_ref[:, cols]
        ssq = ssq + jnp.sum(y * y, axis=-1, keepdims=True)
        outs.append(y)
    scale = lax.rsqrt(ssq * (1.0 / c) + LN_EPS)
    for gi, y in enumerate(outs):
        cols = slice(gi * gw, (gi + 1) * gw)
        o_ref[:, cols] = (y * scale * gp_ref[:, cols]).astype(o_ref.dtype)


def _pool(z, prefix16, w_pool_bf, pool_scale, g_pool, nb, seq, start_pos):
    t = z.shape[0]
    c = pool_scale.shape[-1]
    bb, r, grid, row_map, seq_map = _seq_grid(nb, seq)
    const2 = lambda b, l: (0, 0)
    return pl.pallas_call(
        functools.partial(_pool_body, bb=bb, r=r, start_pos=start_pos),
        grid=grid,
        in_specs=[pl.BlockSpec((ROWS, c), row_map),
                  pl.BlockSpec((bb, POOL_HALO, c), seq_map),
                  pl.BlockSpec(w_pool_bf.shape, lambda b, l: (0, 0, 0)),
                  pl.BlockSpec((1, c), const2),
                  pl.BlockSpec((1, c), const2)],
        out_specs=pl.BlockSpec((ROWS, c), row_map),
        out_shape=jax.ShapeDtypeStruct((t, c), BF16),
        scratch_shapes=[pltpu.VMEM((bb, POOL_HALO, c), F32)],
        compiler_params=_cparams("arbitrary", "arbitrary"),
        name="pool",
    )(z, prefix16, w_pool_bf, pool_scale.reshape(1, c), g_pool.reshape(1, c))


def _cmul_add(x_r, x_i, m_r, m_i, y_r, y_i):
    return x_r + m_r * y_r - m_i * y_i, x_i + m_r * y_i + m_i * y_r


def _ssm_body(z_ref, h0r_ref, h0i_ref, bbr_ref, bbi_ref, cr_ref, nci_ref, dsk_ref, pr_ref, pi_ref,
              wg_ref, bg_ref, gs_ref, unperm_ref, o_ref, hfr_ref, hfi_ref, hr_s, hi_s, y_s, car_s, cai_s,
              *, seglen, chained):
    li = pl.program_id(1)
    c = z_ref.shape[-1]
    nstate = hr_s.shape[-1]
    ntile = nstate // SCAN_W
    cw = c // ntile
    u = z_ref[...]
    ub = u.astype(BF16)
    for j in range(ntile):
        sl = slice(j * SCAN_W, (j + 1) * SCAN_W)
        uj = ub[:, j * cw:(j + 1) * cw]
        hr_s[:, sl] = jnp.dot(uj, bbr_ref[j], preferred_element_type=F32)
        hi_s[:, sl] = jnp.dot(uj, bbi_ref[j], preferred_element_type=F32)

    if chained:
        @pl.when(li == 0)
        def _():
            car_s[...] = h0r_ref[0]
            cai_s[...] = h0i_ref[0]

    grp_rows = SUBLANES * seglen
    bc8 = lambda v: jnp.broadcast_to(v, (SUBLANES, SCAN_W))
    for j in range(ntile):
        sl = slice(j * SCAN_W, (j + 1) * SCAN_W)
        a_r, a_i = bc8(pr_ref[0:1, sl]), bc8(pi_ref[0:1, sl])

        def step(t, h, base, sl=sl, a_r=a_r, a_i=a_i):
            off = pl.multiple_of(base + t * SUBLANES, SUBLANES)
            h_r, h_i = _cmul_add(hr_s[pl.ds(off, SUBLANES), sl], hi_s[pl.ds(off, SUBLANES), sl],
                                 a_r, a_i, h[0], h[1])
            hr_s[pl.ds(off, SUBLANES), sl] = h_r
            hi_s[pl.ds(off, SUBLANES), sl] = h_i
            return h_r, h_i

        if chained:
            zero = jnp.zeros((SUBLANES, SCAN_W), F32)
            e_r, e_i = lax.fori_loop(0, seglen, functools.partial(step, base=0), (zero, zero), unroll=4)
            al_r, al_i = pr_ref[seglen - 1:seglen, sl], pi_ref[seglen - 1:seglen, sl]
            s_r, s_i = car_s[:, sl], cai_s[:, sl]
            ent_r, ent_i = [s_r], [s_i]
            for i in range(SUBLANES):
                s_r, s_i = _cmul_add(e_r[i:i + 1], e_i[i:i + 1], al_r, al_i, s_r, s_i)
                if i + 1 < SUBLANES:
                    ent_r.append(s_r)
                    ent_i.append(s_i)
            car_s[:, sl] = s_r
            cai_s[:, sl] = s_i
            hfr_ref[0, :, sl] = s_r
            hfi_ref[0, :, sl] = s_i
            ent_r, ent_i = jnp.concatenate(ent_r, axis=0), jnp.concatenate(ent_i, axis=0)

            def fix(t, _, sl=sl, ent_r=ent_r, ent_i=ent_i):
                off = pl.multiple_of(t * SUBLANES, SUBLANES)
                x_r, x_i = _cmul_add(hr_s[pl.ds(off, SUBLANES), sl], hi_s[pl.ds(off, SUBLANES), sl],
                                     bc8(pr_ref[pl.ds(t, 1), sl]), bc8(pi_ref[pl.ds(t, 1), sl]), ent_r, ent_i)
                hr_s[pl.ds(off, SUBLANES), sl] = x_r
                hi_s[pl.ds(off, SUBLANES), sl] = x_i
                return 0

            lax.fori_loop(0, seglen, fix, 0, unroll=4)
        else:
            for g in range(z_ref.shape[0] // grp_rows):
                rs = slice(g * SUBLANES, (g + 1) * SUBLANES)
                e_r, e_i = lax.fori_loop(0, seglen, functools.partial(step, base=g * grp_rows),
                                         (h0r_ref[rs, sl], h0i_ref[rs, sl]), unroll=True)
                hfr_ref[rs, sl] = e_r
                hfi_ref[rs, sl] = e_i

    for j in range(ntile):
        sl = slice(j * SCAN_W, (j + 1) * SCAN_W)
        cs = slice(j * cw, (j + 1) * cw)
        y_s[:, cs] = (jnp.dot(hr_s[:, sl].astype(BF16), cr_ref[j], preferred_element_type=F32)
                      + jnp.dot(hi_s[:, sl].astype(BF16), nci_ref[j], preferred_element_type=F32)
                      + dsk_ref[:, cs] * u[:, cs])
    y = y_s[...]
    g = 0.5 * y * (1.0 + jnp.tanh(0.7978845608028654 * (y + 0.044715 * (y * y * y))))
    gate = jnp.dot(g.astype(BF16), wg_ref[...], preferred_element_type=F32) + bg_ref[...]
    out = g * _sigmoid(gate)
    scale = lax.rsqrt(jnp.mean(out * out, axis=-1, keepdims=True) + LN_EPS)
    outp = (out * scale * gs_ref[...]).astype(BF16)
    o_ref[...] = jnp.dot(unperm_ref[...], outp, preferred_element_type=F32).astype(o_ref.dtype)


def _ssm(z, h0r, h0i, bbr_bd, bbi_bd, cr_bd, nci_bd, d_skip, pw_r, pw_i, w_glu_bf, b_glu, g_ssm, unperm,
         nb, seq):
    t = z.shape[0]
    c = d_skip.shape[-1]
    nstate = pw_r.shape[-1]
    bb, r, grid, row_map, seq_map = _seq_grid(nb, seq)
    chained = bb == 1
    seglen = r // SUBLANES if chained else r
    const2 = lambda b, l: (0, 0)
    const3 = lambda b, l: (0, 0, 0)
    full = lambda a: pl.BlockSpec(a.shape, const3 if a.ndim == 3 else const2)
    right_half = (lambda b, l: (row_map(b, l)[0], 1))
    if chained:
        st_spec = pl.BlockSpec((1, 1, nstate), seq_map)
        st_shape = jax.ShapeDtypeStruct((nb, 1, nstate), F32)
    else:
        st_spec = pl.BlockSpec((bb, nstate), lambda b, l: (b, 0))
        st_shape = jax.ShapeDtypeStruct((nb, nstate), F32)
    return pl.pallas_call(
        functools.partial(_ssm_body, seglen=seglen, chained=chained),
        grid=grid,
        in_specs=[pl.BlockSpec((ROWS, c), right_half),
                  st_spec, st_spec,
                  full(bbr_bd), full(bbi_bd), full(cr_bd), full(nci_bd),
                  pl.BlockSpec((1, c), const2),
                  full(pw_r), full(pw_i),
                  full(w_glu_bf),
                  pl.BlockSpec((1, c), const2),
                  pl.BlockSpec((1, c), const2),
                  pl.BlockSpec((ROWS, ROWS), const2)],
        out_specs=[pl.BlockSpec((ROWS, c), row_map), st_spec, st_spec],
        out_shape=[jax.ShapeDtypeStruct((t, c), BF16), st_shape, st_shape],
        scratch_shapes=[pltpu.VMEM((ROWS, nstate), F32), pltpu.VMEM((ROWS, nstate), F32),
                        pltpu.VMEM((ROWS, c), F32),
                        pltpu.VMEM((1, nstate), F32), pltpu.VMEM((1, nstate), F32)],
        compiler_params=_cparams("arbitrary", "arbitrary"),
        name="ssm",
    )(z, h0r, h0i, bbr_bd, bbi_bd, cr_bd, nci_bd, d_skip.reshape(1, c), pw_r, pw_i,
      w_glu_bf, b_glu.reshape(1, c), g_ssm.reshape(1, c), unperm)


def _route(s_t, bias_t):
    e, tn = s_t.shape
    per = e // N_EXPERT_GROUPS
    neg = -jnp.inf
    sb = s_t + bias_t
    rowl = lax.broadcasted_iota(I32, (per, tn), 0)
    gscore = []
    for g in range(N_EXPERT_GROUPS):
        blk = sb[g * per:(g + 1) * per]
        m1 = jnp.max(blk, axis=0, keepdims=True)
        i1 = jnp.min(jnp.where(blk == m1, rowl, per), axis=0, keepdims=True)
        m2 = jnp.max(jnp.where(rowl == i1, neg, blk), axis=0, keepdims=True)
        gscore.append(m1 + m2)
    cur = jnp.concatenate(gscore, axis=0)
    rowg = lax.broadcasted_iota(I32, cur.shape, 0)
    gsel = jnp.zeros(cur.shape, F32)
    for _ in range(TOPK_GROUPS):
        m = jnp.max(cur, axis=0, keepdims=True)
        hit = rowg == jnp.min(jnp.where(cur == m, rowg, N_EXPERT_GROUPS), axis=0, keepdims=True)
        gsel = jnp.where(hit, 1.0, gsel)
        cur = jnp.where(hit, neg, cur)
    cur = jnp.concatenate(
        [jnp.where(jnp.broadcast_to(gsel[g:g + 1], (per, tn)) > 0.0, sb[g * per:(g + 1) * per], neg)
         for g in range(N_EXPERT_GROUPS)], axis=0)
    rowe = lax.broadcasted_iota(I32, (e, tn), 0)
    idxs, vals = [], []
    for _ in range(TOP_K):
        m = jnp.max(cur, axis=0, keepdims=True)
        idx = jnp.min(jnp.where(cur == m, rowe, e), axis=0, keepdims=True)
        hit = rowe == idx
        idxs.append(idx)
        vals.append(jnp.sum(jnp.where(hit, s_t, 0.0), axis=0, keepdims=True))
        cur = jnp.where(hit, neg, cur)
    w = jnp.concatenate(vals, axis=0)
    w = w / jnp.sum(w, axis=0, keepdims=True) * ROUTED_SCALE
    return jnp.concatenate(idxs, axis=0), w


def _out_proj_body(x_ref, ya_ref, ys_ref, g1_ref, sh2_ref, sc2_ref, wo_ref, lg_ref, lb_ref, wrt_ref, rb_ref,
                   x1_ref, h_ref, ei_ref, ew_ref, *, bb, r):
    ca = ya_ref.shape[-1]
    m = (jnp.dot(ya_ref[...], wo_ref[:ca, :], preferred_element_type=F32)
         + jnp.dot(ys_ref[...], wo_ref[ca:, :], preferred_element_type=F32))
    res = DN_ALPHA * x_ref[...] + (1.0 + _rows(g1_ref, bb, r)) * m
    x1 = _ln(res) * lg_ref[...] + lb_ref[...]
    x1_ref[...] = x1
    h = _ln(x1) * (1.0 + _rows(sc2_ref, bb, r)) + _rows(sh2_ref, bb, r)
    h_ref[...] = _pack_pairs(h)
    logit_t = lax.dot_general(wrt_ref[...], h.astype(BF16), (((1,), (1,)), ((), ())),
                              preferred_element_type=F32)
    idx, w = _route(_sigmoid(logit_t), rb_ref[...])
    ei_ref[...] = idx
    ew_ref[...] = w


def _out_proj(x2, ya, ys, g1, sh2, sc2, w_out_bf, ln_g, ln_b, w_router_t_bf, router_bias, nb, seq,
              h_all, t_all, row_off):
    t, d = x2.shape
    ca = ya.shape[-1]
    e = w_router_t_bf.shape[0]
    bb, r, grid, row_map, seq_map = _seq_grid(nb, seq)
    const2 = lambda b, l: (0, 0)
    tok_map = lambda b, l: (0, row_map(b, l)[0])
    blk_off = row_off // ROWS
    h_map = lambda b, l: (row_map(b, l)[0] + blk_off, 0)
    in_specs = [pl.BlockSpec((ROWS, d), row_map),
                pl.BlockSpec((ROWS, ca), row_map),
                pl.BlockSpec((ROWS, ca), row_map),
                pl.BlockSpec((bb, 1, d), seq_map),
                pl.BlockSpec((bb, 1, d), seq_map),
                pl.BlockSpec((bb, 1, d), seq_map),
                pl.BlockSpec((d, d), const2),
                pl.BlockSpec((1, d), const2),
                pl.BlockSpec((1, d), const2),
                pl.BlockSpec((e, d), const2),
                pl.BlockSpec((e, 1), const2)]
    args = [x2, ya, ys, g1, sh2, sc2, w_out_bf, ln_g.reshape(1, d), ln_b.reshape(1, d),
            w_router_t_bf, router_bias.reshape(e, 1)]
    body = functools.partial(_out_proj_body, bb=bb, r=r)
    aliases = {}
    if h_all is not None:
        in_specs.append(pl.BlockSpec(memory_space=pl.ANY))
        args.append(h_all)
        aliases = {len(args) - 1: 1}
        inner = body
        body = lambda *refs: inner(*refs[:11], *refs[12:])
    return pl.pallas_call(
        body,
        grid=grid,
        in_specs=in_specs,
        out_specs=[pl.BlockSpec((ROWS, d), row_map),
                   pl.BlockSpec((ROWS, d // 2), h_map),
                   pl.BlockSpec((TOP_K, ROWS), tok_map),
                   pl.BlockSpec((TOP_K, ROWS), tok_map)],
        out_shape=[jax.ShapeDtypeStruct((t, d), F32),
                   jax.ShapeDtypeStruct((t_all, d // 2), U32),
                   jax.ShapeDtypeStruct((TOP_K, t), I32),
                   jax.ShapeDtypeStruct((TOP_K, t), F32)],
        input_output_aliases=aliases,
        compiler_params=_cparams("arbitrary", "arbitrary"),
        name="out_proj",
    )(*args)


def _experts_body(pb_ref, pe_ref, plo_ref, pfl_ref, pne_ref, np_ref, tokc_ref, tokn_ref, dstp_ref, dstc_ref,
                  h_hbm, wg_hbm, wu_hbm, wd_hbm, y_hbm, xbuf, ybuf, xlo_s, xhi_s, wg_s, wu_s, wd_s,
                  gsem, ssem, wsem):
    p = pl.program_id(0)
    n_pairs = np_ref[0]
    valid = p < n_pairs
    b = pb_ref[p]
    slot = b % 2
    flags = pfl_ref[p]
    first = (flags & PAIR_FIRST_OF_BLOCK) != 0
    new_expert = (flags & PAIR_NEW_EXPERT) != 0
    has_next = (flags & PAIR_HAS_NEXT_EXPERT) != 0
    wslot = (flags & PAIR_WEIGHT_SLOT) // PAIR_WEIGHT_SLOT
    bm, c = xbuf.shape[1], xbuf.shape[2]
    weights = ((wg_hbm, wg_s), (wu_hbm, wu_s), (wd_hbm, wd_s))

    def weights_start(e, s):
        for w_hbm, w_s in weights:
            pltpu.make_async_copy(w_hbm.at[e], w_s.at[s], wsem.at[s]).start(priority=1)

    def weights_wait(s):
        for w_hbm, w_s in weights:
            pltpu.make_async_copy(w_hbm.at[0], w_s.at[s], wsem.at[s]).wait()

    def gather_start(tok_ref, s):
        for i in range(bm):
            pltpu.make_async_copy(h_hbm.at[pl.ds(tok_ref[0, 0, i], 1), :],
                                  xbuf.at[s, pl.ds(i, 1), :], gsem.at[s]).start()

    def gather_wait(s):
        pltpu.make_async_copy(h_hbm.at[pl.ds(0, bm), :], xbuf.at[s], gsem.at[s]).wait()

    def scatter_start(dst_ref, s):
        for i in range(bm):
            pltpu.make_async_copy(ybuf.at[s, pl.ds(i, 1), :],
                                  y_hbm.at[pl.ds(dst_ref[0, 0, i], 1), :], ssem.at[0]).start(priority=i % 2)

    def scatter_wait():
        pltpu.make_async_copy(ybuf.at[0], y_hbm.at[pl.ds(0, bm), :], ssem.at[0]).wait()

    def expert():
        def proj(w_s):
            return (jnp.dot(xlo_s[...], w_s[wslot, :c, :].astype(BF16), preferred_element_type=F32)
                    + jnp.dot(xhi_s[...], w_s[wslot, c:, :].astype(BF16), preferred_element_type=F32))

        g = proj(wg_s)
        u = proj(wu_s)
        a = (g * _sigmoid(g) * u).astype(BF16)
        return _pack_pairs(jnp.dot(a, wd_s[wslot].astype(BF16), preferred_element_type=F32))

    @pl.when(p == 0)
    def _():
        weights_start(pe_ref[0], 0)
        gather_start(tokc_ref, 0)
        ybuf[1] = jnp.zeros(ybuf.shape[1:], ybuf.dtype)

    @pl.when(jnp.logical_and(valid, new_expert))
    def _():
        weights_wait(wslot)

        @pl.when(has_next)
        def _():
            weights_start(pne_ref[p], 1 - wslot)

    @pl.when(jnp.logical_and(valid, first))
    def _():
        @pl.when(b > 0)
        def _():
            scatter_wait()

        gather_wait(slot)
        x_lo, x_hi = _unpack_pairs(xbuf[slot])
        xlo_s[...] = x_lo.astype(BF16)
        xhi_s[...] = x_hi.astype(BF16)
        gather_start(tokn_ref, 1 - slot)
        scatter_start(dstp_ref, 1 - slot)
        ybuf[slot] = expert()

    @pl.when(jnp.logical_and(valid, jnp.logical_not(first)))
    def _():
        y = expert()
        mine = lax.broadcasted_iota(I32, y.shape, 0) >= plo_ref[p]
        ybuf[slot] = jnp.where(mine, y, ybuf[slot])

    @pl.when(p == n_pairs - 1)
    def _():
        scatter_wait()
        scatter_start(dstc_ref, slot)
        scatter_wait()
        gather_wait(1 - slot)


def _experts(h_pack, w_gate, w_up, w_down, pairs, tok_blocks, dst_blocks):
    t, c = h_pack.shape
    d = 2 * c
    e, _, de = w_gate.shape
    nblk = tok_blocks.shape[0]
    bm = MOE_BM
    pb, pe, plo, pfl, pne, n_pairs = pairs
    cur = lambda p, pb, *_: (pb[p], 0, 0)
    nxt = lambda p, pb, *_: (jnp.minimum(pb[p] + 1, nblk - 1), 0, 0)
    prv = lambda p, pb, *_: (jnp.maximum(pb[p] - 1, 0), 0, 0)
    smem_blk = lambda m: pl.BlockSpec((1, 1, bm), m, memory_space=pltpu.SMEM)
    hbm = pl.BlockSpec(memory_space=pl.ANY)
    grid_spec = pltpu.PrefetchScalarGridSpec(
        num_scalar_prefetch=6,
        grid=(pb.shape[0],),
        in_specs=[smem_blk(cur), smem_blk(nxt), smem_blk(prv), smem_blk(cur), hbm, hbm, hbm, hbm],
        out_specs=hbm,
        scratch_shapes=[pltpu.VMEM((2, bm, c), U32), pltpu.VMEM((2, bm, c), U32),
                        pltpu.VMEM((bm, c), BF16), pltpu.VMEM((bm, c), BF16),
                        pltpu.VMEM((2, d, de), F32), pltpu.VMEM((2, d, de), F32), pltpu.VMEM((2, de, d), F32),
                        pltpu.SemaphoreType.DMA((2,)), pltpu.SemaphoreType.DMA((1,)),
                        pltpu.SemaphoreType.DMA((2,))],
    )
    return pl.pallas_call(
        _experts_body,
        grid_spec=grid_spec,
        out_shape=jax.ShapeDtypeStruct((nblk * bm, c), U32),
        compiler_params=_cparams("arbitrary"),
        name="experts",
    )(pb, pe, plo, pfl, pne, n_pairs, tok_blocks, tok_blocks, dst_blocks, dst_blocks,
      h_pack, w_gate, w_up, w_down)


def _dispatch(eidx_t, n_experts):
    k, t = eidx_t.shape
    a = t * k
    bm = MOE_BM
    nb = a // bm
    flat_e = eidx_t.reshape(a)
    id_bits = (a - 1).bit_length()
    assert id_bits + (n_experts - 1).bit_length() < 32
    keyed = lax.sort(flat_e * (1 << id_bits) + lax.iota(I32, a))
    se, order = keyed >> id_bits, keyed & ((1 << id_bits) - 1)
    tok = order % t
    ex = lax.iota(I32, n_experts)
    starts = jnp.sum((se[None, :] < ex[:, None]).astype(I32), axis=1)
    seb = se.reshape(nb, bm)
    e_lo, e_hi = seb[:, 0], seb[:, bm - 1]
    npair = e_hi - e_lo + 1
    cum = jnp.cumsum(npair)
    off = cum - npair
    p = lax.iota(I32, nb + n_experts)
    pb = jnp.minimum(jnp.sum((cum[None, :] <= p[:, None]).astype(I32), axis=1), nb - 1)
    pe = jnp.minimum(e_lo[pb] + p - off[pb], e_hi[pb])
    plo = jnp.clip(starts[pe] - pb * bm, 0, bm)
    valid = p < cum[-1]
    new_e = jnp.logical_and(valid, jnp.concatenate([jnp.ones((1,), bool), pe[1:] != pe[:-1]]))
    wslot = (jnp.cumsum(new_e.astype(I32)) - 1) % 2
    later = jnp.logical_and(pe[None, :] > pe[:, None], valid[None, :])
    pne = jnp.min(jnp.where(later, pe[None, :], n_experts), axis=1)
    pfl = (PAIR_FIRST_OF_BLOCK * (p == off[pb]) + PAIR_NEW_EXPERT * new_e
           + PAIR_HAS_NEXT_EXPERT * (pne < n_experts) + PAIR_WEIGHT_SLOT * wslot).astype(I32)
    pne = jnp.minimum(pne, n_experts - 1)
    pairs = (pb.astype(I32), pe.astype(I32), plo.astype(I32), pfl, pne.astype(I32), cum[-1:].astype(I32))
    return pairs, tok.reshape(nb, 1, bm), order.reshape(nb, 1, bm)


def _combine_body(x1_ref, h_ref, w_ref, g2_ref, wsg_ref, wsu_ref, wsd_ref, lg_ref, lb_ref, *rest, bb, r):
    y_refs, o_ref = rest[:TOP_K], rest[TOP_K]
    c = h_ref.shape[-1]
    h_lo, h_hi = _unpack_pairs(h_ref[...])
    h_lo, h_hi = h_lo.astype(BF16), h_hi.astype(BF16)

    def proj(w_ref):
        return (jnp.dot(h_lo, w_ref[:c, :], preferred_element_type=F32)
                + jnp.dot(h_hi, w_ref[c:, :], preferred_element_type=F32))

    g = proj(wsg_ref)
    u = proj(wsu_ref)
    shared = jnp.dot((g * _sigmoid(g) * u).astype(BF16), wsd_ref[...], preferred_element_type=F32)
    w = w_ref[...]
    r_lo = r_hi = None
    for k in range(TOP_K):
        y_lo, y_hi = _unpack_pairs(y_refs[k][...])
        wk = w[:, k:k + 1]
        r_lo = wk * y_lo if r_lo is None else r_lo + wk * y_lo
        r_hi = wk * y_hi if r_hi is None else r_hi + wk * y_hi
    routed = jnp.concatenate([r_lo, r_hi], axis=-1)
    res = DN_ALPHA * x1_ref[...] + (1.0 + _rows(g2_ref, bb, r)) * (routed + shared)
    o_ref[...] = _ln(res) * lg_ref[...] + lb_ref[...]


def _combine(x1, h_pack, y_pack, wsel, g2, wsg_bf, wsu_bf, wsd_bf, ln_g, ln_b, nb, seq, row_off):
    t, d = x1.shape
    t_all, c = h_pack.shape
    ds_ = wsg_bf.shape[1]
    rows = COMB_ROWS
    if seq >= rows:
        bb, r, nl = 1, rows, seq // rows
        grid = (nb, nl)
        row_map = lambda b, l: (b * nl + l, 0)
    else:
        bb, r, nl = rows // seq, seq, 1
        grid = (nb // bb, 1)
        row_map = lambda b, l: (b, 0)
    seq_map = lambda b, l: (b, 0, 0)
    off = row_off // rows
    all_map = lambda b, l: (row_map(b, l)[0] + off, 0)
    const2 = lambda b, l: (0, 0)

    def slot_map(k):
        return lambda b, l: (row_map(b, l)[0] + off + k * (t_all // rows), 0)

    return pl.pallas_call(
        functools.partial(_combine_body, bb=bb, r=r),
        grid=grid,
        in_specs=[pl.BlockSpec((rows, d), row_map),
                  pl.BlockSpec((rows, c), all_map),
                  pl.BlockSpec((rows, TOP_K), all_map),
                  pl.BlockSpec((bb, 1, d), seq_map),
                  pl.BlockSpec((d, ds_), const2),
                  pl.BlockSpec((d, ds_), const2),
                  pl.BlockSpec((ds_, d), const2),
                  pl.BlockSpec((1, d), const2),
                  pl.BlockSpec((1, d), const2)]
                 + [pl.BlockSpec((rows, c), slot_map(k)) for k in range(TOP_K)],
        out_specs=pl.BlockSpec((rows, d), row_map),
        out_shape=jax.ShapeDtypeStruct((t, d), F32),
        compiler_params=_cparams("arbitrary", "arbitrary"),
        name="combine",
    )(x1, h_pack, wsel, g2, wsg_bf, wsu_bf, wsd_bf, ln_g.reshape(1, d), ln_b.reshape(1, d),
      *([y_pack] * TOP_K))


def kernel(x_prompt, x_sample, state_pool, state_ssm_re, state_ssm_im, c_prompt, c_sample, w_ada, b_ada, w_in, w_pool, pool_scale, A_re, A_im, log_dt, B_re, B_im, C_re, C_im, D_skip, w_glu, b_glu, g_pool, g_ssm, w_out, ln1_g, ln1_b, w_router, router_bias, w_e_gate, w_e_up, w_e_down, w_sh_gate, w_sh_up, w_sh_down, ln2_g, ln2_b):
    bp, lp, d = x_prompt.shape
    bs, ls, _ = x_sample.shape
    depth = w_ada.shape[0]
    assert depth == DEPTH == 1
    tp, ts = bp * lp, bs * ls
    t_all = tp + ts
    l = 0
    c_pool = pool_scale.shape[-1]
    n_groups, n_state = A_re.shape[1], A_re.shape[2]
    nstate = n_groups * n_state
    n_experts = w_router.shape[-1]
    gpt = SCAN_W // n_state

    c_all = jnp.concatenate([c_prompt, c_sample], axis=0)
    pad = (-c_all.shape[0]) % SUBLANES
    c_all = jnp.pad(c_all, ((0, pad), (0, 0)))
    mod = _adaln(c_all, w_ada[l], b_ada[l]).reshape(c_all.shape[0], 6, 1, d)
    mod_p = [mod[:bp, i] for i in range(6)]
    mod_s = [mod[bp:bp + bs, i] for i in range(6)]

    w_in_bf = w_in[l].astype(BF16)
    w_pool_bf = w_pool[l].astype(BF16)
    w_glu_bf = w_glu[l].astype(BF16)
    w_out_bf = w_out[l].astype(BF16)
    w_router_t_bf = w_router[l].T.astype(BF16)
    wsg_bf, wsu_bf, wsd_bf = w_sh_gate[l].astype(BF16), w_sh_up[l].astype(BF16), w_sh_down[l].astype(BF16)
    pw_r, pw_i, bb_r, bb_i = _ssm_prep(A_re[l], A_im[l], log_dt[l], B_re[l], B_im[l], ROWS // SUBLANES)
    bbr_bd = _block_diag(jnp.swapaxes(bb_r, 1, 2), gpt).astype(BF16)
    bbi_bd = _block_diag(jnp.swapaxes(bb_i, 1, 2), gpt).astype(BF16)
    cr_bd = _block_diag(jnp.swapaxes(C_re[l], 1, 2), gpt).astype(BF16)
    nci_bd = _block_diag(jnp.swapaxes(-C_im[l], 1, 2), gpt).astype(BF16)
    d_skip = D_skip[l].reshape(-1)

    groups = [
        dict(x=x_prompt.reshape(tp, d), nb=bp, seq=lp, mod=mod_p, start=0, row_off=0,
             prefix=jnp.zeros((bp, POOL_HALO, c_pool), F32),
             h0r=jnp.zeros((bp, 1, nstate), F32), h0i=jnp.zeros((bp, 1, nstate), F32)),
        dict(x=x_sample.reshape(ts, d), nb=bs, seq=ls, mod=mod_s, start=PAST_LEN, row_off=tp,
             prefix=jnp.pad(state_pool[l], ((0, 0), (POOL_HALO - state_pool.shape[2], 0), (0, 0))),
             h0r=state_ssm_re[l].reshape(bs, nstate), h0i=state_ssm_im[l].reshape(bs, nstate)),
    ]

    h_all = None
    for gr in groups:
        sh1, sc1, g1, sh2, sc2, g2 = gr['mod']
        nb, seq = gr['nb'], gr['seq']
        perm = _scan_perm(ROWS // SUBLANES if seq >= ROWS else seq)
        z = _mix_in(gr['x'], sh1, sc1, w_in_bf, perm, nb, seq)
        ya = _pool(z, gr['prefix'], w_pool_bf, pool_scale[l], g_pool[l], nb, seq, gr['start'])
        ys, hfr, hfi = _ssm(z, gr['h0r'], gr['h0i'], bbr_bd, bbi_bd, cr_bd, nci_bd, d_skip, pw_r, pw_i,
                            w_glu_bf, b_glu[l], g_ssm[l], perm.T, nb, seq)
        x1, h_all, eidx_t, ew_t = _out_proj(gr['x'], ya, ys, g1, sh2, sc2, w_out_bf, ln1_g[l], ln1_b[l],
                                            w_router_t_bf, router_bias[l], nb, seq,
                                            h_all, t_all, gr['row_off'])
        gr.update(z=z, x1=x1, eidx_t=eidx_t, ew_t=ew_t, hfr=hfr, hfi=hfi)

    eidx_t = jnp.concatenate([gr['eidx_t'] for gr in groups], axis=1)
    wsel = jnp.concatenate([gr['ew_t'] for gr in groups], axis=1).T
    pairs, tok_blocks, dst_blocks = _dispatch(eidx_t, n_experts)
    y3 = _experts(h_all, w_e_gate[l], w_e_up[l], w_e_down[l], pairs, tok_blocks, dst_blocks)

    outs = []
    for gr in groups:
        outs.append(_combine(gr['x1'], h_all, y3, wsel, gr['mod'][5], wsg_bf, wsu_bf, wsd_bf,
                             ln2_g[l], ln2_b[l], gr['nb'], gr['seq'], gr['row_off']))
    y_prompt = outs[0].reshape(bp, lp, d)
    y_sample = outs[1].reshape(bs, ls, d)

    nbuf = state_pool.shape[2]
    zp = groups[0]['z'].reshape(bp, lp, d)[:, :, :c_pool]
    zs = groups[1]['z'].reshape(bs, ls, d)[:, :, :c_pool]
    pool_p = zp[:, lp - nbuf:, :][None]
    pool_s = jnp.concatenate([state_pool[l], zs], axis=1)[:, -nbuf:, :][None]
    st = lambda a, nb: a.reshape(nb, n_groups, n_state)[None]
    return (y_prompt, y_sample, pool_p, pool_s,
            st(groups[0]['hfr'], bp), st(groups[0]['hfi'], bp),
            st(groups[1]['hfr'], bs), st(groups[1]['hfi'], bs))
```

```python
import functools

import jax
import jax.numpy as jnp
from jax import lax
from jax.experimental import pallas as pl
from jax.experimental.pallas import tpu as pltpu

F32 = jnp.float32
BF16 = jnp.bfloat16
I32 = jnp.int32
U32 = jnp.uint32

DEPTH = 1
PAST_LEN = 16384
POOL_WINDOWS = (2, 4, 8, 16)
POOL_HALO = 16
SSM_P = 16
SSM_N = 64
N_EXPERT_GROUPS = 8
TOPK_GROUPS = 4
TOP_K = 8
ROUTED_SCALE = 2.5
LN_EPS = 1e-5
DN_ALPHA = (2.0 * DEPTH) ** 0.25

ROWS = 256
SUBLANES = 8
LANES = 128
SCAN_W = 512
MOE_BM = 256
PAIR_FIRST_OF_BLOCK, PAIR_NEW_EXPERT, PAIR_HAS_NEXT_EXPERT, PAIR_WEIGHT_SLOT, PAIR_SCATTERS = 1, 2, 4, 8, 16
COMB_ROWS = 256
VMEM_LIMIT = 56 * 1024 * 1024


def _cparams(*sem):
    return pltpu.CompilerParams(dimension_semantics=sem, vmem_limit_bytes=VMEM_LIMIT)


def _ln(x):
    xc = x - jnp.mean(x, axis=-1, keepdims=True)
    return xc * lax.rsqrt(jnp.mean(xc * xc, axis=-1, keepdims=True) + LN_EPS)


def _rows(m_ref, bb, r):
    m = m_ref[...]
    c = m.shape[-1]
    return jnp.broadcast_to(m, (bb, r, c)).reshape(bb * r, c)


def _sigmoid(x):
    return 1.0 / (1.0 + jnp.exp(-x))


def _bf16_bits(x):
    return lax.bitcast_convert_type(x.astype(BF16).astype(F32), U32)


def _pack_pairs(x):
    c = x.shape[-1] // 2
    return (_bf16_bits(x[:, :c]) >> 16) | _bf16_bits(x[:, c:])


def _unpack_pairs(w):
    return (lax.bitcast_convert_type(w << 16, F32),
            lax.bitcast_convert_type(w & jnp.uint32(0xFFFF0000), F32))


def _adaln_body(c_ref, w_ref, b_ref, o_ref):
    c = c_ref[...]
    s = (c * _sigmoid(c)).astype(BF16)
    o_ref[...] = jnp.dot(s, w_ref[...].astype(BF16), preferred_element_type=F32) + b_ref[...]


def _adaln(c_all, w_ada, b_ada):
    bc, d = c_all.shape
    n = w_ada.shape[1]
    tn = 1024
    return pl.pallas_call(
        _adaln_body,
        grid=(n // tn,),
        in_specs=[pl.BlockSpec((bc, d), lambda j: (0, 0)),
                  pl.BlockSpec((d, tn), lambda j: (0, j)),
                  pl.BlockSpec((1, tn), lambda j: (0, j))],
        out_specs=pl.BlockSpec((bc, tn), lambda j: (0, j)),
        out_shape=jax.ShapeDtypeStruct((bc, n), F32),
        compiler_params=_cparams("arbitrary"),
        name="adaln",
    )(c_all, w_ada, b_ada.reshape(1, n))


def _ssm_prep_body(ar_ref, ai_ref, dt_ref, ar16_ref, ai16_ref, dt16_ref, br_ref, bi_ref,
                   pr_ref, pi_ref, bbr_ref, bbi_ref):
    def zoh(a_r, a_i, dt):
        mag = jnp.exp(a_r * dt)
        ab_r, ab_i = mag * jnp.cos(a_i * dt), mag * jnp.sin(a_i * dt)
        den = a_r * a_r + a_i * a_i
        nr = ab_r - 1.0
        return ab_r, ab_i, (nr * a_r + ab_i * a_i) / den, (ab_i * a_r - nr * a_i) / den

    ab_r, ab_i, _, _ = zoh(ar_ref[...], ai_ref[...], jnp.exp(dt_ref[...]))
    p_r, p_i = ab_r, ab_i
    for k in range(pr_ref.shape[0]):
        pr_ref[k] = p_r
        pi_ref[k] = p_i
        p_r, p_i = p_r * ab_r - p_i * ab_i, p_r * ab_i + p_i * ab_r
    _, _, f_r, f_i = zoh(ar16_ref[...], ai16_ref[...], jnp.exp(dt16_ref[...]))
    b_r, b_i = br_ref[...], bi_ref[...]
    bbr_ref[...] = f_r * b_r - f_i * b_i
    bbi_ref[...] = f_r * b_i + f_i * b_r


def _ssm_prep(a_re, a_im, log_dt, b_re, b_im, npow):
    g, n = a_re.shape
    p = b_re.shape[-1]
    dt = jnp.broadcast_to(log_dt[:, None], (g, n))
    rep = lambda a: jnp.repeat(a, p, axis=-1)
    outs = pl.pallas_call(
        _ssm_prep_body,
        out_shape=(jax.ShapeDtypeStruct((npow, g, n), F32), jax.ShapeDtypeStruct((npow, g, n), F32),
                   jax.ShapeDtypeStruct((g, n * p), F32), jax.ShapeDtypeStruct((g, n * p), F32)),
        name="ssm_prep",
    )(a_re, a_im, dt, rep(a_re), rep(a_im), rep(dt), b_re.reshape(g, n * p), b_im.reshape(g, n * p))
    pw_r, pw_i, bb_r, bb_i = outs
    return (pw_r.reshape(npow, g * n), pw_i.reshape(npow, g * n),
            bb_r.reshape(g, n, p), bb_i.reshape(g, n, p))


def _block_diag(w, gpt):
    g, a, b = w.shape
    w4 = w.reshape(g // gpt, gpt, a, b)
    eye = jnp.eye(gpt, dtype=w.dtype)
    return jnp.einsum('jgab,gh->jgahb', w4, eye).reshape(g // gpt, gpt * a, gpt * b)


def _scan_perm(seglen):
    new = jnp.arange(ROWS)
    grp, rem = new // (SUBLANES * seglen), new % (SUBLANES * seglen)
    old = grp * (SUBLANES * seglen) + (rem % SUBLANES) * seglen + rem // SUBLANES
    return (old[:, None] == jnp.arange(ROWS)[None, :]).astype(BF16)


def _mix_in_body(x_ref, sh_ref, sc_ref, w_ref, perm_ref, z_ref, *, bb, r):
    c = z_ref.shape[-1] // 2
    u = (_ln(x_ref[...]) * (1.0 + _rows(sc_ref, bb, r)) + _rows(sh_ref, bb, r)).astype(BF16)
    z_ref[:, :c] = jnp.dot(u, w_ref[:, :c], preferred_element_type=F32)
    up = jnp.dot(perm_ref[...], u, preferred_element_type=F32).astype(BF16)
    z_ref[:, c:] = jnp.dot(up, w_ref[:, c:], preferred_element_type=F32)


def _seq_grid(nb, seq):
    if seq >= ROWS:
        bb, r, nl = 1, ROWS, seq // ROWS
        grid = (nb, nl)
        row_map = lambda b, l: (b * nl + l, 0)
    else:
        bb, r, nl = ROWS // seq, seq, 1
        grid = (nb // bb, 1)
        row_map = lambda b, l: (b, 0)
    seq_map = lambda b, l: (b, 0, 0)
    return bb, r, grid, row_map, seq_map


def _mix_in(x2, sh, sc, w_in_bf, perm, nb, seq):
    t, d = x2.shape
    bb, r, grid, row_map, seq_map = _seq_grid(nb, seq)
    const2 = lambda b, l: (0, 0)
    return pl.pallas_call(
        functools.partial(_mix_in_body, bb=bb, r=r),
        grid=grid,
        in_specs=[pl.BlockSpec((ROWS, d), row_map),
                  pl.BlockSpec((bb, 1, d), seq_map),
                  pl.BlockSpec((bb, 1, d), seq_map),
                  pl.BlockSpec((d, d), const2),
                  pl.BlockSpec((ROWS, ROWS), const2)],
        out_specs=pl.BlockSpec((ROWS, d), row_map),
        out_shape=jax.ShapeDtypeStruct((t, d), F32),
        compiler_params=_cparams("arbitrary", "arbitrary"),
        name="mix_in",
    )(x2, sh, sc, w_in_bf, perm)


def _pool_body(z_ref, pre_ref, wp_ref, ps_ref, gp_ref, o_ref, carry_ref, *, bb, r, start_pos):
    li = pl.program_id(1)
    c = z_ref.shape[-1]
    gw = c // len(POOL_WINDOWS)
    rp = POOL_HALO + r

    @pl.when(li == 0)
    def _():
        carry_ref[...] = pre_ref[...]

    za = z_ref[...].reshape(bb, r, c)
    xp3 = jnp.concatenate([carry_ref[...], za], axis=1)
    carry_ref[...] = xp3[:, r:, :]
    xp = xp3.reshape(bb * rp, c)
    pos1 = lax.broadcasted_iota(I32, (bb, r, gw), 1) + (start_pos + 1) + li * r
    outs = []
    ssq = jnp.zeros((bb * r, 1), F32)
    for gi, w in enumerate(POOL_WINDOWS):
        cols = slice(gi * gw, (gi + 1) * gw)
        s = xp[:, cols]
        sh = 1
        while sh < w:
            s = s + pltpu.roll(s, sh, 0)
            sh *= 2
        win = s.reshape(bb, rp, gw)[:, POOL_HALO:, :]
        cnt = jnp.minimum(pos1, w).astype(F32)
        d = (win / cnt - za[:, :, cols]).reshape(bb * r, gw)
        y = jnp.dot(d.astype(BF16), wp_ref[gi], preferred_element_type=F32) * ps_ref[:, cols]
        ssq = ssq + jnp.sum(y * y, axis=-1, keepdims=True)
        outs.append(y)
    scale = lax.rsqrt(ssq * (1.0 / c) + LN_EPS)
    for gi, y in enumerate(outs):
        cols = slice(gi * gw, (gi + 1) * gw)
        o_ref[:, cols] = (y * scale * gp_ref[:, cols]).astype(o_ref.dtype)


def _pool(z, prefix16, w_pool_bf, pool_scale, g_pool, nb, seq, start_pos):
    t = z.shape[0]
    c = pool_scale.shape[-1]
    bb, r, grid, row_map, seq_map = _seq_grid(nb, seq)
    const2 = lambda b, l: (0, 0)
    return pl.pallas_call(
        functools.partial(_pool_body, bb=bb, r=r, start_pos=start_pos),
        grid=grid,
        in_specs=[pl.BlockSpec((ROWS, c), row_map),
                  pl.BlockSpec((bb, POOL_HALO, c), seq_map),
                  pl.BlockSpec(w_pool_bf.shape, lambda b, l: (0, 0, 0)),
                  pl.BlockSpec((1, c), const2),
                  pl.BlockSpec((1, c), const2)],
        out_specs=pl.BlockSpec((ROWS, c), row_map),
        out_shape=jax.ShapeDtypeStruct((t, c), BF16),
        scratch_shapes=[pltpu.VMEM((bb, POOL_HALO, c), F32)],
        compiler_params=_cparams("arbitrary", "arbitrary"),
        name="pool",
    )(z, prefix16, w_pool_bf, pool_scale.reshape(1, c), g_pool.reshape(1, c))


def _cmul_add(x_r, x_i, m_r, m_i, y_r, y_i):
    return x_r + m_r * y_r - m_i * y_i, x_i + m_r * y_i + m_i * y_r


def _ssm_body(z_ref, h0r_ref, h0i_ref, bbr_ref, bbi_ref, cr_ref, nci_ref, dsk_ref, pr_ref, pi_ref,
              wg_ref, bg_ref, gs_ref, unperm_ref, o_ref, hfr_ref, hfi_ref, hr_s, hi_s, y_s, car_s, cai_s,
              *, seglen, chained):
    li = pl.program_id(1)
    c = z_ref.shape[-1]
    nstate = hr_s.shape[-1]
    ntile = nstate // SCAN_W
    cw = c // ntile
    u = z_ref[...]
    ub = u.astype(BF16)
    for j in range(ntile):
        sl = slice(j * SCAN_W, (j + 1) * SCAN_W)
        uj = ub[:, j * cw:(j + 1) * cw]
        hr_s[:, sl] = jnp.dot(uj, bbr_ref[j], preferred_element_type=F32)
        hi_s[:, sl] = jnp.dot(uj, bbi_ref[j], preferred_element_type=F32)

    if chained:
        @pl.when(li == 0)
        def _():
            car_s[...] = h0r_ref[0]
            cai_s[...] = h0i_ref[0]

    grp_rows = SUBLANES * seglen
    bc8 = lambda v: jnp.broadcast_to(v, (SUBLANES, SCAN_W))
    for j in range(ntile):
        sl = slice(j * SCAN_W, (j + 1) * SCAN_W)
        a_r, a_i = bc8(pr_ref[0:1, sl]), bc8(pi_ref[0:1, sl])

        def step(t, h, base, sl=sl, a_r=a_r, a_i=a_i):
            off = pl.multiple_of(base + t * SUBLANES, SUBLANES)
            h_r, h_i = _cmul_add(hr_s[pl.ds(off, SUBLANES), sl], hi_s[pl.ds(off, SUBLANES), sl],
                                 a_r, a_i, h[0], h[1])
            hr_s[pl.ds(off, SUBLANES), sl] = h_r
            hi_s[pl.ds(off, SUBLANES), sl] = h_i
            return h_r, h_i

        if chained:
            zero = jnp.zeros((SUBLANES, SCAN_W), F32)
            e_r, e_i = lax.fori_loop(0, seglen, functools.partial(step, base=0), (zero, zero), unroll=4)
            al_r, al_i = pr_ref[seglen - 1:seglen, sl], pi_ref[seglen - 1:seglen, sl]
            s_r, s_i = car_s[:, sl], cai_s[:, sl]
            ent_r, ent_i = [s_r], [s_i]
            for i in range(SUBLANES):
                s_r, s_i = _cmul_add(e_r[i:i + 1], e_i[i:i + 1], al_r, al_i, s_r, s_i)
                if i + 1 < SUBLANES:
                    ent_r.append(s_r)
                    ent_i.append(s_i)
            car_s[:, sl] = s_r
            cai_s[:, sl] = s_i
            hfr_ref[0, :, sl] = s_r
            hfi_ref[0, :, sl] = s_i
            ent_r, ent_i = jnp.concatenate(ent_r, axis=0), jnp.concatenate(ent_i, axis=0)

            def fix(t, _, sl=sl, ent_r=ent_r, ent_i=ent_i):
                off = pl.multiple_of(t * SUBLANES, SUBLANES)
                x_r, x_i = _cmul_add(hr_s[pl.ds(off, SUBLANES), sl], hi_s[pl.ds(off, SUBLANES), sl],
                                     bc8(pr_ref[pl.ds(t, 1), sl]), bc8(pi_ref[pl.ds(t, 1), sl]), ent_r, ent_i)
                hr_s[pl.ds(off, SUBLANES), sl] = x_r
                hi_s[pl.ds(off, SUBLANES), sl] = x_i
                return 0

            lax.fori_loop(0, seglen, fix, 0, unroll=4)
        else:
            for g in range(z_ref.shape[0] // grp_rows):
                rs = slice(g * SUBLANES, (g + 1) * SUBLANES)
                e_r, e_i = lax.fori_loop(0, seglen, functools.partial(step, base=g * grp_rows),
                                         (h0r_ref[rs, sl], h0i_ref[rs, sl]), unroll=True)
                hfr_ref[rs, sl] = e_r
                hfi_ref[rs, sl] = e_i

    for j in range(ntile):
        sl = slice(j * SCAN_W, (j + 1) * SCAN_W)
        cs = slice(j * cw, (j + 1) * cw)
        y_s[:, cs] = (jnp.dot(hr_s[:, sl].astype(BF16), cr_ref[j], preferred_element_type=F32)
                      + jnp.dot(hi_s[:, sl].astype(BF16), nci_ref[j], preferred_element_type=F32)
                      + dsk_ref[:, cs] * u[:, cs])
    y = y_s[...]
    g = 0.5 * y * (1.0 + jnp.tanh(0.7978845608028654 * (y + 0.044715 * (y * y * y))))
    gate = jnp.dot(g.astype(BF16), wg_ref[...], preferred_element_type=F32) + bg_ref[...]
    out = g * _sigmoid(gate)
    scale = lax.rsqrt(jnp.mean(out * out, axis=-1, keepdims=True) + LN_EPS)
    outp = (out * scale * gs_ref[...]).astype(BF16)
    o_ref[...] = jnp.dot(unperm_ref[...], outp, preferred_element_type=F32).astype(o_ref.dtype)


def _ssm(z, h0r, h0i, bbr_bd, bbi_bd, cr_bd, nci_bd, d_skip, pw_r, pw_i, w_glu_bf, b_glu, g_ssm, unperm,
         nb, seq):
    t = z.shape[0]
    c = d_skip.shape[-1]
    nstate = pw_r.shape[-1]
    bb, r, grid, row_map, seq_map = _seq_grid(nb, seq)
    chained = bb == 1
    seglen = r // SUBLANES if chained else r
    const2 = lambda b, l: (0, 0)
    const3 = lambda b, l: (0, 0, 0)
    full = lambda a: pl.BlockSpec(a.shape, const3 if a.ndim == 3 else const2)
    right_half = (lambda b, l: (row_map(b, l)[0], 1))
    if chained:
        st_spec = pl.BlockSpec((1, 1, nstate), seq_map)
        st_shape = jax.ShapeDtypeStruct((nb, 1, nstate), F32)
    else:
        st_spec = pl.BlockSpec((bb, nstate), lambda b, l: (b, 0))
        st_shape = jax.ShapeDtypeStruct((nb, nstate), F32)
    return pl.pallas_call(
        functools.partial(_ssm_body, seglen=seglen, chained=chained),
        grid=grid,
        in_specs=[pl.BlockSpec((ROWS, c), right_half),
                  st_spec, st_spec,
                  full(bbr_bd), full(bbi_bd), full(cr_bd), full(nci_bd),
                  pl.BlockSpec((1, c), const2),
                  full(pw_r), full(pw_i),
                  full(w_glu_bf),
                  pl.BlockSpec((1, c), const2),
                  pl.BlockSpec((1, c), const2),
                  pl.BlockSpec((ROWS, ROWS), const2)],
        out_specs=[pl.BlockSpec((ROWS, c), row_map), st_spec, st_spec],
        out_shape=[jax.ShapeDtypeStruct((t, c), BF16), st_shape, st_shape],
        scratch_shapes=[pltpu.VMEM((ROWS, nstate), F32), pltpu.VMEM((ROWS, nstate), F32),
                        pltpu.VMEM((ROWS, c), F32),
                        pltpu.VMEM((1, nstate), F32), pltpu.VMEM((1, nstate), F32)],
        compiler_params=_cparams("arbitrary", "arbitrary"),
        name="ssm",
    )(z, h0r, h0i, bbr_bd, bbi_bd, cr_bd, nci_bd, d_skip.reshape(1, c), pw_r, pw_i,
      w_glu_bf, b_glu.reshape(1, c), g_ssm.reshape(1, c), unperm)


def _route(s_t, bias_t):
    e, tn = s_t.shape
    per = e // N_EXPERT_GROUPS
    neg = -jnp.inf
    sb = s_t + bias_t
    rowl = lax.broadcasted_iota(I32, (per, tn), 0)
    gscore = []
    for g in range(N_EXPERT_GROUPS):
        blk = sb[g * per:(g + 1) * per]
        m1 = jnp.max(blk, axis=0, keepdims=True)
        i1 = jnp.min(jnp.where(blk == m1, rowl, per), axis=0, keepdims=True)
        m2 = jnp.max(jnp.where(rowl == i1, neg, blk), axis=0, keepdims=True)
        gscore.append(m1 + m2)
    cur = jnp.concatenate(gscore, axis=0)
    rowg = lax.broadcasted_iota(I32, cur.shape, 0)
    gsel = jnp.zeros(cur.shape, F32)
    for _ in range(TOPK_GROUPS):
        m = jnp.max(cur, axis=0, keepdims=True)
        hit = rowg == jnp.min(jnp.where(cur == m, rowg, N_EXPERT_GROUPS), axis=0, keepdims=True)
        gsel = jnp.where(hit, 1.0, gsel)
        cur = jnp.where(hit, neg, cur)
    cur = jnp.concatenate(
        [jnp.where(jnp.broadcast_to(gsel[g:g + 1], (per, tn)) > 0.0, sb[g * per:(g + 1) * per], neg)
         for g in range(N_EXPERT_GROUPS)], axis=0)
    rowe = lax.broadcasted_iota(I32, (e, tn), 0)
    idxs, vals = [], []
    for _ in range(TOP_K):
        m = jnp.max(cur, axis=0, keepdims=True)
        idx = jnp.min(jnp.where(cur == m, rowe, e), axis=0, keepdims=True)
        hit = rowe == idx
        idxs.append(idx)
        vals.append(jnp.sum(jnp.where(hit, s_t, 0.0), axis=0, keepdims=True))
        cur = jnp.where(hit, neg, cur)
    w = jnp.concatenate(vals, axis=0)
    w = w / jnp.sum(w, axis=0, keepdims=True) * ROUTED_SCALE
    return jnp.concatenate(idxs, axis=0), w


def _out_proj_body(x_ref, ya_ref, ys_ref, g1_ref, sh2_ref, sc2_ref, wo_ref, lg_ref, lb_ref, wrt_ref, rb_ref,
                   x1_ref, h_ref, ei_ref, ew_ref, *, bb, r):
    ca = ya_ref.shape[-1]
    m = (jnp.dot(ya_ref[...], wo_ref[:ca, :], preferred_element_type=F32)
         + jnp.dot(ys_ref[...], wo_ref[ca:, :], preferred_element_type=F32))
    res = DN_ALPHA * x_ref[...] + (1.0 + _rows(g1_ref, bb, r)) * m
    x1 = _ln(res) * lg_ref[...] + lb_ref[...]
    x1_ref[...] = x1
    h = _ln(x1) * (1.0 + _rows(sc2_ref, bb, r)) + _rows(sh2_ref, bb, r)
    h_ref[...] = _pack_pairs(h)
    logit_t = lax.dot_general(wrt_ref[...], h.astype(BF16), (((1,), (1,)), ((), ())),
                              preferred_element_type=F32)
    idx, w = _route(_sigmoid(logit_t), rb_ref[...])
    ei_ref[...] = idx
    ew_ref[...] = w


def _out_proj(x2, ya, ys, g1, sh2, sc2, w_out_bf, ln_g, ln_b, w_router_t_bf, router_bias, nb, seq,
              h_all, t_all, row_off):
    t, d = x2.shape
    ca = ya.shape[-1]
    e = w_router_t_bf.shape[0]
    bb, r, grid, row_map, seq_map = _seq_grid(nb, seq)
    const2 = lambda b, l: (0, 0)
    tok_map = lambda b, l: (0, row_map(b, l)[0])
    blk_off = row_off // ROWS
    h_map = lambda b, l: (row_map(b, l)[0] + blk_off, 0)
    in_specs = [pl.BlockSpec((ROWS, d), row_map),
                pl.BlockSpec((ROWS, ca), row_map),
                pl.BlockSpec((ROWS, ca), row_map),
                pl.BlockSpec((bb, 1, d), seq_map),
                pl.BlockSpec((bb, 1, d), seq_map),
                pl.BlockSpec((bb, 1, d), seq_map),
                pl.BlockSpec((d, d), const2),
                pl.BlockSpec((1, d), const2),
                pl.BlockSpec((1, d), const2),
                pl.BlockSpec((e, d), const2),
                pl.BlockSpec((e, 1), const2)]
    args = [x2, ya, ys, g1, sh2, sc2, w_out_bf, ln_g.reshape(1, d), ln_b.reshape(1, d),
            w_router_t_bf, router_bias.reshape(e, 1)]
    body = functools.partial(_out_proj_body, bb=bb, r=r)
    aliases = {}
    if h_all is not None:
        in_specs.append(pl.BlockSpec(memory_space=pl.ANY))
        args.append(h_all)
        aliases = {len(args) - 1: 1}
        inner = body
        body = lambda *refs: inner(*refs[:11], *refs[12:])
    return pl.pallas_call(
        body,
        grid=grid,
        in_specs=in_specs,
        out_specs=[pl.BlockSpec((ROWS, d), row_map),
                   pl.BlockSpec((ROWS, d // 2), h_map),
                   pl.BlockSpec((TOP_K, ROWS), tok_map),
                   pl.BlockSpec((TOP_K, ROWS), tok_map)],
        out_shape=[jax.ShapeDtypeStruct((t, d), F32),
                   jax.ShapeDtypeStruct((t_all, d // 2), U32),
                   jax.ShapeDtypeStruct((TOP_K, t), I32),
                   jax.ShapeDtypeStruct((TOP_K, t), F32)],
        input_output_aliases=aliases,
        compiler_params=_cparams("arbitrary", "arbitrary"),
        name="out_proj",
    )(*args)


def _experts_body(pb_ref, pe_ref, plo_ref, pfl_ref, pne_ref, np_ref, tokc_ref, tokn_ref, dstp_ref, dstc_ref,
                  h_hbm, wg_hbm, wu_hbm, wd_hbm, y_hbm, xbuf, ybuf, xlo_s, xhi_s, wg_s, wu_s, wd_s,
                  gsem, ssem, wsem):
    p = pl.program_id(0)
    n_pairs = np_ref[0]
    valid = p < n_pairs
    b = pb_ref[p]
    slot = b % 2
    flags = pfl_ref[p]
    first = (flags & PAIR_FIRST_OF_BLOCK) != 0
    scatters = (flags & PAIR_SCATTERS) != 0
    new_expert = (flags & PAIR_NEW_EXPERT) != 0
    has_next = (flags & PAIR_HAS_NEXT_EXPERT) != 0
    wslot = (flags & PAIR_WEIGHT_SLOT) // PAIR_WEIGHT_SLOT
    bm, c = xbuf.shape[1], xbuf.shape[2]
    weights = ((wg_hbm, wg_s), (wu_hbm, wu_s), (wd_hbm, wd_s))

    def weights_start(e, s):
        for w_hbm, w_s in weights:
            pltpu.make_async_copy(w_hbm.at[e], w_s.at[s], wsem.at[s]).start(priority=1)

    def weights_wait(s):
        for w_hbm, w_s in weights:
            pltpu.make_async_copy(w_hbm.at[0], w_s.at[s], wsem.at[s]).wait()

    def gather_start(tok_ref, s):
        for i in range(bm):
            pltpu.make_async_copy(h_hbm.at[pl.ds(tok_ref[0, 0, i], 1), :],
                                  xbuf.at[s, pl.ds(i, 1), :], gsem.at[s]).start()

    def gather_wait(s):
        pltpu.make_async_copy(h_hbm.at[pl.ds(0, bm), :], xbuf.at[s], gsem.at[s]).wait()

    def scatter_start(dst_ref, s):
        for i in range(bm):
            pltpu.make_async_copy(ybuf.at[s, pl.ds(i, 1), :],
                                  y_hbm.at[pl.ds(dst_ref[0, 0, i], 1), :], ssem.at[0]).start(priority=i % 2)

    def scatter_wait():
        pltpu.make_async_copy(ybuf.at[0], y_hbm.at[pl.ds(0, bm), :], ssem.at[0]).wait()

    def expert():
        def proj(w_s):
            return (jnp.dot(xlo_s[...], w_s[wslot, :c, :].astype(BF16), preferred_element_type=F32)
                    + jnp.dot(xhi_s[...], w_s[wslot, c:, :].astype(BF16), preferred_element_type=F32))

        g = proj(wg_s)
        u = proj(wu_s)
        a = (g * _sigmoid(g) * u).astype(BF16)
        return _pack_pairs(jnp.dot(a, wd_s[wslot].astype(BF16), preferred_element_type=F32))

    @pl.when(p == 0)
    def _():
        weights_start(pe_ref[0], 0)
        gather_start(tokc_ref, 0)
        ybuf[1] = jnp.zeros(ybuf.shape[1:], ybuf.dtype)

    @pl.when(jnp.logical_and(valid, new_expert))
    def _():
        weights_wait(wslot)

        @pl.when(has_next)
        def _():
            weights_start(pne_ref[p], 1 - wslot)

    def first_pair(with_scatter):
        @pl.when(b > 0)
        def _():
            scatter_wait()

        gather_wait(slot)
        x_lo, x_hi = _unpack_pairs(xbuf[slot])
        xlo_s[...] = x_lo.astype(BF16)
        xhi_s[...] = x_hi.astype(BF16)
        gather_start(tokn_ref, 1 - slot)
        if with_scatter:
            scatter_start(dstp_ref, 1 - slot)
        ybuf[slot] = expert()

    def later_pair(with_scatter):
        if with_scatter:
            scatter_start(dstp_ref, 1 - slot)
        y = expert()
        mine = lax.broadcasted_iota(I32, y.shape, 0) >= plo_ref[p]
        ybuf[slot] = jnp.where(mine, y, ybuf[slot])

    for is_first, body in ((True, first_pair), (False, later_pair)):
        for with_scatter in (False, True):
            cond = jnp.logical_and(valid, jnp.logical_and(first == is_first, scatters == with_scatter))
            pl.when(cond)(functools.partial(body, with_scatter))

    @pl.when(p == n_pairs - 1)
    def _():
        scatter_wait()
        scatter_start(dstc_ref, slot)
        scatter_wait()
        gather_wait(1 - slot)


def _experts(h_pack, w_gate, w_up, w_down, pairs, tok_blocks, dst_blocks):
    t, c = h_pack.shape
    d = 2 * c
    e, _, de = w_gate.shape
    nblk = tok_blocks.shape[0]
    bm = MOE_BM
    pb, pe, plo, pfl, pne, n_pairs = pairs
    cur = lambda p, pb, *_: (pb[p], 0, 0)
    nxt = lambda p, pb, *_: (jnp.minimum(pb[p] + 1, nblk - 1), 0, 0)
    prv = lambda p, pb, *_: (jnp.maximum(pb[p] - 1, 0), 0, 0)
    smem_blk = lambda m: pl.BlockSpec((1, 1, bm), m, memory_space=pltpu.SMEM)
    hbm = pl.BlockSpec(memory_space=pl.ANY)
    grid_spec = pltpu.PrefetchScalarGridSpec(
        num_scalar_prefetch=6,
        grid=(pb.shape[0],),
        in_specs=[smem_blk(cur), smem_blk(nxt), smem_blk(prv), smem_blk(cur), hbm, hbm, hbm, hbm],
        out_specs=hbm,
        scratch_shapes=[pltpu.VMEM((2, bm, c), U32), pltpu.VMEM((2, bm, c), U32),
                        pltpu.VMEM((bm, c), BF16), pltpu.VMEM((bm, c), BF16),
                        pltpu.VMEM((2, d, de), F32), pltpu.VMEM((2, d, de), F32), pltpu.VMEM((2, de, d), F32),
                        pltpu.SemaphoreType.DMA((2,)), pltpu.SemaphoreType.DMA((1,)),
                        pltpu.SemaphoreType.DMA((2,))],
    )
    return pl.pallas_call(
        _experts_body,
        grid_spec=grid_spec,
        out_shape=jax.ShapeDtypeStruct((nblk * bm, c), U32),
        compiler_params=_cparams("arbitrary"),
        name="experts",
    )(pb, pe, plo, pfl, pne, n_pairs, tok_blocks, tok_blocks, dst_blocks, dst_blocks,
      h_pack, w_gate, w_up, w_down)


def _dispatch(eidx_t, n_experts):
    k, t = eidx_t.shape
    a = t * k
    bm = MOE_BM
    nb = a // bm
    flat_e = eidx_t.reshape(a)
    id_bits = (a - 1).bit_length()
    assert id_bits + (n_experts - 1).bit_length() < 32
    keyed = lax.sort(flat_e * (1 << id_bits) + lax.iota(I32, a))
    se, order = keyed >> id_bits, keyed & ((1 << id_bits) - 1)
    tok = order % t
    ex = lax.iota(I32, n_experts)
    starts = jnp.sum((se[None, :] < ex[:, None]).astype(I32), axis=1)
    seb = se.reshape(nb, bm)
    e_lo, e_hi = seb[:, 0], seb[:, bm - 1]
    npair = e_hi - e_lo + 1
    cum = jnp.cumsum(npair)
    off = cum - npair
    p = lax.iota(I32, nb + n_experts)
    pb = jnp.minimum(jnp.sum((cum[None, :] <= p[:, None]).astype(I32), axis=1), nb - 1)
    pe = jnp.minimum(e_lo[pb] + p - off[pb], e_hi[pb])
    plo = jnp.clip(starts[pe] - pb * bm, 0, bm)
    valid = p < cum[-1]
    new_e = jnp.logical_and(valid, jnp.concatenate([jnp.ones((1,), bool), pe[1:] != pe[:-1]]))
    wslot = (jnp.cumsum(new_e.astype(I32)) - 1) % 2
    later = jnp.logical_and(pe[None, :] > pe[:, None], valid[None, :])
    pne = jnp.min(jnp.where(later, pe[None, :], n_experts), axis=1)
    j = p - off[pb]
    scatters = jnp.logical_or(j == 1, jnp.logical_and(j == 0, npair[pb] == 1))
    pfl = (PAIR_FIRST_OF_BLOCK * (j == 0) + PAIR_NEW_EXPERT * new_e + PAIR_HAS_NEXT_EXPERT * (pne < n_experts)
           + PAIR_WEIGHT_SLOT * wslot + PAIR_SCATTERS * scatters).astype(I32)
    pne = jnp.minimum(pne, n_experts - 1)
    pairs = (pb.astype(I32), pe.astype(I32), plo.astype(I32), pfl, pne.astype(I32), cum[-1:].astype(I32))
    return pairs, tok.reshape(nb, 1, bm), order.reshape(nb, 1, bm)


def _combine_body(x1_ref, h_ref, w_ref, g2_ref, wsg_ref, wsu_ref, wsd_ref, lg_ref, lb_ref, *rest, bb, r):
    y_refs, o_ref = rest[:TOP_K], rest[TOP_K]
    c = h_ref.shape[-1]
    h_lo, h_hi = _unpack_pairs(h_ref[...])
    h_lo, h_hi = h_lo.astype(BF16), h_hi.astype(BF16)

    def proj(w_ref):
        return (jnp.dot(h_lo, w_ref[:c, :], preferred_element_type=F32)
                + jnp.dot(h_hi, w_ref[c:, :], preferred_element_type=F32))

    g = proj(wsg_ref)
    u = proj(wsu_ref)
    shared = jnp.dot((g * _sigmoid(g) * u).astype(BF16), wsd_ref[...], preferred_element_type=F32)
    w = w_ref[...]
    r_lo = r_hi = None
    for k in range(TOP_K):
        y_lo, y_hi = _unpack_pairs(y_refs[k][...])
        wk = w[:, k:k + 1]
        r_lo = wk * y_lo if r_lo is None else r_lo + wk * y_lo
        r_hi = wk * y_hi if r_hi is None else r_hi + wk * y_hi
    routed = jnp.concatenate([r_lo, r_hi], axis=-1)
    res = DN_ALPHA * x1_ref[...] + (1.0 + _rows(g2_ref, bb, r)) * (routed + shared)
    o_ref[...] = _ln(res) * lg_ref[...] + lb_ref[...]


def _combine(x1, h_pack, y_pack, wsel, g2, wsg_bf, wsu_bf, wsd_bf, ln_g, ln_b, nb, seq, row_off):
    t, d = x1.shape
    t_all, c = h_pack.shape
    ds_ = wsg_bf.shape[1]
    rows = COMB_ROWS
    if seq >= rows:
        bb, r, nl = 1, rows, seq // rows
        grid = (nb, nl)
        row_map = lambda b, l: (b * nl + l, 0)
    else:
        bb, r, nl = rows // seq, seq, 1
        grid = (nb // bb, 1)
        row_map = lambda b, l: (b, 0)
    seq_map = lambda b, l: (b, 0, 0)
    off = row_off // rows
    all_map = lambda b, l: (row_map(b, l)[0] + off, 0)
    const2 = lambda b, l: (0, 0)

    def slot_map(k):
        return lambda b, l: (row_map(b, l)[0] + off + k * (t_all // rows), 0)

    return pl.pallas_call(
        functools.partial(_combine_body, bb=bb, r=r),
        grid=grid,
        in_specs=[pl.BlockSpec((rows, d), row_map),
                  pl.BlockSpec((rows, c), all_map),
                  pl.BlockSpec((rows, TOP_K), all_map),
                  pl.BlockSpec((bb, 1, d), seq_map),
                  pl.BlockSpec((d, ds_), const2),
                  pl.BlockSpec((d, ds_), const2),
                  pl.BlockSpec((ds_, d), const2),
                  pl.BlockSpec((1, d), const2),
                  pl.BlockSpec((1, d), const2)]
                 + [pl.BlockSpec((rows, c), slot_map(k)) for k in range(TOP_K)],
        out_specs=pl.BlockSpec((rows, d), row_map),
        out_shape=jax.ShapeDtypeStruct((t, d), F32),
        compiler_params=_cparams("arbitrary", "arbitrary"),
        name="combine",
    )(x1, h_pack, wsel, g2, wsg_bf, wsu_bf, wsd_bf, ln_g.reshape(1, d), ln_b.reshape(1, d),
      *([y_pack] * TOP_K))


def kernel(x_prompt, x_sample, state_pool, state_ssm_re, state_ssm_im, c_prompt, c_sample, w_ada, b_ada, w_in, w_pool, pool_scale, A_re, A_im, log_dt, B_re, B_im, C_re, C_im, D_skip, w_glu, b_glu, g_pool, g_ssm, w_out, ln1_g, ln1_b, w_router, router_bias, w_e_gate, w_e_up, w_e_down, w_sh_gate, w_sh_up, w_sh_down, ln2_g, ln2_b):
    bp, lp, d = x_prompt.shape
    bs, ls, _ = x_sample.shape
    depth = w_ada.shape[0]
    assert depth == DEPTH == 1
    tp, ts = bp * lp, bs * ls
    t_all = tp + ts
    l = 0
    c_pool = pool_scale.shape[-1]
    n_groups, n_state = A_re.shape[1], A_re.shape[2]
    nstate = n_groups * n_state
    n_experts = w_router.shape[-1]
    gpt = SCAN_W // n_state

    c_all = jnp.concatenate([c_prompt, c_sample], axis=0)
    pad = (-c_all.shape[0]) % SUBLANES
    c_all = jnp.pad(c_all, ((0, pad), (0, 0)))
    mod = _adaln(c_all, w_ada[l], b_ada[l]).reshape(c_all.shape[0], 6, 1, d)
    mod_p = [mod[:bp, i] for i in range(6)]
    mod_s = [mod[bp:bp + bs, i] for i in range(6)]

    w_in_bf = w_in[l].astype(BF16)
    w_pool_bf = w_pool[l].astype(BF16)
    w_glu_bf = w_glu[l].astype(BF16)
    w_out_bf = w_out[l].astype(BF16)
    w_router_t_bf = w_router[l].T.astype(BF16)
    wsg_bf, wsu_bf, wsd_bf = w_sh_gate[l].astype(BF16), w_sh_up[l].astype(BF16), w_sh_down[l].astype(BF16)
    pw_r, pw_i, bb_r, bb_i = _ssm_prep(A_re[l], A_im[l], log_dt[l], B_re[l], B_im[l], ROWS // SUBLANES)
    bbr_bd = _block_diag(jnp.swapaxes(bb_r, 1, 2), gpt).astype(BF16)
    bbi_bd = _block_diag(jnp.swapaxes(bb_i, 1, 2), gpt).astype(BF16)
    cr_bd = _block_diag(jnp.swapaxes(C_re[l], 1, 2), gpt).astype(BF16)
    nci_bd = _block_diag(jnp.swapaxes(-C_im[l], 1, 2), gpt).astype(BF16)
    d_skip = D_skip[l].reshape(-1)

    groups = [
        dict(x=x_prompt.reshape(tp, d), nb=bp, seq=lp, mod=mod_p, start=0, row_off=0,
             prefix=jnp.zeros((bp, POOL_HALO, c_pool), F32),
             h0r=jnp.zeros((bp, 1, nstate), F32), h0i=jnp.zeros((bp, 1, nstate), F32)),
        dict(x=x_sample.reshape(ts, d), nb=bs, seq=ls, mod=mod_s, start=PAST_LEN, row_off=tp,
             prefix=jnp.pad(state_pool[l], ((0, 0), (POOL_HALO - state_pool.shape[2], 0), (0, 0))),
             h0r=state_ssm_re[l].reshape(bs, nstate), h0i=state_ssm_im[l].reshape(bs, nstate)),
    ]

    h_all = None
    for gr in groups:
        sh1, sc1, g1, sh2, sc2, g2 = gr['mod']
        nb, seq = gr['nb'], gr['seq']
        perm = _scan_perm(ROWS // SUBLANES if seq >= ROWS else seq)
        z = _mix_in(gr['x'], sh1, sc1, w_in_bf, perm, nb, seq)
        ya = _pool(z, gr['prefix'], w_pool_bf, pool_scale[l], g_pool[l], nb, seq, gr['start'])
        ys, hfr, hfi = _ssm(z, gr['h0r'], gr['h0i'], bbr_bd, bbi_bd, cr_bd, nci_bd, d_skip, pw_r, pw_i,
                            w_glu_bf, b_glu[l], g_ssm[l], perm.T, nb, seq)
        x1, h_all, eidx_t, ew_t = _out_proj(gr['x'], ya, ys, g1, sh2, sc2, w_out_bf, ln1_g[l], ln1_b[l],
                                            w_router_t_bf, router_bias[l], nb, seq,
                                            h_all, t_all, gr['row_off'])
        gr.update(z=z, x1=x1, eidx_t=eidx_t, ew_t=ew_t, hfr=hfr, hfi=hfi)

    eidx_t = jnp.concatenate([gr['eidx_t'] for gr in groups], axis=1)
    wsel = jnp.concatenate([gr['ew_t'] for gr in groups], axis=1).T
    pairs, tok_blocks, dst_blocks = _dispatch(eidx_t, n_experts)
    y3 = _experts(h_all, w_e_gate[l], w_e_up[l], w_e_down[l], pairs, tok_blocks, dst_blocks)

    outs = []
    for gr in groups:
        outs.append(_combine(gr['x1'], h_all, y3, wsel, gr['mod'][5], wsg_bf, wsu_bf, wsd_bf,
                             ln2_g[l], ln2_b[l], gr['nb'], gr['seq'], gr['row_off']))
    y_prompt = outs[0].reshape(bp, lp, d)
    y_sample = outs[1].reshape(bs, ls, d)

    nbuf = state_pool.shape[2]
    zp = groups[0]['z'].reshape(bp, lp, d)[:, :, :c_pool]
    zs = groups[1]['z'].reshape(bs, ls, d)[:, :, :c_pool]
    pool_p = zp[:, lp - nbuf:, :][None]
    pool_s = jnp.concatenate([state_pool[l], zs], axis=1)[:, -nbuf:, :][None]
    st = lambda a, nb: a.reshape(nb, n_groups, n_state)[None]
    return (y_prompt, y_sample, pool_p, pool_s,
            st(groups[0]['hfr'], bp), st(groups[0]['hfi'], bp),
            st(groups[1]['hfr'], bs), st(groups[1]['hfi'], bs))
```

```python
import functools

import jax
import jax.numpy as jnp
from jax import lax
from jax.experimental import pallas as pl
from jax.experimental.pallas import tpu as pltpu

F32 = jnp.float32
BF16 = jnp.bfloat16
I32 = jnp.int32
U32 = jnp.uint32

DEPTH = 1
PAST_LEN = 16384
POOL_WINDOWS = (2, 4, 8, 16)
POOL_HALO = 16
SSM_P = 16
SSM_N = 64
N_EXPERT_GROUPS = 8
TOPK_GROUPS = 4
TOP_K = 8
ROUTED_SCALE = 2.5
LN_EPS = 1e-5
DN_ALPHA = (2.0 * DEPTH) ** 0.25

ROWS = 256
SUBLANES = 8
LANES = 128
SCAN_W = 512
MOE_BM = 256
PAIR_FIRST_OF_BLOCK, PAIR_NEW_EXPERT, PAIR_HAS_NEXT_EXPERT, PAIR_WEIGHT_SLOT, PAIR_SCATTERS = 1, 2, 4, 8, 16
COMB_ROWS = 256
VMEM_LIMIT = 56 * 1024 * 1024


def _cparams(*sem):
    return pltpu.CompilerParams(dimension_semantics=sem, vmem_limit_bytes=VMEM_LIMIT)


def _ln(x):
    xc = x - jnp.mean(x, axis=-1, keepdims=True)
    return xc * lax.rsqrt(jnp.mean(xc * xc, axis=-1, keepdims=True) + LN_EPS)


def _rows(m_ref, bb, r):
    m = m_ref[...]
    c = m.shape[-1]
    return jnp.broadcast_to(m, (bb, r, c)).reshape(bb * r, c)


def _sigmoid(x):
    return 1.0 / (1.0 + jnp.exp(-x))


def _bf16_bits(x):
    return lax.bitcast_convert_type(x.astype(BF16).astype(F32), U32)


def _pack_pairs(x):
    c = x.shape[-1] // 2
    return (_bf16_bits(x[:, :c]) >> 16) | _bf16_bits(x[:, c:])


def _store_tiles(ref, x, lead=()):
    for j in range(x.shape[-1] // LANES):
        ref[lead + (slice(None), j, slice(None))] = x[:, j * LANES:(j + 1) * LANES]


def _load_tiles(ref, lead=()):
    return jnp.concatenate([ref[lead + (slice(None), j, slice(None))] for j in range(ref.shape[-2])], axis=-1)


def _unpack_pairs(w):
    return (lax.bitcast_convert_type(w << 16, F32),
            lax.bitcast_convert_type(w & jnp.uint32(0xFFFF0000), F32))


def _adaln_body(c_ref, w_ref, b_ref, o_ref):
    c = c_ref[...]
    s = (c * _sigmoid(c)).astype(BF16)
    o_ref[...] = jnp.dot(s, w_ref[...].astype(BF16), preferred_element_type=F32) + b_ref[...]


def _adaln(c_all, w_ada, b_ada):
    bc, d = c_all.shape
    n = w_ada.shape[1]
    tn = 1024
    return pl.pallas_call(
        _adaln_body,
        grid=(n // tn,),
        in_specs=[pl.BlockSpec((bc, d), lambda j: (0, 0)),
                  pl.BlockSpec((d, tn), lambda j: (0, j)),
                  pl.BlockSpec((1, tn), lambda j: (0, j))],
        out_specs=pl.BlockSpec((bc, tn), lambda j: (0, j)),
        out_shape=jax.ShapeDtypeStruct((bc, n), F32),
        compiler_params=_cparams("arbitrary"),
        name="adaln",
    )(c_all, w_ada, b_ada.reshape(1, n))


def _ssm_prep_body(ar_ref, ai_ref, dt_ref, ar16_ref, ai16_ref, dt16_ref, br_ref, bi_ref,
                   pr_ref, pi_ref, bbr_ref, bbi_ref):
    def zoh(a_r, a_i, dt):
        mag = jnp.exp(a_r * dt)
        ab_r, ab_i = mag * jnp.cos(a_i * dt), mag * jnp.sin(a_i * dt)
        den = a_r * a_r + a_i * a_i
        nr = ab_r - 1.0
        return ab_r, ab_i, (nr * a_r + ab_i * a_i) / den, (ab_i * a_r - nr * a_i) / den

    ab_r, ab_i, _, _ = zoh(ar_ref[...], ai_ref[...], jnp.exp(dt_ref[...]))
    p_r, p_i = ab_r, ab_i
    for k in range(pr_ref.shape[0]):
        pr_ref[k] = p_r
        pi_ref[k] = p_i
        p_r, p_i = p_r * ab_r - p_i * ab_i, p_r * ab_i + p_i * ab_r
    _, _, f_r, f_i = zoh(ar16_ref[...], ai16_ref[...], jnp.exp(dt16_ref[...]))
    b_r, b_i = br_ref[...], bi_ref[...]
    bbr_ref[...] = f_r * b_r - f_i * b_i
    bbi_ref[...] = f_r * b_i + f_i * b_r


def _ssm_prep(a_re, a_im, log_dt, b_re, b_im, npow):
    g, n = a_re.shape
    p = b_re.shape[-1]
    dt = jnp.broadcast_to(log_dt[:, None], (g, n))
    rep = lambda a: jnp.repeat(a, p, axis=-1)
    outs = pl.pallas_call(
        _ssm_prep_body,
        out_shape=(jax.ShapeDtypeStruct((npow, g, n), F32), jax.ShapeDtypeStruct((npow, g, n), F32),
                   jax.ShapeDtypeStruct((g, n * p), F32), jax.ShapeDtypeStruct((g, n * p), F32)),
        name="ssm_prep",
    )(a_re, a_im, dt, rep(a_re), rep(a_im), rep(dt), b_re.reshape(g, n * p), b_im.reshape(g, n * p))
    pw_r, pw_i, bb_r, bb_i = outs
    return (pw_r.reshape(npow, g * n), pw_i.reshape(npow, g * n),
            bb_r.reshape(g, n, p), bb_i.reshape(g, n, p))


def _block_diag(w, gpt):
    g, a, b = w.shape
    w4 = w.reshape(g // gpt, gpt, a, b)
    eye = jnp.eye(gpt, dtype=w.dtype)
    return jnp.einsum('jgab,gh->jgahb', w4, eye).reshape(g // gpt, gpt * a, gpt * b)


def _scan_perm(seglen):
    new = jnp.arange(ROWS)
    grp, rem = new // (SUBLANES * seglen), new % (SUBLANES * seglen)
    old = grp * (SUBLANES * seglen) + (rem % SUBLANES) * seglen + rem // SUBLANES
    return (old[:, None] == jnp.arange(ROWS)[None, :]).astype(BF16)


def _mix_in_body(x_ref, sh_ref, sc_ref, w_ref, perm_ref, z_ref, *, bb, r):
    c = z_ref.shape[-1] // 2
    u = (_ln(x_ref[...]) * (1.0 + _rows(sc_ref, bb, r)) + _rows(sh_ref, bb, r)).astype(BF16)
    z_ref[:, :c] = jnp.dot(u, w_ref[:, :c], preferred_element_type=F32)
    up = jnp.dot(perm_ref[...], u, preferred_element_type=F32).astype(BF16)
    z_ref[:, c:] = jnp.dot(up, w_ref[:, c:], preferred_element_type=F32)


def _seq_grid(nb, seq):
    if seq >= ROWS:
        bb, r, nl = 1, ROWS, seq // ROWS
        grid = (nb, nl)
        row_map = lambda b, l: (b * nl + l, 0)
    else:
        bb, r, nl = ROWS // seq, seq, 1
        grid = (nb // bb, 1)
        row_map = lambda b, l: (b, 0)
    seq_map = lambda b, l: (b, 0, 0)
    return bb, r, grid, row_map, seq_map


def _mix_in(x2, sh, sc, w_in_bf, perm, nb, seq):
    t, d = x2.shape
    bb, r, grid, row_map, seq_map = _seq_grid(nb, seq)
    const2 = lambda b, l: (0, 0)
    return pl.pallas_call(
        functools.partial(_mix_in_body, bb=bb, r=r),
        grid=grid,
        in_specs=[pl.BlockSpec((ROWS, d), row_map),
                  pl.BlockSpec((bb, 1, d), seq_map),
                  pl.BlockSpec((bb, 1, d), seq_map),
                  pl.BlockSpec((d, d), const2),
                  pl.BlockSpec((ROWS, ROWS), const2)],
        out_specs=pl.BlockSpec((ROWS, d), row_map),
        out_shape=jax.ShapeDtypeStruct((t, d), F32),
        compiler_params=_cparams("arbitrary", "arbitrary"),
        name="mix_in",
    )(x2, sh, sc, w_in_bf, perm)


def _pool_body(z_ref, pre_ref, wp_ref, ps_ref, gp_ref, o_ref, carry_ref, *, bb, r, start_pos):
    li = pl.program_id(1)
    c = z_ref.shape[-1]
    gw = c // len(POOL_WINDOWS)
    rp = POOL_HALO + r

    @pl.when(li == 0)
    def _():
        carry_ref[...] = pre_ref[...]

    za = z_ref[...].reshape(bb, r, c)
    xp3 = jnp.concatenate([carry_ref[...], za], axis=1)
    carry_ref[...] = xp3[:, r:, :]
    xp = xp3.reshape(bb * rp, c)
    pos1 = lax.broadcasted_iota(I32, (bb, r, gw), 1) + (start_pos + 1) + li * r
    outs = []
    ssq = jnp.zeros((bb * r, 1), F32)
    for gi, w in enumerate(POOL_WINDOWS):
        cols = slice(gi * gw, (gi + 1) * gw)
        s = xp[:, cols]
        sh = 1
        while sh < w:
            s = s + pltpu.roll(s, sh, 0)
            sh *= 2
        win = s.reshape(bb, rp, gw)[:, POOL_HALO:, :]
        cnt = jnp.minimum(pos1, w).astype(F32)
        d = (win / cnt - za[:, :, cols]).reshape(bb * r, gw)
        y = jnp.dot(d.astype(BF16), wp_ref[gi], preferred_element_type=F32) * ps_ref[:, cols]
        ssq = ssq + jnp.sum(y * y, axis=-1, keepdims=True)
        outs.append(y)
    scale = lax.rsqrt(ssq * (1.0 / c) + LN_EPS)
    for gi, y in enumerate(outs):
        cols = slice(gi * gw, (gi + 1) * gw)
        o_ref[:, cols] = (y * scale * gp_ref[:, cols]).astype(o_ref.dtype)


def _pool(z, prefix16, w_pool_bf, pool_scale, g_pool, nb, seq, start_pos):
    t = z.shape[0]
    c = pool_scale.shape[-1]
    bb, r, grid, row_map, seq_map = _seq_grid(nb, seq)
    const2 = lambda b, l: (0, 0)
    return pl.pallas_call(
        functools.partial(_pool_body, bb=bb, r=r, start_pos=start_pos),
        grid=grid,
        in_specs=[pl.BlockSpec((ROWS, c), row_map),
                  pl.BlockSpec((bb, POOL_HALO, c), seq_map),
                  pl.BlockSpec(w_pool_bf.shape, lambda b, l: (0, 0, 0)),
                  pl.BlockSpec((1, c), const2),
                  pl.BlockSpec((1, c), const2)],
        out_specs=pl.BlockSpec((ROWS, c), row_map),
        out_shape=jax.ShapeDtypeStruct((t, c), BF16),
        scratch_shapes=[pltpu.VMEM((bb, POOL_HALO, c), F32)],
        compiler_params=_cparams("arbitrary", "arbitrary"),
        name="pool",
    )(z, prefix16, w_pool_bf, pool_scale.reshape(1, c), g_pool.reshape(1, c))


def _cmul_add(x_r, x_i, m_r, m_i, y_r, y_i):
    return x_r + m_r * y_r - m_i * y_i, x_i + m_r * y_i + m_i * y_r


def _ssm_body(z_ref, h0r_ref, h0i_ref, bbr_ref, bbi_ref, cr_ref, nci_ref, dsk_ref, pr_ref, pi_ref,
              wg_ref, bg_ref, gs_ref, unperm_ref, o_ref, hfr_ref, hfi_ref, hr_s, hi_s, y_s, car_s, cai_s,
              *, seglen, chained):
    li = pl.program_id(1)
    c = z_ref.shape[-1]
    nstate = hr_s.shape[-1]
    ntile = nstate // SCAN_W
    cw = c // ntile
    u = z_ref[...]
    ub = u.astype(BF16)
    for j in range(ntile):
        sl = slice(j * SCAN_W, (j + 1) * SCAN_W)
        uj = ub[:, j * cw:(j + 1) * cw]
        hr_s[:, sl] = jnp.dot(uj, bbr_ref[j], preferred_element_type=F32)
        hi_s[:, sl] = jnp.dot(uj, bbi_ref[j], preferred_element_type=F32)

    if chained:
        @pl.when(li == 0)
        def _():
            car_s[...] = h0r_ref[0]
            cai_s[...] = h0i_ref[0]

    grp_rows = SUBLANES * seglen
    bc8 = lambda v: jnp.broadcast_to(v, (SUBLANES, SCAN_W))
    for j in range(ntile):
        sl = slice(j * SCAN_W, (j + 1) * SCAN_W)
        a_r, a_i = bc8(pr_ref[0:1, sl]), bc8(pi_ref[0:1, sl])

        def step(t, h, base, sl=sl, a_r=a_r, a_i=a_i):
            off = pl.multiple_of(base + t * SUBLANES, SUBLANES)
            h_r, h_i = _cmul_add(hr_s[pl.ds(off, SUBLANES), sl], hi_s[pl.ds(off, SUBLANES), sl],
                                 a_r, a_i, h[0], h[1])
            hr_s[pl.ds(off, SUBLANES), sl] = h_r
            hi_s[pl.ds(off, SUBLANES), sl] = h_i
            return h_r, h_i

        if chained:
            zero = jnp.zeros((SUBLANES, SCAN_W), F32)
            e_r, e_i = lax.fori_loop(0, seglen, functools.partial(step, base=0), (zero, zero), unroll=4)
            al_r, al_i = pr_ref[seglen - 1:seglen, sl], pi_ref[seglen - 1:seglen, sl]
            s_r, s_i = car_s[:, sl], cai_s[:, sl]
            ent_r, ent_i = [s_r], [s_i]
            for i in range(SUBLANES):
                s_r, s_i = _cmul_add(e_r[i:i + 1], e_i[i:i + 1], al_r, al_i, s_r, s_i)
                if i + 1 < SUBLANES:
                    ent_r.append(s_r)
                    ent_i.append(s_i)
            car_s[:, sl] = s_r
            cai_s[:, sl] = s_i
            hfr_ref[0, :, sl] = s_r
            hfi_ref[0, :, sl] = s_i
            ent_r, ent_i = jnp.concatenate(ent_r, axis=0), jnp.concatenate(ent_i, axis=0)

            def fix(t, _, sl=sl, ent_r=ent_r, ent_i=ent_i):
                off = pl.multiple_of(t * SUBLANES, SUBLANES)
                x_r, x_i = _cmul_add(hr_s[pl.ds(off, SUBLANES), sl], hi_s[pl.ds(off, SUBLANES), sl],
                                     bc8(pr_ref[pl.ds(t, 1), sl]), bc8(pi_ref[pl.ds(t, 1), sl]), ent_r, ent_i)
                hr_s[pl.ds(off, SUBLANES), sl] = x_r
                hi_s[pl.ds(off, SUBLANES), sl] = x_i
                return 0

            lax.fori_loop(0, seglen, fix, 0, unroll=4)
        else:
            for g in range(z_ref.shape[0] // grp_rows):
                rs = slice(g * SUBLANES, (g + 1) * SUBLANES)
                e_r, e_i = lax.fori_loop(0, seglen, functools.partial(step, base=g * grp_rows),
                                         (h0r_ref[rs, sl], h0i_ref[rs, sl]), unroll=True)
                hfr_ref[rs, sl] = e_r
                hfi_ref[rs, sl] = e_i

    for j in range(ntile):
        sl = slice(j * SCAN_W, (j + 1) * SCAN_W)
        cs = slice(j * cw, (j + 1) * cw)
        y_s[:, cs] = (jnp.dot(hr_s[:, sl].astype(BF16), cr_ref[j], preferred_element_type=F32)
                      + jnp.dot(hi_s[:, sl].astype(BF16), nci_ref[j], preferred_element_type=F32)
                      + dsk_ref[:, cs] * u[:, cs])
    y = y_s[...]
    g = 0.5 * y * (1.0 + jnp.tanh(0.7978845608028654 * (y + 0.044715 * (y * y * y))))
    gate = jnp.dot(g.astype(BF16), wg_ref[...], preferred_element_type=F32) + bg_ref[...]
    out = g * _sigmoid(gate)
    scale = lax.rsqrt(jnp.mean(out * out, axis=-1, keepdims=True) + LN_EPS)
    outp = (out * scale * gs_ref[...]).astype(BF16)
    o_ref[...] = jnp.dot(unperm_ref[...], outp, preferred_element_type=F32).astype(o_ref.dtype)


def _ssm(z, h0r, h0i, bbr_bd, bbi_bd, cr_bd, nci_bd, d_skip, pw_r, pw_i, w_glu_bf, b_glu, g_ssm, unperm,
         nb, seq):
    t = z.shape[0]
    c = d_skip.shape[-1]
    nstate = pw_r.shape[-1]
    bb, r, grid, row_map, seq_map = _seq_grid(nb, seq)
    chained = bb == 1
    seglen = r // SUBLANES if chained else r
    const2 = lambda b, l: (0, 0)
    const3 = lambda b, l: (0, 0, 0)
    full = lambda a: pl.BlockSpec(a.shape, const3 if a.ndim == 3 else const2)
    right_half = (lambda b, l: (row_map(b, l)[0], 1))
    if chained:
        st_spec = pl.BlockSpec((1, 1, nstate), seq_map)
        st_shape = jax.ShapeDtypeStruct((nb, 1, nstate), F32)
    else:
        st_spec = pl.BlockSpec((bb, nstate), lambda b, l: (b, 0))
        st_shape = jax.ShapeDtypeStruct((nb, nstate), F32)
    return pl.pallas_call(
        functools.partial(_ssm_body, seglen=seglen, chained=chained),
        grid=grid,
        in_specs=[pl.BlockSpec((ROWS, c), right_half),
                  st_spec, st_spec,
                  full(bbr_bd), full(bbi_bd), full(cr_bd), full(nci_bd),
                  pl.BlockSpec((1, c), const2),
                  full(pw_r), full(pw_i),
                  full(w_glu_bf),
                  pl.BlockSpec((1, c), const2),
                  pl.BlockSpec((1, c), const2),
                  pl.BlockSpec((ROWS, ROWS), const2)],
        out_specs=[pl.BlockSpec((ROWS, c), row_map), st_spec, st_spec],
        out_shape=[jax.ShapeDtypeStruct((t, c), BF16), st_shape, st_shape],
        scratch_shapes=[pltpu.VMEM((ROWS, nstate), F32), pltpu.VMEM((ROWS, nstate), F32),
                        pltpu.VMEM((ROWS, c), F32),
                        pltpu.VMEM((1, nstate), F32), pltpu.VMEM((1, nstate), F32)],
        compiler_params=_cparams("arbitrary", "arbitrary"),
        name="ssm",
    )(z, h0r, h0i, bbr_bd, bbi_bd, cr_bd, nci_bd, d_skip.reshape(1, c), pw_r, pw_i,
      w_glu_bf, b_glu.reshape(1, c), g_ssm.reshape(1, c), unperm)


def _route(s_t, bias_t):
    e, tn = s_t.shape
    per = e // N_EXPERT_GROUPS
    neg = -jnp.inf
    sb = s_t + bias_t
    rowl = lax.broadcasted_iota(I32, (per, tn), 0)
    gscore = []
    for g in range(N_EXPERT_GROUPS):
        blk = sb[g * per:(g + 1) * per]
        m1 = jnp.max(blk, axis=0, keepdims=True)
        i1 = jnp.min(jnp.where(blk == m1, rowl, per), axis=0, keepdims=True)
        m2 = jnp.max(jnp.where(rowl == i1, neg, blk), axis=0, keepdims=True)
        gscore.append(m1 + m2)
    cur = jnp.concatenate(gscore, axis=0)
    rowg = lax.broadcasted_iota(I32, cur.shape, 0)
    gsel = jnp.zeros(cur.shape, F32)
    for _ in range(TOPK_GROUPS):
        m = jnp.max(cur, axis=0, keepdims=True)
        hit = rowg == jnp.min(jnp.where(cur == m, rowg, N_EXPERT_GROUPS), axis=0, keepdims=True)
        gsel = jnp.where(hit, 1.0, gsel)
        cur = jnp.where(hit, neg, cur)
    cur = jnp.concatenate(
        [jnp.where(jnp.broadcast_to(gsel[g:g + 1], (per, tn)) > 0.0, sb[g * per:(g + 1) * per], neg)
         for g in range(N_EXPERT_GROUPS)], axis=0)
    rowe = lax.broadcasted_iota(I32, (e, tn), 0)
    idxs, vals = [], []
    for _ in range(TOP_K):
        m = jnp.max(cur, axis=0, keepdims=True)
        idx = jnp.min(jnp.where(cur == m, rowe, e), axis=0, keepdims=True)
        hit = rowe == idx
        idxs.append(idx)
        vals.append(jnp.sum(jnp.where(hit, s_t, 0.0), axis=0, keepdims=True))
        cur = jnp.where(hit, neg, cur)
    w = jnp.concatenate(vals, axis=0)
    w = w / jnp.sum(w, axis=0, keepdims=True) * ROUTED_SCALE
    return jnp.concatenate(idxs, axis=0), w


def _out_proj_body(x_ref, ya_ref, ys_ref, g1_ref, sh2_ref, sc2_ref, wo_ref, lg_ref, lb_ref, wrt_ref, rb_ref,
                   x1_ref, h_ref, ei_ref, ew_ref, *, bb, r):
    ca = ya_ref.shape[-1]
    m = (jnp.dot(ya_ref[...], wo_ref[:ca, :], preferred_element_type=F32)
         + jnp.dot(ys_ref[...], wo_ref[ca:, :], preferred_element_type=F32))
    res = DN_ALPHA * x_ref[...] + (1.0 + _rows(g1_ref, bb, r)) * m
    x1 = _ln(res) * lg_ref[...] + lb_ref[...]
    x1_ref[...] = x1
    h = _ln(x1) * (1.0 + _rows(sc2_ref, bb, r)) + _rows(sh2_ref, bb, r)
    _store_tiles(h_ref, _pack_pairs(h))
    logit_t = lax.dot_general(wrt_ref[...], h.astype(BF16), (((1,), (1,)), ((), ())),
                              preferred_element_type=F32)
    idx, w = _route(_sigmoid(logit_t), rb_ref[...])
    ei_ref[...] = idx
    ew_ref[...] = w


def _out_proj(x2, ya, ys, g1, sh2, sc2, w_out_bf, ln_g, ln_b, w_router_t_bf, router_bias, nb, seq,
              h_all, t_all, row_off):
    t, d = x2.shape
    ca = ya.shape[-1]
    e = w_router_t_bf.shape[0]
    bb, r, grid, row_map, seq_map = _seq_grid(nb, seq)
    const2 = lambda b, l: (0, 0)
    tok_map = lambda b, l: (0, row_map(b, l)[0])
    blk_off = row_off // ROWS
    h_map = lambda b, l: (row_map(b, l)[0] + blk_off, 0, 0)
    in_specs = [pl.BlockSpec((ROWS, d), row_map),
                pl.BlockSpec((ROWS, ca), row_map),
                pl.BlockSpec((ROWS, ca), row_map),
                pl.BlockSpec((bb, 1, d), seq_map),
                pl.BlockSpec((bb, 1, d), seq_map),
                pl.BlockSpec((bb, 1, d), seq_map),
                pl.BlockSpec((d, d), const2),
                pl.BlockSpec((1, d), const2),
                pl.BlockSpec((1, d), const2),
                pl.BlockSpec((e, d), const2),
                pl.BlockSpec((e, 1), const2)]
    args = [x2, ya, ys, g1, sh2, sc2, w_out_bf, ln_g.reshape(1, d), ln_b.reshape(1, d),
            w_router_t_bf, router_bias.reshape(e, 1)]
    body = functools.partial(_out_proj_body, bb=bb, r=r)
    aliases = {}
    if h_all is not None:
        in_specs.append(pl.BlockSpec(memory_space=pl.ANY))
        args.append(h_all)
        aliases = {len(args) - 1: 1}
        inner = body
        body = lambda *refs: inner(*refs[:11], *refs[12:])
    return pl.pallas_call(
        body,
        grid=grid,
        in_specs=in_specs,
        out_specs=[pl.BlockSpec((ROWS, d), row_map),
                   pl.BlockSpec((ROWS, d // 2 // LANES, LANES), h_map),
                   pl.BlockSpec((TOP_K, ROWS), tok_map),
                   pl.BlockSpec((TOP_K, ROWS), tok_map)],
        out_shape=[jax.ShapeDtypeStruct((t, d), F32),
                   jax.ShapeDtypeStruct((t_all, d // 2 // LANES, LANES), U32),
                   jax.ShapeDtypeStruct((TOP_K, t), I32),
                   jax.ShapeDtypeStruct((TOP_K, t), F32)],
        input_output_aliases=aliases,
        compiler_params=_cparams("arbitrary", "arbitrary"),
        name="out_proj",
    )(*args)


def _experts_body(pb_ref, pe_ref, plo_ref, pfl_ref, pne_ref, np_ref, tokc_ref, tokn_ref, dstp_ref, dstc_ref,
                  h_hbm, wg_hbm, wu_hbm, wd_hbm, y_hbm, xbuf, ybuf, xlo_s, xhi_s, wg_s, wu_s, wd_s,
                  gsem, ssem, wsem):
    p = pl.program_id(0)
    n_pairs = np_ref[0]
    valid = p < n_pairs
    b = pb_ref[p]
    slot = b % 2
    flags = pfl_ref[p]
    first = (flags & PAIR_FIRST_OF_BLOCK) != 0
    scatters = (flags & PAIR_SCATTERS) != 0
    new_expert = (flags & PAIR_NEW_EXPERT) != 0
    has_next = (flags & PAIR_HAS_NEXT_EXPERT) != 0
    wslot = (flags & PAIR_WEIGHT_SLOT) // PAIR_WEIGHT_SLOT
    bm, c = ybuf.shape[1], ybuf.shape[2]
    weights = ((wg_hbm, wg_s), (wu_hbm, wu_s), (wd_hbm, wd_s))

    def weights_start(e, s):
        for w_hbm, w_s in weights:
            pltpu.make_async_copy(w_hbm.at[e], w_s.at[s], wsem.at[s]).start(priority=1)

    def weights_wait(s):
        for w_hbm, w_s in weights:
            pltpu.make_async_copy(w_hbm.at[0], w_s.at[s], wsem.at[s]).wait()

    def gather_start(tok_ref, s):
        for i in range(bm):
            pltpu.make_async_copy(h_hbm.at[tok_ref[0, 0, i]], xbuf.at[s, i], gsem.at[s]).start()

    def gather_wait(s):
        pltpu.make_async_copy(h_hbm.at[pl.ds(0, bm)], xbuf.at[s], gsem.at[s]).wait()

    def scatter_start(dst_ref, s):
        for i in range(bm):
            pltpu.make_async_copy(ybuf.at[s, pl.ds(i, 1), :],
                                  y_hbm.at[pl.ds(dst_ref[0, 0, i], 1), :], ssem.at[0]).start(priority=i % 2)

    def scatter_wait():
        pltpu.make_async_copy(ybuf.at[0], y_hbm.at[pl.ds(0, bm), :], ssem.at[0]).wait()

    def expert():
        def proj(w_s):
            return (jnp.dot(xlo_s[...], w_s[wslot, :c, :].astype(BF16), preferred_element_type=F32)
                    + jnp.dot(xhi_s[...], w_s[wslot, c:, :].astype(BF16), preferred_element_type=F32))

        g = proj(wg_s)
        u = proj(wu_s)
        a = (g * _sigmoid(g) * u).astype(BF16)
        return _pack_pairs(jnp.dot(a, wd_s[wslot].astype(BF16), preferred_element_type=F32))

    @pl.when(p == 0)
    def _():
        weights_start(pe_ref[0], 0)
        gather_start(tokc_ref, 0)
        ybuf[1] = jnp.zeros(ybuf.shape[1:], ybuf.dtype)

    @pl.when(jnp.logical_and(valid, new_expert))
    def _():
        weights_wait(wslot)

        @pl.when(has_next)
        def _():
            weights_start(pne_ref[p], 1 - wslot)

    def first_pair(with_scatter):
        @pl.when(b > 0)
        def _():
            scatter_wait()

        gather_wait(slot)
        for j in range(xbuf.shape[-2]):
            x_lo, x_hi = _unpack_pairs(xbuf[slot, :, j, :])
            xlo_s[:, j * LANES:(j + 1) * LANES] = x_lo.astype(BF16)
            xhi_s[:, j * LANES:(j + 1) * LANES] = x_hi.astype(BF16)
        gather_start(tokn_ref, 1 - slot)
        if with_scatter:
            scatter_start(dstp_ref, 1 - slot)
        ybuf[slot] = expert()

    def later_pair(with_scatter):
        if with_scatter:
            scatter_start(dstp_ref, 1 - slot)
        y = expert()
        mine = lax.broadcasted_iota(I32, y.shape, 0) >= plo_ref[p]
        ybuf[slot] = jnp.where(mine, y, ybuf[slot])

    for is_first, body in ((True, first_pair), (False, later_pair)):
        for with_scatter in (False, True):
            cond = jnp.logical_and(valid, jnp.logical_and(first == is_first, scatters == with_scatter))
            pl.when(cond)(functools.partial(body, with_scatter))

    @pl.when(p == n_pairs - 1)
    def _():
        scatter_wait()
        scatter_start(dstc_ref, slot)
        scatter_wait()
        gather_wait(1 - slot)


def _experts(h_pack, w_gate, w_up, w_down, pairs, tok_blocks, dst_blocks):
    t, ntile, _ = h_pack.shape
    c = ntile * LANES
    d = 2 * c
    e, _, de = w_gate.shape
    nblk = tok_blocks.shape[0]
    bm = MOE_BM
    pb, pe, plo, pfl, pne, n_pairs = pairs
    cur = lambda p, pb, *_: (pb[p], 0, 0)
    nxt = lambda p, pb, *_: (jnp.minimum(pb[p] + 1, nblk - 1), 0, 0)
    prv = lambda p, pb, *_: (jnp.maximum(pb[p] - 1, 0), 0, 0)
    smem_blk = lambda m: pl.BlockSpec((1, 1, bm), m, memory_space=pltpu.SMEM)
    hbm = pl.BlockSpec(memory_space=pl.ANY)
    grid_spec = pltpu.PrefetchScalarGridSpec(
        num_scalar_prefetch=6,
        grid=(pb.shape[0],),
        in_specs=[smem_blk(cur), smem_blk(nxt), smem_blk(prv), smem_blk(cur), hbm, hbm, hbm, hbm],
        out_specs=hbm,
        scratch_shapes=[pltpu.VMEM((2, bm, ntile, LANES), U32), pltpu.VMEM((2, bm, c), U32),
                        pltpu.VMEM((bm, c), BF16), pltpu.VMEM((bm, c), BF16),
                        pltpu.VMEM((2, d, de), F32), pltpu.VMEM((2, d, de), F32), pltpu.VMEM((2, de, d), F32),
                        pltpu.SemaphoreType.DMA((2,)), pltpu.SemaphoreType.DMA((1,)),
                        pltpu.SemaphoreType.DMA((2,))],
    )
    return pl.pallas_call(
        _experts_body,
        grid_spec=grid_spec,
        out_shape=jax.ShapeDtypeStruct((nblk * bm, c), U32),
        compiler_params=_cparams("arbitrary"),
        name="experts",
    )(pb, pe, plo, pfl, pne, n_pairs, tok_blocks, tok_blocks, dst_blocks, dst_blocks,
      h_pack, w_gate, w_up, w_down)


def _dispatch(eidx_t, n_experts):
    k, t = eidx_t.shape
    a = t * k
    bm = MOE_BM
    nb = a // bm
    flat_e = eidx_t.reshape(a)
    id_bits = (a - 1).bit_length()
    assert id_bits + (n_experts - 1).bit_length() < 32
    keyed = lax.sort(flat_e * (1 << id_bits) + lax.iota(I32, a))
    se, order = keyed >> id_bits, keyed & ((1 << id_bits) - 1)
    tok = order % t
    ex = lax.iota(I32, n_experts)
    starts = jnp.sum((se[None, :] < ex[:, None]).astype(I32), axis=1)
    seb = se.reshape(nb, bm)
    e_lo, e_hi = seb[:, 0], seb[:, bm - 1]
    npair = e_hi - e_lo + 1
    cum = jnp.cumsum(npair)
    off = cum - npair
    p = lax.iota(I32, nb + n_experts)
    pb = jnp.minimum(jnp.sum((cum[None, :] <= p[:, None]).astype(I32), axis=1), nb - 1)
    pe = jnp.minimum(e_lo[pb] + p - off[pb], e_hi[pb])
    plo = jnp.clip(starts[pe] - pb * bm, 0, bm)
    valid = p < cum[-1]
    new_e = jnp.logical_and(valid, jnp.concatenate([jnp.ones((1,), bool), pe[1:] != pe[:-1]]))
    wslot = (jnp.cumsum(new_e.astype(I32)) - 1) % 2
    later = jnp.logical_and(pe[None, :] > pe[:, None], valid[None, :])
    pne = jnp.min(jnp.where(later, pe[None, :], n_experts), axis=1)
    j = p - off[pb]
    scatters = jnp.logical_or(j == 1, jnp.logical_and(j == 0, npair[pb] == 1))
    pfl = (PAIR_FIRST_OF_BLOCK * (j == 0) + PAIR_NEW_EXPERT * new_e + PAIR_HAS_NEXT_EXPERT * (pne < n_experts)
           + PAIR_WEIGHT_SLOT * wslot + PAIR_SCATTERS * scatters).astype(I32)
    pne = jnp.minimum(pne, n_experts - 1)
    pairs = (pb.astype(I32), pe.astype(I32), plo.astype(I32), pfl, pne.astype(I32), cum[-1:].astype(I32))
    return pairs, tok.reshape(nb, 1, bm), order.reshape(nb, 1, bm)


def _combine_body(x1_ref, h_ref, w_ref, g2_ref, wsg_ref, wsu_ref, wsd_ref, lg_ref, lb_ref, *rest, bb, r):
    y_refs, o_ref = rest[:TOP_K], rest[TOP_K]
    c = h_ref.shape[-2] * LANES
    h_lo, h_hi = _unpack_pairs(_load_tiles(h_ref))
    h_lo, h_hi = h_lo.astype(BF16), h_hi.astype(BF16)

    def proj(w_ref):
        return (jnp.dot(h_lo, w_ref[:c, :], preferred_element_type=F32)
                + jnp.dot(h_hi, w_ref[c:, :], preferred_element_type=F32))

    g = proj(wsg_ref)
    u = proj(wsu_ref)
    shared = jnp.dot((g * _sigmoid(g) * u).astype(BF16), wsd_ref[...], preferred_element_type=F32)
    w = w_ref[...]
    r_lo = r_hi = None
    for k in range(TOP_K):
        y_lo, y_hi = _unpack_pairs(y_refs[k][...])
        wk = w[:, k:k + 1]
        r_lo = wk * y_lo if r_lo is None else r_lo + wk * y_lo
        r_hi = wk * y_hi if r_hi is None else r_hi + wk * y_hi
    routed = jnp.concatenate([r_lo, r_hi], axis=-1)
    res = DN_ALPHA * x1_ref[...] + (1.0 + _rows(g2_ref, bb, r)) * (routed + shared)
    o_ref[...] = _ln(res) * lg_ref[...] + lb_ref[...]


def _combine(x1, h_pack, y_pack, wsel, g2, wsg_bf, wsu_bf, wsd_bf, ln_g, ln_b, nb, seq, row_off):
    t, d = x1.shape
    t_all, ntile, _ = h_pack.shape
    c = ntile * LANES
    ds_ = wsg_bf.shape[1]
    rows = COMB_ROWS
    if seq >= rows:
        bb, r, nl = 1, rows, seq // rows
        grid = (nb, nl)
        row_map = lambda b, l: (b * nl + l, 0)
    else:
        bb, r, nl = rows // seq, seq, 1
        grid = (nb // bb, 1)
        row_map = lambda b, l: (b, 0)
    seq_map = lambda b, l: (b, 0, 0)
    off = row_off // rows
    all_map = lambda b, l: (row_map(b, l)[0] + off, 0)
    const2 = lambda b, l: (0, 0)

    def slot_map(k):
        return lambda b, l: (row_map(b, l)[0] + off + k * (t_all // rows), 0)

    return pl.pallas_call(
        functools.partial(_combine_body, bb=bb, r=r),
        grid=grid,
        in_specs=[pl.BlockSpec((rows, d), row_map),
                  pl.BlockSpec((rows, ntile, LANES), lambda b, l: (row_map(b, l)[0] + off, 0, 0)),
                  pl.BlockSpec((rows, TOP_K), all_map),
                  pl.BlockSpec((bb, 1, d), seq_map),
                  pl.BlockSpec((d, ds_), const2),
                  pl.BlockSpec((d, ds_), const2),
                  pl.BlockSpec((ds_, d), const2),
                  pl.BlockSpec((1, d), const2),
                  pl.BlockSpec((1, d), const2)]
                 + [pl.BlockSpec((rows, c), slot_map(k)) for k in range(TOP_K)],
        out_specs=pl.BlockSpec((rows, d), row_map),
        out_shape=jax.ShapeDtypeStruct((t, d), F32),
        compiler_params=_cparams("arbitrary", "arbitrary"),
        name="combine",
    )(x1, h_pack, wsel, g2, wsg_bf, wsu_bf, wsd_bf, ln_g.reshape(1, d), ln_b.reshape(1, d),
      *([y_pack] * TOP_K))


def kernel(x_prompt, x_sample, state_pool, state_ssm_re, state_ssm_im, c_prompt, c_sample, w_ada, b_ada, w_in, w_pool, pool_scale, A_re, A_im, log_dt, B_re, B_im, C_re, C_im, D_skip, w_glu, b_glu, g_pool, g_ssm, w_out, ln1_g, ln1_b, w_router, router_bias, w_e_gate, w_e_up, w_e_down, w_sh_gate, w_sh_up, w_sh_down, ln2_g, ln2_b):
    bp, lp, d = x_prompt.shape
    bs, ls, _ = x_sample.shape
    depth = w_ada.shape[0]
    assert depth == DEPTH == 1
    tp, ts = bp * lp, bs * ls
    t_all = tp + ts
    l = 0
    c_pool = pool_scale.shape[-1]
    n_groups, n_state = A_re.shape[1], A_re.shape[2]
    nstate = n_groups * n_state
    n_experts = w_router.shape[-1]
    gpt = SCAN_W // n_state

    c_all = jnp.concatenate([c_prompt, c_sample], axis=0)
    pad = (-c_all.shape[0]) % SUBLANES
    c_all = jnp.pad(c_all, ((0, pad), (0, 0)))
    mod = _adaln(c_all, w_ada[l], b_ada[l]).reshape(c_all.shape[0], 6, 1, d)
    mod_p = [mod[:bp, i] for i in range(6)]
    mod_s = [mod[bp:bp + bs, i] for i in range(6)]

    w_in_bf = w_in[l].astype(BF16)
    w_pool_bf = w_pool[l].astype(BF16)
    w_glu_bf = w_glu[l].astype(BF16)
    w_out_bf = w_out[l].astype(BF16)
    w_router_t_bf = w_router[l].T.astype(BF16)
    wsg_bf, wsu_bf, wsd_bf = w_sh_gate[l].astype(BF16), w_sh_up[l].astype(BF16), w_sh_down[l].astype(BF16)
    pw_r, pw_i, bb_r, bb_i = _ssm_prep(A_re[l], A_im[l], log_dt[l], B_re[l], B_im[l], ROWS // SUBLANES)
    bbr_bd = _block_diag(jnp.swapaxes(bb_r, 1, 2), gpt).astype(BF16)
    bbi_bd = _block_diag(jnp.swapaxes(bb_i, 1, 2), gpt).astype(BF16)
    cr_bd = _block_diag(jnp.swapaxes(C_re[l], 1, 2), gpt).astype(BF16)
    nci_bd = _block_diag(jnp.swapaxes(-C_im[l], 1, 2), gpt).astype(BF16)
    d_skip = D_skip[l].reshape(-1)

    groups = [
        dict(x=x_prompt.reshape(tp, d), nb=bp, seq=lp, mod=mod_p, start=0, row_off=0,
             prefix=jnp.zeros((bp, POOL_HALO, c_pool), F32),
             h0r=jnp.zeros((bp, 1, nstate), F32), h0i=jnp.zeros((bp, 1, nstate), F32)),
        dict(x=x_sample.reshape(ts, d), nb=bs, seq=ls, mod=mod_s, start=PAST_LEN, row_off=tp,
             prefix=jnp.pad(state_pool[l], ((0, 0), (POOL_HALO - state_pool.shape[2], 0), (0, 0))),
             h0r=state_ssm_re[l].reshape(bs, nstate), h0i=state_ssm_im[l].reshape(bs, nstate)),
    ]

    h_all = None
    for gr in groups:
        sh1, sc1, g1, sh2, sc2, g2 = gr['mod']
        nb, seq = gr['nb'], gr['seq']
        perm = _scan_perm(ROWS // SUBLANES if seq >= ROWS else seq)
        z = _mix_in(gr['x'], sh1, sc1, w_in_bf, perm, nb, seq)
        ya = _pool(z, gr['prefix'], w_pool_bf, pool_scale[l], g_pool[l], nb, seq, gr['start'])
        ys, hfr, hfi = _ssm(z, gr['h0r'], gr['h0i'], bbr_bd, bbi_bd, cr_bd, nci_bd, d_skip, pw_r, pw_i,
                            w_glu_bf, b_glu[l], g_ssm[l], perm.T, nb, seq)
        x1, h_all, eidx_t, ew_t = _out_proj(gr['x'], ya, ys, g1, sh2, sc2, w_out_bf, ln1_g[l], ln1_b[l],
                                            w_router_t_bf, router_bias[l], nb, seq,
                                            h_all, t_all, gr['row_off'])
        gr.update(z=z, x1=x1, eidx_t=eidx_t, ew_t=ew_t, hfr=hfr, hfi=hfi)

    eidx_t = jnp.concatenate([gr['eidx_t'] for gr in groups], axis=1)
    wsel = jnp.concatenate([gr['ew_t'] for gr in groups], axis=1).T
    pairs, tok_blocks, dst_blocks = _dispatch(eidx_t, n_experts)
    y3 = _experts(h_all, w_e_gate[l], w_e_up[l], w_e_down[l], pairs, tok_blocks, dst_blocks)

    outs = []
    for gr in groups:
        outs.append(_combine(gr['x1'], h_all, y3, wsel, gr['mod'][5], wsg_bf, wsu_bf, wsd_bf,
                             ln2_g[l], ln2_b[l], gr['nb'], gr['seq'], gr['row_off']))
    y_prompt = outs[0].reshape(bp, lp, d)
    y_sample = outs[1].reshape(bs, ls, d)

    nbuf = state_pool.shape[2]
    zp = groups[0]['z'].reshape(bp, lp, d)[:, :, :c_pool]
    zs = groups[1]['z'].reshape(bs, ls, d)[:, :, :c_pool]
    pool_p = zp[:, lp - nbuf:, :][None]
    pool_s = jnp.concatenate([state_pool[l], zs], axis=1)[:, -nbuf:, :][None]
    st = lambda a, nb: a.reshape(nb, n_groups, n_state)[None]
    return (y_prompt, y_sample, pool_p, pool_s,
            st(groups[0]['hfr'], bp), st(groups[0]['hfi'], bp),
            st(groups[1]['hfr'], bs), st(groups[1]['hfi'], bs))
```

```python
import functools

import jax
import jax.numpy as jnp
from jax import lax
from jax.experimental import pallas as pl
from jax.experimental.pallas import tpu as pltpu

F32 = jnp.float32
BF16 = jnp.bfloat16
I32 = jnp.int32
U32 = jnp.uint32

DEPTH = 1
PAST_LEN = 16384
POOL_WINDOWS = (2, 4, 8, 16)
POOL_HALO = 16
SSM_P = 16
SSM_N = 64
N_EXPERT_GROUPS = 8
TOPK_GROUPS = 4
TOP_K = 8
ROUTED_SCALE = 2.5
LN_EPS = 1e-5
DN_ALPHA = (2.0 * DEPTH) ** 0.25

ROWS = 256
SUBLANES = 8
LANES = 128
SCAN_W = 512
MOE_BM = 256
PAIR_FIRST_OF_BLOCK, PAIR_NEW_EXPERT, PAIR_HAS_NEXT_EXPERT, PAIR_HAS_NEXT2_EXPERT, PAIR_SCATTERS = 1, 2, 4, 8, 16
PAIR_WEIGHT_SLOT = 32
WEIGHT_SLOTS = 3
COMB_ROWS = 256
VMEM_LIMIT = 56 * 1024 * 1024


def _cparams(*sem):
    return pltpu.CompilerParams(dimension_semantics=sem, vmem_limit_bytes=VMEM_LIMIT)


def _ln(x):
    xc = x - jnp.mean(x, axis=-1, keepdims=True)
    return xc * lax.rsqrt(jnp.mean(xc * xc, axis=-1, keepdims=True) + LN_EPS)


def _rows(m_ref, bb, r):
    m = m_ref[...]
    c = m.shape[-1]
    return jnp.broadcast_to(m, (bb, r, c)).reshape(bb * r, c)


def _sigmoid(x):
    return 1.0 / (1.0 + jnp.exp(-x))


def _bf16_bits(x):
    return lax.bitcast_convert_type(x.astype(BF16).astype(F32), U32)


def _pack_pairs(x):
    c = x.shape[-1] // 2
    return (_bf16_bits(x[:, :c]) >> 16) | _bf16_bits(x[:, c:])


def _unpack_pairs(w):
    return (lax.bitcast_convert_type(w << 16, F32),
            lax.bitcast_convert_type(w & jnp.uint32(0xFFFF0000), F32))


def _adaln_body(c_ref, w_ref, b_ref, o_ref):
    c = c_ref[...]
    s = (c * _sigmoid(c)).astype(BF16)
    o_ref[...] = jnp.dot(s, w_ref[...].astype(BF16), preferred_element_type=F32) + b_ref[...]


def _adaln(c_all, w_ada, b_ada):
    bc, d = c_all.shape
    n = w_ada.shape[1]
    tn = 1024
    return pl.pallas_call(
        _adaln_body,
        grid=(n // tn,),
        in_specs=[pl.BlockSpec((bc, d), lambda j: (0, 0)),
                  pl.BlockSpec((d, tn), lambda j: (0, j)),
                  pl.BlockSpec((1, tn), lambda j: (0, j))],
        out_specs=pl.BlockSpec((bc, tn), lambda j: (0, j)),
        out_shape=jax.ShapeDtypeStruct((bc, n), F32),
        compiler_params=_cparams("arbitrary"),
        name="adaln",
    )(c_all, w_ada, b_ada.reshape(1, n))


def _ssm_prep_body(ar_ref, ai_ref, dt_ref, ar16_ref, ai16_ref, dt16_ref, br_ref, bi_ref,
                   pr_ref, pi_ref, bbr_ref, bbi_ref):
    def zoh(a_r, a_i, dt):
        mag = jnp.exp(a_r * dt)
        ab_r, ab_i = mag * jnp.cos(a_i * dt), mag * jnp.sin(a_i * dt)
        den = a_r * a_r + a_i * a_i
        nr = ab_r - 1.0
        return ab_r, ab_i, (nr * a_r + ab_i * a_i) / den, (ab_i * a_r - nr * a_i) / den

    ab_r, ab_i, _, _ = zoh(ar_ref[...], ai_ref[...], jnp.exp(dt_ref[...]))
    p_r, p_i = ab_r, ab_i
    for k in range(pr_ref.shape[0]):
        pr_ref[k] = p_r
        pi_ref[k] = p_i
        p_r, p_i = p_r * ab_r - p_i * ab_i, p_r * ab_i + p_i * ab_r
    _, _, f_r, f_i = zoh(ar16_ref[...], ai16_ref[...], jnp.exp(dt16_ref[...]))
    b_r, b_i = br_ref[...], bi_ref[...]
    bbr_ref[...] = f_r * b_r - f_i * b_i
    bbi_ref[...] = f_r * b_i + f_i * b_r


def _ssm_prep(a_re, a_im, log_dt, b_re, b_im, npow):
    g, n = a_re.shape
    p = b_re.shape[-1]
    dt = jnp.broadcast_to(log_dt[:, None], (g, n))
    rep = lambda a: jnp.repeat(a, p, axis=-1)
    outs = pl.pallas_call(
        _ssm_prep_body,
        out_shape=(jax.ShapeDtypeStruct((npow, g, n), F32), jax.ShapeDtypeStruct((npow, g, n), F32),
                   jax.ShapeDtypeStruct((g, n * p), F32), jax.ShapeDtypeStruct((g, n * p), F32)),
        name="ssm_prep",
    )(a_re, a_im, dt, rep(a_re), rep(a_im), rep(dt), b_re.reshape(g, n * p), b_im.reshape(g, n * p))
    pw_r, pw_i, bb_r, bb_i = outs
    return (pw_r.reshape(npow, g * n), pw_i.reshape(npow, g * n),
            bb_r.reshape(g, n, p), bb_i.reshape(g, n, p))


def _block_diag(w, gpt):
    g, a, b = w.shape
    w4 = w.reshape(g // gpt, gpt, a, b)
    eye = jnp.eye(gpt, dtype=w.dtype)
    return jnp.einsum('jgab,gh->jgahb', w4, eye).reshape(g // gpt, gpt * a, gpt * b)


def _scan_perm(seglen):
    new = jnp.arange(ROWS)
    grp, rem = new // (SUBLANES * seglen), new % (SUBLANES * seglen)
    old = grp * (SUBLANES * seglen) + (rem % SUBLANES) * seglen + rem // SUBLANES
    return (old[:, None] == jnp.arange(ROWS)[None, :]).astype(BF16)


def _mix_in_body(x_ref, sh_ref, sc_ref, w_ref, perm_ref, z_ref, *, bb, r):
    c = z_ref.shape[-1] // 2
    u = (_ln(x_ref[...]) * (1.0 + _rows(sc_ref, bb, r)) + _rows(sh_ref, bb, r)).astype(BF16)
    z_ref[:, :c] = jnp.dot(u, w_ref[:, :c], preferred_element_type=F32)
    up = jnp.dot(perm_ref[...], u, preferred_element_type=F32).astype(BF16)
    z_ref[:, c:] = jnp.dot(up, w_ref[:, c:], preferred_element_type=F32)


def _seq_grid(nb, seq):
    if seq >= ROWS:
        bb, r, nl = 1, ROWS, seq // ROWS
        grid = (nb, nl)
        row_map = lambda b, l: (b * nl + l, 0)
    else:
        bb, r, nl = ROWS // seq, seq, 1
        grid = (nb // bb, 1)
        row_map = lambda b, l: (b, 0)
    seq_map = lambda b, l: (b, 0, 0)
    return bb, r, grid, row_map, seq_map


def _mix_in(x2, sh, sc, w_in_bf, perm, nb, seq):
    t, d = x2.shape
    bb, r, grid, row_map, seq_map = _seq_grid(nb, seq)
    const2 = lambda b, l: (0, 0)
    return pl.pallas_call(
        functools.partial(_mix_in_body, bb=bb, r=r),
        grid=grid,
        in_specs=[pl.BlockSpec((ROWS, d), row_map),
                  pl.BlockSpec((bb, 1, d), seq_map),
                  pl.BlockSpec((bb, 1, d), seq_map),
                  pl.BlockSpec((d, d), const2),
                  pl.BlockSpec((ROWS, ROWS), const2)],
        out_specs=pl.BlockSpec((ROWS, d), row_map),
        out_shape=jax.ShapeDtypeStruct((t, d), F32),
        compiler_params=_cparams("arbitrary", "arbitrary"),
        name="mix_in",
    )(x2, sh, sc, w_in_bf, perm)


def _pool_body(z_ref, pre_ref, wp_ref, ps_ref, gp_ref, o_ref, carry_ref, *, bb, r, start_pos):
    li = pl.program_id(1)
    c = z_ref.shape[-1]
    gw = c // len(POOL_WINDOWS)
    rp = POOL_HALO + r

    @pl.when(li == 0)
    def _():
        carry_ref[...] = pre_ref[...]

    za = z_ref[...].reshape(bb, r, c)
    xp3 = jnp.concatenate([carry_ref[...], za], axis=1)
    carry_ref[...] = xp3[:, r:, :]
    xp = xp3.reshape(bb * rp, c)
    pos1 = lax.broadcasted_iota(I32, (bb, r, gw), 1) + (start_pos + 1) + li * r
    outs = []
    ssq = jnp.zeros((bb * r, 1), F32)
    for gi, w in enumerate(POOL_WINDOWS):
        cols = slice(gi * gw, (gi + 1) * gw)
        s = xp[:, cols]
        sh = 1
        while sh < w:
            s = s + pltpu.roll(s, sh, 0)
            sh *= 2
        win = s.reshape(bb, rp, gw)[:, POOL_HALO:, :]
        cnt = jnp.minimum(pos1, w).astype(F32)
        d = (win / cnt - za[:, :, cols]).reshape(bb * r, gw)
        y = jnp.dot(d.astype(BF16), wp_ref[gi], preferred_element_type=F32) * ps_ref[:, cols]
        ssq = ssq + jnp.sum(y * y, axis=-1, keepdims=True)
        outs.append(y)
    scale = lax.rsqrt(ssq * (1.0 / c) + LN_EPS)
    for gi, y in enumerate(outs):
        cols = slice(gi * gw, (gi + 1) * gw)
        o_ref[:, cols] = (y * scale * gp_ref[:, cols]).astype(o_ref.dtype)


def _pool(z, prefix16, w_pool_bf, pool_scale, g_pool, nb, seq, start_pos):
    t = z.shape[0]
    c = pool_scale.shape[-1]
    bb, r, grid, row_map, seq_map = _seq_grid(nb, seq)
    const2 = lambda b, l: (0, 0)
    return pl.pallas_call(
        functools.partial(_pool_body, bb=bb, r=r, start_pos=start_pos),
        grid=grid,
        in_specs=[pl.BlockSpec((ROWS, c), row_map),
                  pl.BlockSpec((bb, POOL_HALO, c), seq_map),
                  pl.BlockSpec(w_pool_bf.shape, lambda b, l: (0, 0, 0)),
                  pl.BlockSpec((1, c), const2),
                  pl.BlockSpec((1, c), const2)],
        out_specs=pl.BlockSpec((ROWS, c), row_map),
        out_shape=jax.ShapeDtypeStruct((t, c), BF16),
        scratch_shapes=[pltpu.VMEM((bb, POOL_HALO, c), F32)],
        compiler_params=_cparams("arbitrary", "arbitrary"),
        name="pool",
    )(z, prefix16, w_pool_bf, pool_scale.reshape(1, c), g_pool.reshape(1, c))


def _cmul_add(x_r, x_i, m_r, m_i, y_r, y_i):
    return x_r + m_r * y_r - m_i * y_i, x_i + m_r * y_i + m_i * y_r


def _ssm_body(z_ref, h0r_ref, h0i_ref, bbr_ref, bbi_ref, cr_ref, nci_ref, dsk_ref, pr_ref, pi_ref,
              wg_ref, bg_ref, gs_ref, unperm_ref, o_ref, hfr_ref, hfi_ref, hr_s, hi_s, y_s, car_s, cai_s,
              *, seglen, chained):
    li = pl.program_id(1)
    c = z_ref.shape[-1]
    nstate = hr_s.shape[-1]
    ntile = nstate // SCAN_W
    cw = c // ntile
    u = z_ref[...]
    ub = u.astype(BF16)
    for j in range(ntile):
        sl = slice(j * SCAN_W, (j + 1) * SCAN_W)
        uj = ub[:, j * cw:(j + 1) * cw]
        hr_s[:, sl] = jnp.dot(uj, bbr_ref[j], preferred_element_type=F32)
        hi_s[:, sl] = jnp.dot(uj, bbi_ref[j], preferred_element_type=F32)

    if chained:
        @pl.when(li == 0)
        def _():
            car_s[...] = h0r_ref[0]
            cai_s[...] = h0i_ref[0]

    grp_rows = SUBLANES * seglen
    bc8 = lambda v: jnp.broadcast_to(v, (SUBLANES, SCAN_W))
    for j in range(ntile):
        sl = slice(j * SCAN_W, (j + 1) * SCAN_W)
        a_r, a_i = bc8(pr_ref[0:1, sl]), bc8(pi_ref[0:1, sl])

        def step(t, h, base, sl=sl, a_r=a_r, a_i=a_i):
            off = pl.multiple_of(base + t * SUBLANES, SUBLANES)
            h_r, h_i = _cmul_add(hr_s[pl.ds(off, SUBLANES), sl], hi_s[pl.ds(off, SUBLANES), sl],
                                 a_r, a_i, h[0], h[1])
            hr_s[pl.ds(off, SUBLANES), sl] = h_r
            hi_s[pl.ds(off, SUBLANES), sl] = h_i
            return h_r, h_i

        if chained:
            zero = jnp.zeros((SUBLANES, SCAN_W), F32)
            e_r, e_i = lax.fori_loop(0, seglen, functools.partial(step, base=0), (zero, zero), unroll=4)
            al_r, al_i = pr_ref[seglen - 1:seglen, sl], pi_ref[seglen - 1:seglen, sl]
            s_r, s_i = car_s[:, sl], cai_s[:, sl]
            ent_r, ent_i = [s_r], [s_i]
            for i in range(SUBLANES):
                s_r, s_i = _cmul_add(e_r[i:i + 1], e_i[i:i + 1], al_r, al_i, s_r, s_i)
                if i + 1 < SUBLANES:
                    ent_r.append(s_r)
                    ent_i.append(s_i)
            car_s[:, sl] = s_r
            cai_s[:, sl] = s_i
            hfr_ref[0, :, sl] = s_r
            hfi_ref[0, :, sl] = s_i
            ent_r, ent_i = jnp.concatenate(ent_r, axis=0), jnp.concatenate(ent_i, axis=0)

            def fix(t, _, sl=sl, ent_r=ent_r, ent_i=ent_i):
                off = pl.multiple_of(t * SUBLANES, SUBLANES)
                x_r, x_i = _cmul_add(hr_s[pl.ds(off, SUBLANES), sl], hi_s[pl.ds(off, SUBLANES), sl],
                                     bc8(pr_ref[pl.ds(t, 1), sl]), bc8(pi_ref[pl.ds(t, 1), sl]), ent_r, ent_i)
                hr_s[pl.ds(off, SUBLANES), sl] = x_r
                hi_s[pl.ds(off, SUBLANES), sl] = x_i
                return 0

            lax.fori_loop(0, seglen, fix, 0, unroll=4)
        else:
            for g in range(z_ref.shape[0] // grp_rows):
                rs = slice(g * SUBLANES, (g + 1) * SUBLANES)
                e_r, e_i = lax.fori_loop(0, seglen, functools.partial(step, base=g * grp_rows),
                                         (h0r_ref[rs, sl], h0i_ref[rs, sl]), unroll=True)
                hfr_ref[rs, sl] = e_r
                hfi_ref[rs, sl] = e_i

    for j in range(ntile):
        sl = slice(j * SCAN_W, (j + 1) * SCAN_W)
        cs = slice(j * cw, (j + 1) * cw)
        y_s[:, cs] = (jnp.dot(hr_s[:, sl].astype(BF16), cr_ref[j], preferred_element_type=F32)
                      + jnp.dot(hi_s[:, sl].astype(BF16), nci_ref[j], preferred_element_type=F32)
                      + dsk_ref[:, cs] * u[:, cs])
    y = y_s[...]
    g = 0.5 * y * (1.0 + jnp.tanh(0.7978845608028654 * (y + 0.044715 * (y * y * y))))
    gate = jnp.dot(g.astype(BF16), wg_ref[...], preferred_element_type=F32) + bg_ref[...]
    out = g * _sigmoid(gate)
    scale = lax.rsqrt(jnp.mean(out * out, axis=-1, keepdims=True) + LN_EPS)
    outp = (out * scale * gs_ref[...]).astype(BF16)
    o_ref[...] = jnp.dot(unperm_ref[...], outp, preferred_element_type=F32).astype(o_ref.dtype)


def _ssm(z, h0r, h0i, bbr_bd, bbi_bd, cr_bd, nci_bd, d_skip, pw_r, pw_i, w_glu_bf, b_glu, g_ssm, unperm,
         nb, seq):
    t = z.shape[0]
    c = d_skip.shape[-1]
    nstate = pw_r.shape[-1]
    bb, r, grid, row_map, seq_map = _seq_grid(nb, seq)
    chained = bb == 1
    seglen = r // SUBLANES if chained else r
    const2 = lambda b, l: (0, 0)
    const3 = lambda b, l: (0, 0, 0)
    full = lambda a: pl.BlockSpec(a.shape, const3 if a.ndim == 3 else const2)
    right_half = (lambda b, l: (row_map(b, l)[0], 1))
    if chained:
        st_spec = pl.BlockSpec((1, 1, nstate), seq_map)
        st_shape = jax.ShapeDtypeStruct((nb, 1, nstate), F32)
    else:
        st_spec = pl.BlockSpec((bb, nstate), lambda b, l: (b, 0))
        st_shape = jax.ShapeDtypeStruct((nb, nstate), F32)
    return pl.pallas_call(
        functools.partial(_ssm_body, seglen=seglen, chained=chained),
        grid=grid,
        in_specs=[pl.BlockSpec((ROWS, c), right_half),
                  st_spec, st_spec,
                  full(bbr_bd), full(bbi_bd), full(cr_bd), full(nci_bd),
                  pl.BlockSpec((1, c), const2),
                  full(pw_r), full(pw_i),
                  full(w_glu_bf),
                  pl.BlockSpec((1, c), const2),
                  pl.BlockSpec((1, c), const2),
                  pl.BlockSpec((ROWS, ROWS), const2)],
        out_specs=[pl.BlockSpec((ROWS, c), row_map), st_spec, st_spec],
        out_shape=[jax.ShapeDtypeStruct((t, c), BF16), st_shape, st_shape],
        scratch_shapes=[pltpu.VMEM((ROWS, nstate), F32), pltpu.VMEM((ROWS, nstate), F32),
                        pltpu.VMEM((ROWS, c), F32),
                        pltpu.VMEM((1, nstate), F32), pltpu.VMEM((1, nstate), F32)],
        compiler_params=_cparams("arbitrary", "arbitrary"),
        name="ssm",
    )(z, h0r, h0i, bbr_bd, bbi_bd, cr_bd, nci_bd, d_skip.reshape(1, c), pw_r, pw_i,
      w_glu_bf, b_glu.reshape(1, c), g_ssm.reshape(1, c), unperm)


def _route(s_t, bias_t):
    e, tn = s_t.shape
    per = e // N_EXPERT_GROUPS
    neg = -jnp.inf
    sb = s_t + bias_t
    rowl = lax.broadcasted_iota(I32, (per, tn), 0)
    gscore = []
    for g in range(N_EXPERT_GROUPS):
        blk = sb[g * per:(g + 1) * per]
        m1 = jnp.max(blk, axis=0, keepdims=True)
        i1 = jnp.min(jnp.where(blk == m1, rowl, per), axis=0, keepdims=True)
        m2 = jnp.max(jnp.where(rowl == i1, neg, blk), axis=0, keepdims=True)
        gscore.append(m1 + m2)
    cur = jnp.concatenate(gscore, axis=0)
    rowg = lax.broadcasted_iota(I32, cur.shape, 0)
    gsel = jnp.zeros(cur.shape, F32)
    for _ in range(TOPK_GROUPS):
        m = jnp.max(cur, axis=0, keepdims=True)
        hit = rowg == jnp.min(jnp.where(cur == m, rowg, N_EXPERT_GROUPS), axis=0, keepdims=True)
        gsel = jnp.where(hit, 1.0, gsel)
        cur = jnp.where(hit, neg, cur)
    cur = jnp.concatenate(
        [jnp.where(jnp.broadcast_to(gsel[g:g + 1], (per, tn)) > 0.0, sb[g * per:(g + 1) * per], neg)
         for g in range(N_EXPERT_GROUPS)], axis=0)
    rowe = lax.broadcasted_iota(I32, (e, tn), 0)
    idxs, vals = [], []
    for _ in range(TOP_K):
        m = jnp.max(cur, axis=0, keepdims=True)
        idx = jnp.min(jnp.where(cur == m, rowe, e), axis=0, keepdims=True)
        hit = rowe == idx
        idxs.append(idx)
        vals.append(jnp.sum(jnp.where(hit, s_t, 0.0), axis=0, keepdims=True))
        cur = jnp.where(hit, neg, cur)
    w = jnp.concatenate(vals, axis=0)
    w = w / jnp.sum(w, axis=0, keepdims=True) * ROUTED_SCALE
    return jnp.concatenate(idxs, axis=0), w


def _out_proj_body(x_ref, ya_ref, ys_ref, g1_ref, sh2_ref, sc2_ref, wo_ref, lg_ref, lb_ref, wrt_ref, rb_ref,
                   x1_ref, h_ref, ei_ref, ew_ref, *, bb, r):
    ca = ya_ref.shape[-1]
    m = (jnp.dot(ya_ref[...], wo_ref[:ca, :], preferred_element_type=F32)
         + jnp.dot(ys_ref[...], wo_ref[ca:, :], preferred_element_type=F32))
    res = DN_ALPHA * x_ref[...] + (1.0 + _rows(g1_ref, bb, r)) * m
    x1 = _ln(res) * lg_ref[...] + lb_ref[...]
    x1_ref[...] = x1
    h = _ln(x1) * (1.0 + _rows(sc2_ref, bb, r)) + _rows(sh2_ref, bb, r)
    h_ref[...] = _pack_pairs(h)
    logit_t = lax.dot_general(wrt_ref[...], h.astype(BF16), (((1,), (1,)), ((), ())),
                              preferred_element_type=F32)
    idx, w = _route(_sigmoid(logit_t), rb_ref[...])
    ei_ref[...] = idx
    ew_ref[...] = w


def _out_proj(x2, ya, ys, g1, sh2, sc2, w_out_bf, ln_g, ln_b, w_router_t_bf, router_bias, nb, seq,
              h_all, t_all, row_off):
    t, d = x2.shape
    ca = ya.shape[-1]
    e = w_router_t_bf.shape[0]
    bb, r, grid, row_map, seq_map = _seq_grid(nb, seq)
    const2 = lambda b, l: (0, 0)
    tok_map = lambda b, l: (0, row_map(b, l)[0])
    blk_off = row_off // ROWS
    h_map = lambda b, l: (row_map(b, l)[0] + blk_off, 0)
    in_specs = [pl.BlockSpec((ROWS, d), row_map),
                pl.BlockSpec((ROWS, ca), row_map),
                pl.BlockSpec((ROWS, ca), row_map),
                pl.BlockSpec((bb, 1, d), seq_map),
                pl.BlockSpec((bb, 1, d), seq_map),
                pl.BlockSpec((bb, 1, d), seq_map),
                pl.BlockSpec((d, d), const2),
                pl.BlockSpec((1, d), const2),
                pl.BlockSpec((1, d), const2),
                pl.BlockSpec((e, d), const2),
                pl.BlockSpec((e, 1), const2)]
    args = [x2, ya, ys, g1, sh2, sc2, w_out_bf, ln_g.reshape(1, d), ln_b.reshape(1, d),
            w_router_t_bf, router_bias.reshape(e, 1)]
    body = functools.partial(_out_proj_body, bb=bb, r=r)
    aliases = {}
    if h_all is not None:
        in_specs.append(pl.BlockSpec(memory_space=pl.ANY))
        args.append(h_all)
        aliases = {len(args) - 1: 1}
        inner = body
        body = lambda *refs: inner(*refs[:11], *refs[12:])
    return pl.pallas_call(
        body,
        grid=grid,
        in_specs=in_specs,
        out_specs=[pl.BlockSpec((ROWS, d), row_map),
                   pl.BlockSpec((ROWS, d // 2), h_map),
                   pl.BlockSpec((TOP_K, ROWS), tok_map),
                   pl.BlockSpec((TOP_K, ROWS), tok_map)],
        out_shape=[jax.ShapeDtypeStruct((t, d), F32),
                   jax.ShapeDtypeStruct((t_all, d // 2), U32),
                   jax.ShapeDtypeStruct((TOP_K, t), I32),
                   jax.ShapeDtypeStruct((TOP_K, t), F32)],
        input_output_aliases=aliases,
        compiler_params=_cparams("arbitrary", "arbitrary"),
        name="out_proj",
    )(*args)


def _experts_body(pb_ref, pe_ref, plo_ref, pfl_ref, pne_ref, pne2_ref, np_ref, tokc_ref, tokn_ref, dstp_ref, dstc_ref,
                  h_hbm, wg_hbm, wu_hbm, wd_hbm, y_hbm, xbuf, ybuf, xlo_s, xhi_s, wg_s, wu_s, wd_s,
                  gsem, ssem, wsem):
    p = pl.program_id(0)
    n_pairs = np_ref[0]
    valid = p < n_pairs
    b = pb_ref[p]
    slot = b % 2
    flags = pfl_ref[p]
    first = (flags & PAIR_FIRST_OF_BLOCK) != 0
    scatters = (flags & PAIR_SCATTERS) != 0
    new_expert = (flags & PAIR_NEW_EXPERT) != 0
    has_next = (flags & PAIR_HAS_NEXT_EXPERT) != 0
    has_next2 = (flags & PAIR_HAS_NEXT2_EXPERT) != 0
    wslot = flags // PAIR_WEIGHT_SLOT
    bm, c = xbuf.shape[1], xbuf.shape[2]
    weights = ((wg_hbm, wg_s), (wu_hbm, wu_s), (wd_hbm, wd_s))

    def weights_start(e, s):
        for w_hbm, w_s in weights:
            pltpu.make_async_copy(w_hbm.at[e], w_s.at[s], wsem.at[s]).start(priority=1)

    def weights_wait(s):
        for w_hbm, w_s in weights:
            pltpu.make_async_copy(w_hbm.at[0], w_s.at[s], wsem.at[s]).wait()

    def gather_start(tok_ref, s):
        for i in range(bm):
            pltpu.make_async_copy(h_hbm.at[pl.ds(tok_ref[0, 0, i], 1), :],
                                  xbuf.at[s, pl.ds(i, 1), :], gsem.at[s]).start()

    def gather_wait(s):
        pltpu.make_async_copy(h_hbm.at[pl.ds(0, bm), :], xbuf.at[s], gsem.at[s]).wait()

    def scatter_start(dst_ref, s):
        for i in range(bm):
            pltpu.make_async_copy(ybuf.at[s, pl.ds(i, 1), :],
                                  y_hbm.at[pl.ds(dst_ref[0, 0, i], 1), :], ssem.at[0]).start(priority=i % 2)

    def scatter_wait():
        pltpu.make_async_copy(ybuf.at[0], y_hbm.at[pl.ds(0, bm), :], ssem.at[0]).wait()

    def expert():
        def proj(w_s):
            return (jnp.dot(xlo_s[...], w_s[wslot, :c, :].astype(BF16), preferred_element_type=F32)
                    + jnp.dot(xhi_s[...], w_s[wslot, c:, :].astype(BF16), preferred_element_type=F32))

        g = proj(wg_s)
        u = proj(wu_s)
        a = (g * _sigmoid(g) * u).astype(BF16)
        return _pack_pairs(jnp.dot(a, wd_s[wslot].astype(BF16), preferred_element_type=F32))

    @pl.when(p == 0)
    def _():
        weights_start(pe_ref[0], 0)

        @pl.when(has_next)
        def _():
            weights_start(pne_ref[0], 1)

        gather_start(tokc_ref, 0)
        ybuf[1] = jnp.zeros(ybuf.shape[1:], ybuf.dtype)

    @pl.when(jnp.logical_and(valid, new_expert))
    def _():
        weights_wait(wslot)

        @pl.when(has_next2)
        def _():
            weights_start(pne2_ref[p], (wslot + 2) % WEIGHT_SLOTS)

    def first_pair(with_scatter):
        @pl.when(b > 0)
        def _():
            scatter_wait()

        gather_wait(slot)
        x_lo, x_hi = _unpack_pairs(xbuf[slot])
        xlo_s[...] = x_lo.astype(BF16)
        xhi_s[...] = x_hi.astype(BF16)
        gather_start(tokn_ref, 1 - slot)
        if with_scatter:
            scatter_start(dstp_ref, 1 - slot)
        ybuf[slot] = expert()

    def later_pair(with_scatter):
        if with_scatter:
            scatter_start(dstp_ref, 1 - slot)
        y = expert()
        mine = lax.broadcasted_iota(I32, y.shape, 0) >= plo_ref[p]
        ybuf[slot] = jnp.where(mine, y, ybuf[slot])

    for is_first, body in ((True, first_pair), (False, later_pair)):
        for with_scatter in (False, True):
            cond = jnp.logical_and(valid, jnp.logical_and(first == is_first, scatters == with_scatter))
            pl.when(cond)(functools.partial(body, with_scatter))

    @pl.when(p == n_pairs - 1)
    def _():
        scatter_wait()
        scatter_start(dstc_ref, slot)
        scatter_wait()
        gather_wait(1 - slot)


def _experts(h_pack, w_gate, w_up, w_down, pairs, tok_blocks, dst_blocks):
    t, c = h_pack.shape
    d = 2 * c
    e, _, de = w_gate.shape
    nblk = tok_blocks.shape[0]
    bm = MOE_BM
    pb, pe, plo, pfl, pne, pne2, n_pairs = pairs
    ws = WEIGHT_SLOTS
    cur = lambda p, pb, *_: (pb[p], 0, 0)
    nxt = lambda p, pb, *_: (jnp.minimum(pb[p] + 1, nblk - 1), 0, 0)
    prv = lambda p, pb, *_: (jnp.maximum(pb[p] - 1, 0), 0, 0)
    smem_blk = lambda m: pl.BlockSpec((1, 1, bm), m, memory_space=pltpu.SMEM)
    hbm = pl.BlockSpec(memory_space=pl.ANY)
    grid_spec = pltpu.PrefetchScalarGridSpec(
        num_scalar_prefetch=7,
        grid=(pb.shape[0],),
        in_specs=[smem_blk(cur), smem_blk(nxt), smem_blk(prv), smem_blk(cur), hbm, hbm, hbm, hbm],
        out_specs=hbm,
        scratch_shapes=[pltpu.VMEM((2, bm, c), U32), pltpu.VMEM((2, bm, c), U32),
                        pltpu.VMEM((bm, c), BF16), pltpu.VMEM((bm, c), BF16),
                        pltpu.VMEM((ws, d, de), F32), pltpu.VMEM((ws, d, de), F32), pltpu.VMEM((ws, de, d), F32),
                        pltpu.SemaphoreType.DMA((2,)), pltpu.SemaphoreType.DMA((1,)),
                        pltpu.SemaphoreType.DMA((ws,))],
    )
    return pl.pallas_call(
        _experts_body,
        grid_spec=grid_spec,
        out_shape=jax.ShapeDtypeStruct((nblk * bm, c), U32),
        compiler_params=_cparams("arbitrary"),
        name="experts",
    )(pb, pe, plo, pfl, pne, pne2, n_pairs, tok_blocks, tok_blocks, dst_blocks, dst_blocks,
      h_pack, w_gate, w_up, w_down)


def _dispatch(eidx_t, n_experts):
    k, t = eidx_t.shape
    a = t * k
    bm = MOE_BM
    nb = a // bm
    flat_e = eidx_t.reshape(a)
    id_bits = (a - 1).bit_length()
    assert id_bits + (n_experts - 1).bit_length() < 32
    keyed = lax.sort(flat_e * (1 << id_bits) + lax.iota(I32, a))
    se, order = keyed >> id_bits, keyed & ((1 << id_bits) - 1)
    tok = order % t
    ex = lax.iota(I32, n_experts)
    starts = jnp.sum((se[None, :] < ex[:, None]).astype(I32), axis=1)
    seb = se.reshape(nb, bm)
    e_lo, e_hi = seb[:, 0], seb[:, bm - 1]
    npair = e_hi - e_lo + 1
    cum = jnp.cumsum(npair)
    off = cum - npair
    p = lax.iota(I32, nb + n_experts)
    pb = jnp.minimum(jnp.sum((cum[None, :] <= p[:, None]).astype(I32), axis=1), nb - 1)
    pe = jnp.minimum(e_lo[pb] + p - off[pb], e_hi[pb])
    plo = jnp.clip(starts[pe] - pb * bm, 0, bm)
    valid = p < cum[-1]
    new_e = jnp.logical_and(valid, jnp.concatenate([jnp.ones((1,), bool), pe[1:] != pe[:-1]]))
    wslot = (jnp.cumsum(new_e.astype(I32)) - 1) % WEIGHT_SLOTS

    def next_used(e):
        later = jnp.logical_and(pe[None, :] > e[:, None], valid[None, :])
        return jnp.min(jnp.where(later, pe[None, :], n_experts), axis=1)

    pne = next_used(pe)
    pne2 = next_used(pne)
    j = p - off[pb]
    scatters = jnp.logical_or(j == 1, jnp.logical_and(j == 0, npair[pb] == 1))
    pfl = (PAIR_FIRST_OF_BLOCK * (j == 0) + PAIR_NEW_EXPERT * new_e + PAIR_HAS_NEXT_EXPERT * (pne < n_experts)
           + PAIR_HAS_NEXT2_EXPERT * (pne2 < n_experts) + PAIR_WEIGHT_SLOT * wslot
           + PAIR_SCATTERS * scatters).astype(I32)
    pne, pne2 = jnp.minimum(pne, n_experts - 1), jnp.minimum(pne2, n_experts - 1)
    pairs = (pb.astype(I32), pe.astype(I32), plo.astype(I32), pfl, pne.astype(I32), pne2.astype(I32),
             cum[-1:].astype(I32))
    return pairs, tok.reshape(nb, 1, bm), order.reshape(nb, 1, bm)


def _combine_body(x1_ref, h_ref, w_ref, g2_ref, wsg_ref, wsu_ref, wsd_ref, lg_ref, lb_ref, *rest, bb, r):
    y_refs, o_ref = rest[:TOP_K], rest[TOP_K]
    c = h_ref.shape[-1]
    h_lo, h_hi = _unpack_pairs(h_ref[...])
    h_lo, h_hi = h_lo.astype(BF16), h_hi.astype(BF16)

    def proj(w_ref):
        return (jnp.dot(h_lo, w_ref[:c, :], preferred_element_type=F32)
                + jnp.dot(h_hi, w_ref[c:, :], preferred_element_type=F32))

    g = proj(wsg_ref)
    u = proj(wsu_ref)
    shared = jnp.dot((g * _sigmoid(g) * u).astype(BF16), wsd_ref[...], preferred_element_type=F32)
    w = w_ref[...]
    r_lo = r_hi = None
    for k in range(TOP_K):
        y_lo, y_hi = _unpack_pairs(y_refs[k][...])
        wk = w[:, k:k + 1]
        r_lo = wk * y_lo if r_lo is None else r_lo + wk * y_lo
        r_hi = wk * y_hi if r_hi is None else r_hi + wk * y_hi
    routed = jnp.concatenate([r_lo, r_hi], axis=-1)
    res = DN_ALPHA * x1_ref[...] + (1.0 + _rows(g2_ref, bb, r)) * (routed + shared)
    o_ref[...] = _ln(res) * lg_ref[...] + lb_ref[...]


def _combine(x1, h_pack, y_pack, wsel, g2, wsg_bf, wsu_bf, wsd_bf, ln_g, ln_b, nb, seq, row_off):
    t, d = x1.shape
    t_all, c = h_pack.shape
    ds_ = wsg_bf.shape[1]
    rows = COMB_ROWS
    if seq >= rows:
        bb, r, nl = 1, rows, seq // rows
        grid = (nb, nl)
        row_map = lambda b, l: (b * nl + l, 0)
    else:
        bb, r, nl = rows // seq, seq, 1
        grid = (nb // bb, 1)
        row_map = lambda b, l: (b, 0)
    seq_map = lambda b, l: (b, 0, 0)
    off = row_off // rows
    all_map = lambda b, l: (row_map(b, l)[0] + off, 0)
    const2 = lambda b, l: (0, 0)

    def slot_map(k):
        return lambda b, l: (row_map(b, l)[0] + off + k * (t_all // rows), 0)

    return pl.pallas_call(
        functools.partial(_combine_body, bb=bb, r=r),
        grid=grid,
        in_specs=[pl.BlockSpec((rows, d), row_map),
                  pl.BlockSpec((rows, c), all_map),
                  pl.BlockSpec((rows, TOP_K), all_map),
                  pl.BlockSpec((bb, 1, d), seq_map),
                  pl.BlockSpec((d, ds_), const2),
                  pl.BlockSpec((d, ds_), const2),
                  pl.BlockSpec((ds_, d), const2),
                  pl.BlockSpec((1, d), const2),
                  pl.BlockSpec((1, d), const2)]
                 + [pl.BlockSpec((rows, c), slot_map(k)) for k in range(TOP_K)],
        out_specs=pl.BlockSpec((rows, d), row_map),
        out_shape=jax.ShapeDtypeStruct((t, d), F32),
        compiler_params=_cparams("arbitrary", "arbitrary"),
        name="combine",
    )(x1, h_pack, wsel, g2, wsg_bf, wsu_bf, wsd_bf, ln_g.reshape(1, d), ln_b.reshape(1, d),
      *([y_pack] * TOP_K))


def kernel(x_prompt, x_sample, state_pool, state_ssm_re, state_ssm_im, c_prompt, c_sample, w_ada, b_ada, w_in, w_pool, pool_scale, A_re, A_im, log_dt, B_re, B_im, C_re, C_im, D_skip, w_glu, b_glu, g_pool, g_ssm, w_out, ln1_g, ln1_b, w_router, router_bias, w_e_gate, w_e_up, w_e_down, w_sh_gate, w_sh_up, w_sh_down, ln2_g, ln2_b):
    bp, lp, d = x_prompt.shape
    bs, ls, _ = x_sample.shape
    depth = w_ada.shape[0]
    assert depth == DEPTH == 1
    tp, ts = bp * lp, bs * ls
    t_all = tp + ts
    l = 0
    c_pool = pool_scale.shape[-1]
    n_groups, n_state = A_re.shape[1], A_re.shape[2]
    nstate = n_groups * n_state
    n_experts = w_router.shape[-1]
    gpt = SCAN_W // n_state

    c_all = jnp.concatenate([c_prompt, c_sample], axis=0)
    pad = (-c_all.shape[0]) % SUBLANES
    c_all = jnp.pad(c_all, ((0, pad), (0, 0)))
    mod = _adaln(c_all, w_ada[l], b_ada[l]).reshape(c_all.shape[0], 6, 1, d)
    mod_p = [mod[:bp, i] for i in range(6)]
    mod_s = [mod[bp:bp + bs, i] for i in range(6)]

    w_in_bf = w_in[l].astype(BF16)
    w_pool_bf = w_pool[l].astype(BF16)
    w_glu_bf = w_glu[l].astype(BF16)
    w_out_bf = w_out[l].astype(BF16)
    w_router_t_bf = w_router[l].T.astype(BF16)
    wsg_bf, wsu_bf, wsd_bf = w_sh_gate[l].astype(BF16), w_sh_up[l].astype(BF16), w_sh_down[l].astype(BF16)
    pw_r, pw_i, bb_r, bb_i = _ssm_prep(A_re[l], A_im[l], log_dt[l], B_re[l], B_im[l], ROWS // SUBLANES)
    bbr_bd = _block_diag(jnp.swapaxes(bb_r, 1, 2), gpt).astype(BF16)
    bbi_bd = _block_diag(jnp.swapaxes(bb_i, 1, 2), gpt).astype(BF16)
    cr_bd = _block_diag(jnp.swapaxes(C_re[l], 1, 2), gpt).astype(BF16)
    nci_bd = _block_diag(jnp.swapaxes(-C_im[l], 1, 2), gpt).astype(BF16)
    d_skip = D_skip[l].reshape(-1)

    groups = [
        dict(x=x_prompt.reshape(tp, d), nb=bp, seq=lp, mod=mod_p, start=0, row_off=0,
             prefix=jnp.zeros((bp, POOL_HALO, c_pool), F32),
             h0r=jnp.zeros((bp, 1, nstate), F32), h0i=jnp.zeros((bp, 1, nstate), F32)),
        dict(x=x_sample.reshape(ts, d), nb=bs, seq=ls, mod=mod_s, start=PAST_LEN, row_off=tp,
             prefix=jnp.pad(state_pool[l], ((0, 0), (POOL_HALO - state_pool.shape[2], 0), (0, 0))),
             h0r=state_ssm_re[l].reshape(bs, nstate), h0i=state_ssm_im[l].reshape(bs, nstate)),
    ]

    h_all = None
    for gr in groups:
        sh1, sc1, g1, sh2, sc2, g2 = gr['mod']
        nb, seq = gr['nb'], gr['seq']
        perm = _scan_perm(ROWS // SUBLANES if seq >= ROWS else seq)
        z = _mix_in(gr['x'], sh1, sc1, w_in_bf, perm, nb, seq)
        ya = _pool(z, gr['prefix'], w_pool_bf, pool_scale[l], g_pool[l], nb, seq, gr['start'])
        ys, hfr, hfi = _ssm(z, gr['h0r'], gr['h0i'], bbr_bd, bbi_bd, cr_bd, nci_bd, d_skip, pw_r, pw_i,
                            w_glu_bf, b_glu[l], g_ssm[l], perm.T, nb, seq)
        x1, h_all, eidx_t, ew_t = _out_proj(gr['x'], ya, ys, g1, sh2, sc2, w_out_bf, ln1_g[l], ln1_b[l],
                                            w_router_t_bf, router_bias[l], nb, seq,
                                            h_all, t_all, gr['row_off'])
        gr.update(z=z, x1=x1, eidx_t=eidx_t, ew_t=ew_t, hfr=hfr, hfi=hfi)

    eidx_t = jnp.concatenate([gr['eidx_t'] for gr in groups], axis=1)
    wsel = jnp.concatenate([gr['ew_t'] for gr in groups], axis=1).T
    pairs, tok_blocks, dst_blocks = _dispatch(eidx_t, n_experts)
    y3 = _experts(h_all, w_e_gate[l], w_e_up[l], w_e_down[l], pairs, tok_blocks, dst_blocks)

    outs = []
    for gr in groups:
        outs.append(_combine(gr['x1'], h_all, y3, wsel, gr['mod'][5], wsg_bf, wsu_bf, wsd_bf,
                             ln2_g[l], ln2_b[l], gr['nb'], gr['seq'], gr['row_off']))
    y_prompt = outs[0].reshape(bp, lp, d)
    y_sample = outs[1].reshape(bs, ls, d)

    nbuf = state_pool.shape[2]
    zp = groups[0]['z'].reshape(bp, lp, d)[:, :, :c_pool]
    zs = groups[1]['z'].reshape(bs, ls, d)[:, :, :c_pool]
    pool_p = zp[:, lp - nbuf:, :][None]
    pool_s = jnp.concatenate([state_pool[l], zs], axis=1)[:, -nbuf:, :][None]
    st = lambda a, nb: a.reshape(nb, n_groups, n_state)[None]
    return (y_prompt, y_sample, pool_p, pool_s,
            st(groups[0]['hfr'], bp), st(groups[0]['hfi'], bp),
            st(groups[1]['hfr'], bs), st(groups[1]['hfi'], bs))
```

```python
import functools

import jax
import jax.numpy as jnp
from jax import lax
from jax.experimental import pallas as pl
from jax.experimental.pallas import tpu as pltpu

F32 = jnp.float32
BF16 = jnp.bfloat16
I32 = jnp.int32
U32 = jnp.uint32

DEPTH = 1
PAST_LEN = 16384
POOL_WINDOWS = (2, 4, 8, 16)
POOL_HALO = 16
SSM_P = 16
SSM_N = 64
N_EXPERT_GROUPS = 8
TOPK_GROUPS = 4
TOP_K = 8
ROUTED_SCALE = 2.5
LN_EPS = 1e-5
DN_ALPHA = (2.0 * DEPTH) ** 0.25

ROWS = 256
SUBLANES = 8
LANES = 128
SCAN_W = 512
MOE_BM = 256
PAIR_FIRST_OF_BLOCK, PAIR_NEW_EXPERT, PAIR_HAS_NEXT_EXPERT, PAIR_HAS_NEXT2_EXPERT, PAIR_SCATTERS = 1, 2, 4, 8, 16
PAIR_WEIGHT_SLOT = 32
WEIGHT_SLOTS = 3
GATHER_SLOTS = 3
COMB_ROWS = 256
VMEM_LIMIT = 56 * 1024 * 1024


def _cparams(*sem):
    return pltpu.CompilerParams(dimension_semantics=sem, vmem_limit_bytes=VMEM_LIMIT)


def _ln(x):
    xc = x - jnp.mean(x, axis=-1, keepdims=True)
    return xc * lax.rsqrt(jnp.mean(xc * xc, axis=-1, keepdims=True) + LN_EPS)


def _rows(m_ref, bb, r):
    m = m_ref[...]
    c = m.shape[-1]
    return jnp.broadcast_to(m, (bb, r, c)).reshape(bb * r, c)


def _sigmoid(x):
    return 1.0 / (1.0 + jnp.exp(-x))


def _bf16_bits(x):
    return lax.bitcast_convert_type(x.astype(BF16).astype(F32), U32)


def _pack_pairs(x):
    c = x.shape[-1] // 2
    return (_bf16_bits(x[:, :c]) >> 16) | _bf16_bits(x[:, c:])


def _unpack_pairs(w):
    return (lax.bitcast_convert_type(w << 16, F32),
            lax.bitcast_convert_type(w & jnp.uint32(0xFFFF0000), F32))


def _adaln_body(c_ref, w_ref, b_ref, o_ref):
    c = c_ref[...]
    s = (c * _sigmoid(c)).astype(BF16)
    o_ref[...] = jnp.dot(s, w_ref[...].astype(BF16), preferred_element_type=F32) + b_ref[...]


def _adaln(c_all, w_ada, b_ada):
    bc, d = c_all.shape
    n = w_ada.shape[1]
    tn = 1024
    return pl.pallas_call(
        _adaln_body,
        grid=(n // tn,),
        in_specs=[pl.BlockSpec((bc, d), lambda j: (0, 0)),
                  pl.BlockSpec((d, tn), lambda j: (0, j)),
                  pl.BlockSpec((1, tn), lambda j: (0, j))],
        out_specs=pl.BlockSpec((bc, tn), lambda j: (0, j)),
        out_shape=jax.ShapeDtypeStruct((bc, n), F32),
        compiler_params=_cparams("arbitrary"),
        name="adaln",
    )(c_all, w_ada, b_ada.reshape(1, n))


def _ssm_prep_body(ar_ref, ai_ref, dt_ref, ar16_ref, ai16_ref, dt16_ref, br_ref, bi_ref,
                   pr_ref, pi_ref, bbr_ref, bbi_ref):
    def zoh(a_r, a_i, dt):
        mag = jnp.exp(a_r * dt)
        ab_r, ab_i = mag * jnp.cos(a_i * dt), mag * jnp.sin(a_i * dt)
        den = a_r * a_r + a_i * a_i
        nr = ab_r - 1.0
        return ab_r, ab_i, (nr * a_r + ab_i * a_i) / den, (ab_i * a_r - nr * a_i) / den

    ab_r, ab_i, _, _ = zoh(ar_ref[...], ai_ref[...], jnp.exp(dt_ref[...]))
    p_r, p_i = ab_r, ab_i
    for k in range(pr_ref.shape[0]):
        pr_ref[k] = p_r
        pi_ref[k] = p_i
        p_r, p_i = p_r * ab_r - p_i * ab_i, p_r * ab_i + p_i * ab_r
    _, _, f_r, f_i = zoh(ar16_ref[...], ai16_ref[...], jnp.exp(dt16_ref[...]))
    b_r, b_i = br_ref[...], bi_ref[...]
    bbr_ref[...] = f_r * b_r - f_i * b_i
    bbi_ref[...] = f_r * b_i + f_i * b_r


def _ssm_prep(a_re, a_im, log_dt, b_re, b_im, npow):
    g, n = a_re.shape
    p = b_re.shape[-1]
    dt = jnp.broadcast_to(log_dt[:, None], (g, n))
    rep = lambda a: jnp.repeat(a, p, axis=-1)
    outs = pl.pallas_call(
        _ssm_prep_body,
        out_shape=(jax.ShapeDtypeStruct((npow, g, n), F32), jax.ShapeDtypeStruct((npow, g, n), F32),
                   jax.ShapeDtypeStruct((g, n * p), F32), jax.ShapeDtypeStruct((g, n * p), F32)),
        name="ssm_prep",
    )(a_re, a_im, dt, rep(a_re), rep(a_im), rep(dt), b_re.reshape(g, n * p), b_im.reshape(g, n * p))
    pw_r, pw_i, bb_r, bb_i = outs
    return (pw_r.reshape(npow, g * n), pw_i.reshape(npow, g * n),
            bb_r.reshape(g, n, p), bb_i.reshape(g, n, p))


def _block_diag(w, gpt):
    g, a, b = w.shape
    w4 = w.reshape(g // gpt, gpt, a, b)
    eye = jnp.eye(gpt, dtype=w.dtype)
    return jnp.einsum('jgab,gh->jgahb', w4, eye).reshape(g // gpt, gpt * a, gpt * b)


def _scan_perm(seglen):
    new = jnp.arange(ROWS)
    grp, rem = new // (SUBLANES * seglen), new % (SUBLANES * seglen)
    old = grp * (SUBLANES * seglen) + (rem % SUBLANES) * seglen + rem // SUBLANES
    return (old[:, None] == jnp.arange(ROWS)[None, :]).astype(BF16)


def _mix_in_body(x_ref, sh_ref, sc_ref, w_ref, perm_ref, z_ref, *, bb, r):
    c = z_ref.shape[-1] // 2
    u = (_ln(x_ref[...]) * (1.0 + _rows(sc_ref, bb, r)) + _rows(sh_ref, bb, r)).astype(BF16)
    z_ref[:, :c] = jnp.dot(u, w_ref[:, :c], preferred_element_type=F32)
    up = jnp.dot(perm_ref[...], u, preferred_element_type=F32).astype(BF16)
    z_ref[:, c:] = jnp.dot(up, w_ref[:, c:], preferred_element_type=F32)


def _seq_grid(nb, seq):
    if seq >= ROWS:
        bb, r, nl = 1, ROWS, seq // ROWS
        grid = (nb, nl)
        row_map = lambda b, l: (b * nl + l, 0)
    else:
        bb, r, nl = ROWS // seq, seq, 1
        grid = (nb // bb, 1)
        row_map = lambda b, l: (b, 0)
    seq_map = lambda b, l: (b, 0, 0)
    return bb, r, grid, row_map, seq_map


def _mix_in(x2, sh, sc, w_in_bf, perm, nb, seq):
    t, d = x2.shape
    bb, r, grid, row_map, seq_map = _seq_grid(nb, seq)
    const2 = lambda b, l: (0, 0)
    return pl.pallas_call(
        functools.partial(_mix_in_body, bb=bb, r=r),
        grid=grid,
        in_specs=[pl.BlockSpec((ROWS, d), row_map),
                  pl.BlockSpec((bb, 1, d), seq_map),
                  pl.BlockSpec((bb, 1, d), seq_map),
                  pl.BlockSpec((d, d), const2),
                  pl.BlockSpec((ROWS, ROWS), const2)],
        out_specs=pl.BlockSpec((ROWS, d), row_map),
        out_shape=jax.ShapeDtypeStruct((t, d), F32),
        compiler_params=_cparams("arbitrary", "arbitrary"),
        name="mix_in",
    )(x2, sh, sc, w_in_bf, perm)


def _pool_body(z_ref, pre_ref, wp_ref, ps_ref, gp_ref, o_ref, carry_ref, *, bb, r, start_pos):
    li = pl.program_id(1)
    c = z_ref.shape[-1]
    gw = c // len(POOL_WINDOWS)
    rp = POOL_HALO + r

    @pl.when(li == 0)
    def _():
        carry_ref[...] = pre_ref[...]

    za = z_ref[...].reshape(bb, r, c)
    xp3 = jnp.concatenate([carry_ref[...], za], axis=1)
    carry_ref[...] = xp3[:, r:, :]
    xp = xp3.reshape(bb * rp, c)
    pos1 = lax.broadcasted_iota(I32, (bb, r, gw), 1) + (start_pos + 1) + li * r
    outs = []
    ssq = jnp.zeros((bb * r, 1), F32)
    for gi, w in enumerate(POOL_WINDOWS):
        cols = slice(gi * gw, (gi + 1) * gw)
        s = xp[:, cols]
        sh = 1
        while sh < w:
            s = s + pltpu.roll(s, sh, 0)
            sh *= 2
        win = s.reshape(bb, rp, gw)[:, POOL_HALO:, :]
        cnt = jnp.minimum(pos1, w).astype(F32)
        d = (win / cnt - za[:, :, cols]).reshape(bb * r, gw)
        y = jnp.dot(d.astype(BF16), wp_ref[gi], preferred_element_type=F32) * ps_ref[:, cols]
        ssq = ssq + jnp.sum(y * y, axis=-1, keepdims=True)
        outs.append(y)
    scale = lax.rsqrt(ssq * (1.0 / c) + LN_EPS)
    for gi, y in enumerate(outs):
        cols = slice(gi * gw, (gi + 1) * gw)
        o_ref[:, cols] = (y * scale * gp_ref[:, cols]).astype(o_ref.dtype)


def _pool(z, prefix16, w_pool_bf, pool_scale, g_pool, nb, seq, start_pos):
    t = z.shape[0]
    c = pool_scale.shape[-1]
    bb, r, grid, row_map, seq_map = _seq_grid(nb, seq)
    const2 = lambda b, l: (0, 0)
    return pl.pallas_call(
        functools.partial(_pool_body, bb=bb, r=r, start_pos=start_pos),
        grid=grid,
        in_specs=[pl.BlockSpec((ROWS, c), row_map),
                  pl.BlockSpec((bb, POOL_HALO, c), seq_map),
                  pl.BlockSpec(w_pool_bf.shape, lambda b, l: (0, 0, 0)),
                  pl.BlockSpec((1, c), const2),
                  pl.BlockSpec((1, c), const2)],
        out_specs=pl.BlockSpec((ROWS, c), row_map),
        out_shape=jax.ShapeDtypeStruct((t, c), BF16),
        scratch_shapes=[pltpu.VMEM((bb, POOL_HALO, c), F32)],
        compiler_params=_cparams("arbitrary", "arbitrary"),
        name="pool",
    )(z, prefix16, w_pool_bf, pool_scale.reshape(1, c), g_pool.reshape(1, c))


def _cmul_add(x_r, x_i, m_r, m_i, y_r, y_i):
    return x_r + m_r * y_r - m_i * y_i, x_i + m_r * y_i + m_i * y_r


def _ssm_body(z_ref, h0r_ref, h0i_ref, bbr_ref, bbi_ref, cr_ref, nci_ref, dsk_ref, pr_ref, pi_ref,
              wg_ref, bg_ref, gs_ref, unperm_ref, o_ref, hfr_ref, hfi_ref, hr_s, hi_s, y_s, car_s, cai_s,
              *, seglen, chained):
    li = pl.program_id(1)
    c = z_ref.shape[-1]
    nstate = hr_s.shape[-1]
    ntile = nstate // SCAN_W
    cw = c // ntile
    u = z_ref[...]
    ub = u.astype(BF16)
    for j in range(ntile):
        sl = slice(j * SCAN_W, (j + 1) * SCAN_W)
        uj = ub[:, j * cw:(j + 1) * cw]
        hr_s[:, sl] = jnp.dot(uj, bbr_ref[j], preferred_element_type=F32)
        hi_s[:, sl] = jnp.dot(uj, bbi_ref[j], preferred_element_type=F32)

    if chained:
        @pl.when(li == 0)
        def _():
            car_s[...] = h0r_ref[0]
            cai_s[...] = h0i_ref[0]

    grp_rows = SUBLANES * seglen
    bc8 = lambda v: jnp.broadcast_to(v, (SUBLANES, SCAN_W))
    for j in range(ntile):
        sl = slice(j * SCAN_W, (j + 1) * SCAN_W)
        a_r, a_i = bc8(pr_ref[0:1, sl]), bc8(pi_ref[0:1, sl])

        def step(t, h, base, sl=sl, a_r=a_r, a_i=a_i):
            off = pl.multiple_of(base + t * SUBLANES, SUBLANES)
            h_r, h_i = _cmul_add(hr_s[pl.ds(off, SUBLANES), sl], hi_s[pl.ds(off, SUBLANES), sl],
                                 a_r, a_i, h[0], h[1])
            hr_s[pl.ds(off, SUBLANES), sl] = h_r
            hi_s[pl.ds(off, SUBLANES), sl] = h_i
            return h_r, h_i

        if chained:
            zero = jnp.zeros((SUBLANES, SCAN_W), F32)
            e_r, e_i = lax.fori_loop(0, seglen, functools.partial(step, base=0), (zero, zero), unroll=4)
            al_r, al_i = pr_ref[seglen - 1:seglen, sl], pi_ref[seglen - 1:seglen, sl]
            s_r, s_i = car_s[:, sl], cai_s[:, sl]
            ent_r, ent_i = [s_r], [s_i]
            for i in range(SUBLANES):
                s_r, s_i = _cmul_add(e_r[i:i + 1], e_i[i:i + 1], al_r, al_i, s_r, s_i)
                if i + 1 < SUBLANES:
                    ent_r.append(s_r)
                    ent_i.append(s_i)
            car_s[:, sl] = s_r
            cai_s[:, sl] = s_i
            hfr_ref[0, :, sl] = s_r
            hfi_ref[0, :, sl] = s_i
            ent_r, ent_i = jnp.concatenate(ent_r, axis=0), jnp.concatenate(ent_i, axis=0)

            def fix(t, _, sl=sl, ent_r=ent_r, ent_i=ent_i):
                off = pl.multiple_of(t * SUBLANES, SUBLANES)
                x_r, x_i = _cmul_add(hr_s[pl.ds(off, SUBLANES), sl], hi_s[pl.ds(off, SUBLANES), sl],
                                     bc8(pr_ref[pl.ds(t, 1), sl]), bc8(pi_ref[pl.ds(t, 1), sl]), ent_r, ent_i)
                hr_s[pl.ds(off, SUBLANES), sl] = x_r
                hi_s[pl.ds(off, SUBLANES), sl] = x_i
                return 0

            lax.fori_loop(0, seglen, fix, 0, unroll=4)
        else:
            for g in range(z_ref.shape[0] // grp_rows):
                rs = slice(g * SUBLANES, (g + 1) * SUBLANES)
                e_r, e_i = lax.fori_loop(0, seglen, functools.partial(step, base=g * grp_rows),
                                         (h0r_ref[rs, sl], h0i_ref[rs, sl]), unroll=True)
                hfr_ref[rs, sl] = e_r
                hfi_ref[rs, sl] = e_i

    for j in range(ntile):
        sl = slice(j * SCAN_W, (j + 1) * SCAN_W)
        cs = slice(j * cw, (j + 1) * cw)
        y_s[:, cs] = (jnp.dot(hr_s[:, sl].astype(BF16), cr_ref[j], preferred_element_type=F32)
                      + jnp.dot(hi_s[:, sl].astype(BF16), nci_ref[j], preferred_element_type=F32)
                      + dsk_ref[:, cs] * u[:, cs])
    y = y_s[...]
    g = 0.5 * y * (1.0 + jnp.tanh(0.7978845608028654 * (y + 0.044715 * (y * y * y))))
    gate = jnp.dot(g.astype(BF16), wg_ref[...], preferred_element_type=F32) + bg_ref[...]
    out = g * _sigmoid(gate)
    scale = lax.rsqrt(jnp.mean(out * out, axis=-1, keepdims=True) + LN_EPS)
    outp = (out * scale * gs_ref[...]).astype(BF16)
    o_ref[...] = jnp.dot(unperm_ref[...], outp, preferred_element_type=F32).astype(o_ref.dtype)


def _ssm(z, h0r, h0i, bbr_bd, bbi_bd, cr_bd, nci_bd, d_skip, pw_r, pw_i, w_glu_bf, b_glu, g_ssm, unperm,
         nb, seq):
    t = z.shape[0]
    c = d_skip.shape[-1]
    nstate = pw_r.shape[-1]
    bb, r, grid, row_map, seq_map = _seq_grid(nb, seq)
    chained = bb == 1
    seglen = r // SUBLANES if chained else r
    const2 = lambda b, l: (0, 0)
    const3 = lambda b, l: (0, 0, 0)
    full = lambda a: pl.BlockSpec(a.shape, const3 if a.ndim == 3 else const2)
    right_half = (lambda b, l: (row_map(b, l)[0], 1))
    if chained:
        st_spec = pl.BlockSpec((1, 1, nstate), seq_map)
        st_shape = jax.ShapeDtypeStruct((nb, 1, nstate), F32)
    else:
        st_spec = pl.BlockSpec((bb, nstate), lambda b, l: (b, 0))
        st_shape = jax.ShapeDtypeStruct((nb, nstate), F32)
    return pl.pallas_call(
        functools.partial(_ssm_body, seglen=seglen, chained=chained),
        grid=grid,
        in_specs=[pl.BlockSpec((ROWS, c), right_half),
                  st_spec, st_spec,
                  full(bbr_bd), full(bbi_bd), full(cr_bd), full(nci_bd),
                  pl.BlockSpec((1, c), const2),
                  full(pw_r), full(pw_i),
                  full(w_glu_bf),
                  pl.BlockSpec((1, c), const2),
                  pl.BlockSpec((1, c), const2),
                  pl.BlockSpec((ROWS, ROWS), const2)],
        out_specs=[pl.BlockSpec((ROWS, c), row_map), st_spec, st_spec],
        out_shape=[jax.ShapeDtypeStruct((t, c), BF16), st_shape, st_shape],
        scratch_shapes=[pltpu.VMEM((ROWS, nstate), F32), pltpu.VMEM((ROWS, nstate), F32),
                        pltpu.VMEM((ROWS, c), F32),
                        pltpu.VMEM((1, nstate), F32), pltpu.VMEM((1, nstate), F32)],
        compiler_params=_cparams("arbitrary", "arbitrary"),
        name="ssm",
    )(z, h0r, h0i, bbr_bd, bbi_bd, cr_bd, nci_bd, d_skip.reshape(1, c), pw_r, pw_i,
      w_glu_bf, b_glu.reshape(1, c), g_ssm.reshape(1, c), unperm)


def _route(s_t, bias_t):
    e, tn = s_t.shape
    per = e // N_EXPERT_GROUPS
    neg = -jnp.inf
    sb = s_t + bias_t
    rowl = lax.broadcasted_iota(I32, (per, tn), 0)
    gscore = []
    for g in range(N_EXPERT_GROUPS):
        blk = sb[g * per:(g + 1) * per]
        m1 = jnp.max(blk, axis=0, keepdims=True)
        i1 = jnp.min(jnp.where(blk == m1, rowl, per), axis=0, keepdims=True)
        m2 = jnp.max(jnp.where(rowl == i1, neg, blk), axis=0, keepdims=True)
        gscore.append(m1 + m2)
    cur = jnp.concatenate(gscore, axis=0)
    rowg = lax.broadcasted_iota(I32, cur.shape, 0)
    gsel = jnp.zeros(cur.shape, F32)
    for _ in range(TOPK_GROUPS):
        m = jnp.max(cur, axis=0, keepdims=True)
        hit = rowg == jnp.min(jnp.where(cur == m, rowg, N_EXPERT_GROUPS), axis=0, keepdims=True)
        gsel = jnp.where(hit, 1.0, gsel)
        cur = jnp.where(hit, neg, cur)
    cur = jnp.concatenate(
        [jnp.where(jnp.broadcast_to(gsel[g:g + 1], (per, tn)) > 0.0, sb[g * per:(g + 1) * per], neg)
         for g in range(N_EXPERT_GROUPS)], axis=0)
    rowe = lax.broadcasted_iota(I32, (e, tn), 0)
    idxs, vals = [], []
    for _ in range(TOP_K):
        m = jnp.max(cur, axis=0, keepdims=True)
        idx = jnp.min(jnp.where(cur == m, rowe, e), axis=0, keepdims=True)
        hit = rowe == idx
        idxs.append(idx)
        vals.append(jnp.sum(jnp.where(hit, s_t, 0.0), axis=0, keepdims=True))
        cur = jnp.where(hit, neg, cur)
    w = jnp.concatenate(vals, axis=0)
    w = w / jnp.sum(w, axis=0, keepdims=True) * ROUTED_SCALE
    return jnp.concatenate(idxs, axis=0), w


def _out_proj_body(x_ref, ya_ref, ys_ref, g1_ref, sh2_ref, sc2_ref, wo_ref, lg_ref, lb_ref, wrt_ref, rb_ref,
                   x1_ref, h_ref, ei_ref, ew_ref, *, bb, r):
    ca = ya_ref.shape[-1]
    m = (jnp.dot(ya_ref[...], wo_ref[:ca, :], preferred_element_type=F32)
         + jnp.dot(ys_ref[...], wo_ref[ca:, :], preferred_element_type=F32))
    res = DN_ALPHA * x_ref[...] + (1.0 + _rows(g1_ref, bb, r)) * m
    x1 = _ln(res) * lg_ref[...] + lb_ref[...]
    x1_ref[...] = x1
    h = _ln(x1) * (1.0 + _rows(sc2_ref, bb, r)) + _rows(sh2_ref, bb, r)
    h_ref[...] = _pack_pairs(h)
    logit_t = lax.dot_general(wrt_ref[...], h.astype(BF16), (((1,), (1,)), ((), ())),
                              preferred_element_type=F32)
    idx, w = _route(_sigmoid(logit_t), rb_ref[...])
    ei_ref[...] = idx
    ew_ref[...] = w


def _out_proj(x2, ya, ys, g1, sh2, sc2, w_out_bf, ln_g, ln_b, w_router_t_bf, router_bias, nb, seq,
              h_all, t_all, row_off):
    t, d = x2.shape
    ca = ya.shape[-1]
    e = w_router_t_bf.shape[0]
    bb, r, grid, row_map, seq_map = _seq_grid(nb, seq)
    const2 = lambda b, l: (0, 0)
    tok_map = lambda b, l: (0, row_map(b, l)[0])
    blk_off = row_off // ROWS
    h_map = lambda b, l: (row_map(b, l)[0] + blk_off, 0)
    in_specs = [pl.BlockSpec((ROWS, d), row_map),
                pl.BlockSpec((ROWS, ca), row_map),
                pl.BlockSpec((ROWS, ca), row_map),
                pl.BlockSpec((bb, 1, d), seq_map),
                pl.BlockSpec((bb, 1, d), seq_map),
                pl.BlockSpec((bb, 1, d), seq_map),
                pl.BlockSpec((d, d), const2),
                pl.BlockSpec((1, d), const2),
                pl.BlockSpec((1, d), const2),
                pl.BlockSpec((e, d), const2),
                pl.BlockSpec((e, 1), const2)]
    args = [x2, ya, ys, g1, sh2, sc2, w_out_bf, ln_g.reshape(1, d), ln_b.reshape(1, d),
            w_router_t_bf, router_bias.reshape(e, 1)]
    body = functools.partial(_out_proj_body, bb=bb, r=r)
    aliases = {}
    if h_all is not None:
        in_specs.append(pl.BlockSpec(memory_space=pl.ANY))
        args.append(h_all)
        aliases = {len(args) - 1: 1}
        inner = body
        body = lambda *refs: inner(*refs[:11], *refs[12:])
    return pl.pallas_call(
        body,
        grid=grid,
        in_specs=in_specs,
        out_specs=[pl.BlockSpec((ROWS, d), row_map),
                   pl.BlockSpec((ROWS, d // 2), h_map),
                   pl.BlockSpec((TOP_K, ROWS), tok_map),
                   pl.BlockSpec((TOP_K, ROWS), tok_map)],
        out_shape=[jax.ShapeDtypeStruct((t, d), F32),
                   jax.ShapeDtypeStruct((t_all, d // 2), U32),
                   jax.ShapeDtypeStruct((TOP_K, t), I32),
                   jax.ShapeDtypeStruct((TOP_K, t), F32)],
        input_output_aliases=aliases,
        compiler_params=_cparams("arbitrary", "arbitrary"),
        name="out_proj",
    )(*args)


def _experts_body(pb_ref, pe_ref, plo_ref, pfl_ref, pne_ref, pne2_ref, np_ref, tokc_ref, tokn_ref, tok2_ref,
                  dstp_ref, dstc_ref,
                  h_hbm, wg_hbm, wu_hbm, wd_hbm, y_hbm, xbuf, ybuf, xlo_s, xhi_s, wg_s, wu_s, wd_s,
                  gsem, ssem, wsem):
    p = pl.program_id(0)
    n_pairs = np_ref[0]
    valid = p < n_pairs
    b = pb_ref[p]
    slot = b % 2
    xslot = b % GATHER_SLOTS
    flags = pfl_ref[p]
    first = (flags & PAIR_FIRST_OF_BLOCK) != 0
    scatters = (flags & PAIR_SCATTERS) != 0
    new_expert = (flags & PAIR_NEW_EXPERT) != 0
    has_next = (flags & PAIR_HAS_NEXT_EXPERT) != 0
    has_next2 = (flags & PAIR_HAS_NEXT2_EXPERT) != 0
    wslot = flags // PAIR_WEIGHT_SLOT
    bm, c = xbuf.shape[1], xbuf.shape[2]
    weights = ((wg_hbm, wg_s), (wu_hbm, wu_s), (wd_hbm, wd_s))

    def weights_start(e, s):
        for w_hbm, w_s in weights:
            pltpu.make_async_copy(w_hbm.at[e], w_s.at[s], wsem.at[s]).start(priority=1)

    def weights_wait(s):
        for w_hbm, w_s in weights:
            pltpu.make_async_copy(w_hbm.at[0], w_s.at[s], wsem.at[s]).wait()

    def gather_start(tok_ref, s):
        for i in range(bm):
            pltpu.make_async_copy(h_hbm.at[pl.ds(tok_ref[0, 0, i], 1), :],
                                  xbuf.at[s, pl.ds(i, 1), :], gsem.at[s]).start()

    def gather_wait(s):
        pltpu.make_async_copy(h_hbm.at[pl.ds(0, bm), :], xbuf.at[s], gsem.at[s]).wait()

    def scatter_start(dst_ref, s):
        for i in range(bm):
            pltpu.make_async_copy(ybuf.at[s, pl.ds(i, 1), :],
                                  y_hbm.at[pl.ds(dst_ref[0, 0, i], 1), :], ssem.at[0]).start(priority=i % 2)

    def scatter_wait():
        pltpu.make_async_copy(ybuf.at[0], y_hbm.at[pl.ds(0, bm), :], ssem.at[0]).wait()

    def expert():
        def proj(w_s):
            return (jnp.dot(xlo_s[...], w_s[wslot, :c, :].astype(BF16), preferred_element_type=F32)
                    + jnp.dot(xhi_s[...], w_s[wslot, c:, :].astype(BF16), preferred_element_type=F32))

        g = proj(wg_s)
        u = proj(wu_s)
        a = (g * _sigmoid(g) * u).astype(BF16)
        return _pack_pairs(jnp.dot(a, wd_s[wslot].astype(BF16), preferred_element_type=F32))

    @pl.when(p == 0)
    def _():
        weights_start(pe_ref[0], 0)

        @pl.when(has_next)
        def _():
            weights_start(pne_ref[0], 1)

        gather_start(tokc_ref, 0)
        gather_start(tokn_ref, 1)
        ybuf[1] = jnp.zeros(ybuf.shape[1:], ybuf.dtype)

    @pl.when(jnp.logical_and(valid, new_expert))
    def _():
        weights_wait(wslot)

        @pl.when(has_next2)
        def _():
            weights_start(pne2_ref[p], (wslot + 2) % WEIGHT_SLOTS)

    def first_pair(with_scatter):
        @pl.when(b > 0)
        def _():
            scatter_wait()

        gather_wait(xslot)
        x_lo, x_hi = _unpack_pairs(xbuf[xslot])
        xlo_s[...] = x_lo.astype(BF16)
        xhi_s[...] = x_hi.astype(BF16)
        gather_start(tok2_ref, (b + 2) % GATHER_SLOTS)
        if with_scatter:
            scatter_start(dstp_ref, 1 - slot)
        ybuf[slot] = expert()

    def later_pair(with_scatter):
        if with_scatter:
            scatter_start(dstp_ref, 1 - slot)
        y = expert()
        mine = lax.broadcasted_iota(I32, y.shape, 0) >= plo_ref[p]
        ybuf[slot] = jnp.where(mine, y, ybuf[slot])

    for is_first, body in ((True, first_pair), (False, later_pair)):
        for with_scatter in (False, True):
            cond = jnp.logical_and(valid, jnp.logical_and(first == is_first, scatters == with_scatter))
            pl.when(cond)(functools.partial(body, with_scatter))

    @pl.when(p == n_pairs - 1)
    def _():
        scatter_wait()
        scatter_start(dstc_ref, slot)
        scatter_wait()
        gather_wait((b + 1) % GATHER_SLOTS)
        gather_wait((b + 2) % GATHER_SLOTS)


def _experts(h_pack, w_gate, w_up, w_down, pairs, tok_blocks, dst_blocks):
    t, c = h_pack.shape
    d = 2 * c
    e, _, de = w_gate.shape
    nblk = tok_blocks.shape[0]
    bm = MOE_BM
    pb, pe, plo, pfl, pne, pne2, n_pairs = pairs
    ws = WEIGHT_SLOTS
    cur = lambda p, pb, *_: (pb[p], 0, 0)
    nxt = lambda p, pb, *_: (jnp.minimum(pb[p] + 1, nblk - 1), 0, 0)
    nx2 = lambda p, pb, *_: (jnp.minimum(pb[p] + 2, nblk - 1), 0, 0)
    prv = lambda p, pb, *_: (jnp.maximum(pb[p] - 1, 0), 0, 0)
    smem_blk = lambda m: pl.BlockSpec((1, 1, bm), m, memory_space=pltpu.SMEM)
    hbm = pl.BlockSpec(memory_space=pl.ANY)
    grid_spec = pltpu.PrefetchScalarGridSpec(
        num_scalar_prefetch=7,
        grid=(pb.shape[0],),
        in_specs=[smem_blk(cur), smem_blk(nxt), smem_blk(nx2), smem_blk(prv), smem_blk(cur), hbm, hbm, hbm, hbm],
        out_specs=hbm,
        scratch_shapes=[pltpu.VMEM((GATHER_SLOTS, bm, c), U32), pltpu.VMEM((2, bm, c), U32),
                        pltpu.VMEM((bm, c), BF16), pltpu.VMEM((bm, c), BF16),
                        pltpu.VMEM((ws, d, de), F32), pltpu.VMEM((ws, d, de), F32), pltpu.VMEM((ws, de, d), F32),
                        pltpu.SemaphoreType.DMA((GATHER_SLOTS,)), pltpu.SemaphoreType.DMA((1,)),
                        pltpu.SemaphoreType.DMA((ws,))],
    )
    return pl.pallas_call(
        _experts_body,
        grid_spec=grid_spec,
        out_shape=jax.ShapeDtypeStruct((nblk * bm, c), U32),
        compiler_params=_cparams("arbitrary"),
        name="experts",
    )(pb, pe, plo, pfl, pne, pne2, n_pairs, tok_blocks, tok_blocks, tok_blocks, dst_blocks, dst_blocks,
      h_pack, w_gate, w_up, w_down)


def _dispatch(eidx_t, n_experts):
    k, t = eidx_t.shape
    a = t * k
    bm = MOE_BM
    nb = a // bm
    flat_e = eidx_t.reshape(a)
    id_bits = (a - 1).bit_length()
    assert id_bits + (n_experts - 1).bit_length() < 32
    keyed = lax.sort(flat_e * (1 << id_bits) + lax.iota(I32, a))
    se, order = keyed >> id_bits, keyed & ((1 << id_bits) - 1)
    tok = order % t
    ex = lax.iota(I32, n_experts)
    starts = jnp.sum((se[None, :] < ex[:, None]).astype(I32), axis=1)
    seb = se.reshape(nb, bm)
    e_lo, e_hi = seb[:, 0], seb[:, bm - 1]
    npair = e_hi - e_lo + 1
    cum = jnp.cumsum(npair)
    off = cum - npair
    p = lax.iota(I32, nb + n_experts)
    pb = jnp.minimum(jnp.sum((cum[None, :] <= p[:, None]).astype(I32), axis=1), nb - 1)
    pe = jnp.minimum(e_lo[pb] + p - off[pb], e_hi[pb])
    plo = jnp.clip(starts[pe] - pb * bm, 0, bm)
    valid = p < cum[-1]
    new_e = jnp.logical_and(valid, jnp.concatenate([jnp.ones((1,), bool), pe[1:] != pe[:-1]]))
    wslot = (jnp.cumsum(new_e.astype(I32)) - 1) % WEIGHT_SLOTS

    def next_used(e):
        later = jnp.logical_and(pe[None, :] > e[:, None], valid[None, :])
        return jnp.min(jnp.where(later, pe[None, :], n_experts), axis=1)

    pne = next_used(pe)
    pne2 = next_used(pne)
    j = p - off[pb]
    scatters = jnp.logical_or(j == 1, jnp.logical_and(j == 0, npair[pb] == 1))
    pfl = (PAIR_FIRST_OF_BLOCK * (j == 0) + PAIR_NEW_EXPERT * new_e + PAIR_HAS_NEXT_EXPERT * (pne < n_experts)
           + PAIR_HAS_NEXT2_EXPERT * (pne2 < n_experts) + PAIR_WEIGHT_SLOT * wslot
           + PAIR_SCATTERS * scatters).astype(I32)
    pne, pne2 = jnp.minimum(pne, n_experts - 1), jnp.minimum(pne2, n_experts - 1)
    pairs = (pb.astype(I32), pe.astype(I32), plo.astype(I32), pfl, pne.astype(I32), pne2.astype(I32),
             cum[-1:].astype(I32))
    return pairs, tok.reshape(nb, 1, bm), order.reshape(nb, 1, bm)


def _combine_body(x1_ref, h_ref, w_ref, g2_ref, wsg_ref, wsu_ref, wsd_ref, lg_ref, lb_ref, *rest, bb, r):
    y_refs, o_ref = rest[:TOP_K], rest[TOP_K]
    c = h_ref.shape[-1]
    h_lo, h_hi = _unpack_pairs(h_ref[...])
    h_lo, h_hi = h_lo.astype(BF16), h_hi.astype(BF16)

    def proj(w_ref):
        return (jnp.dot(h_lo, w_ref[:c, :], preferred_element_type=F32)
                + jnp.dot(h_hi, w_ref[c:, :], preferred_element_type=F32))

    g = proj(wsg_ref)
    u = proj(wsu_ref)
    shared = jnp.dot((g * _sigmoid(g) * u).astype(BF16), wsd_ref[...], preferred_element_type=F32)
    w = w_ref[...]
    r_lo = r_hi = None
    for k in range(TOP_K):
        y_lo, y_hi = _unpack_pairs(y_refs[k][...])
        wk = w[:, k:k + 1]
        r_lo = wk * y_lo if r_lo is None else r_lo + wk * y_lo
        r_hi = wk * y_hi if r_hi is None else r_hi + wk * y_hi
    routed = jnp.concatenate([r_lo, r_hi], axis=-1)
    res = DN_ALPHA * x1_ref[...] + (1.0 + _rows(g2_ref, bb, r)) * (routed + shared)
    o_ref[...] = _ln(res) * lg_ref[...] + lb_ref[...]


def _combine(x1, h_pack, y_pack, wsel, g2, wsg_bf, wsu_bf, wsd_bf, ln_g, ln_b, nb, seq, row_off):
    t, d = x1.shape
    t_all, c = h_pack.shape
    ds_ = wsg_bf.shape[1]
    rows = COMB_ROWS
    if seq >= rows:
        bb, r, nl = 1, rows, seq // rows
        grid = (nb, nl)
        row_map = lambda b, l: (b * nl + l, 0)
    else:
        bb, r, nl = rows // seq, seq, 1
        grid = (nb // bb, 1)
        row_map = lambda b, l: (b, 0)
    seq_map = lambda b, l: (b, 0, 0)
    off = row_off // rows
    all_map = lambda b, l: (row_map(b, l)[0] + off, 0)
    const2 = lambda b, l: (0, 0)

    def slot_map(k):
        return lambda b, l: (row_map(b, l)[0] + off + k * (t_all // rows), 0)

    return pl.pallas_call(
        functools.partial(_combine_body, bb=bb, r=r),
        grid=grid,
        in_specs=[pl.BlockSpec((rows, d), row_map),
                  pl.BlockSpec((rows, c), all_map),
                  pl.BlockSpec((rows, TOP_K), all_map),
                  pl.BlockSpec((bb, 1, d), seq_map),
                  pl.BlockSpec((d, ds_), const2),
                  pl.BlockSpec((d, ds_), const2),
                  pl.BlockSpec((ds_, d), const2),
                  pl.BlockSpec((1, d), const2),
                  pl.BlockSpec((1, d), const2)]
                 + [pl.BlockSpec((rows, c), slot_map(k)) for k in range(TOP_K)],
        out_specs=pl.BlockSpec((rows, d), row_map),
        out_shape=jax.ShapeDtypeStruct((t, d), F32),
        compiler_params=_cparams("arbitrary", "arbitrary"),
        name="combine",
    )(x1, h_pack, wsel, g2, wsg_bf, wsu_bf, wsd_bf, ln_g.reshape(1, d), ln_b.reshape(1, d),
      *([y_pack] * TOP_K))


def kernel(x_prompt, x_sample, state_pool, state_ssm_re, state_ssm_im, c_prompt, c_sample, w_ada, b_ada, w_in, w_pool, pool_scale, A_re, A_im, log_dt, B_re, B_im, C_re, C_im, D_skip, w_glu, b_glu, g_pool, g_ssm, w_out, ln1_g, ln1_b, w_router, router_bias, w_e_gate, w_e_up, w_e_down, w_sh_gate, w_sh_up, w_sh_down, ln2_g, ln2_b):
    bp, lp, d = x_prompt.shape
    bs, ls, _ = x_sample.shape
    depth = w_ada.shape[0]
    assert depth == DEPTH == 1
    tp, ts = bp * lp, bs * ls
    t_all = tp + ts
    l = 0
    c_pool = pool_scale.shape[-1]
    n_groups, n_state = A_re.shape[1], A_re.shape[2]
    nstate = n_groups * n_state
    n_experts = w_router.shape[-1]
    gpt = SCAN_W // n_state

    c_all = jnp.concatenate([c_prompt, c_sample], axis=0)
    pad = (-c_all.shape[0]) % SUBLANES
    c_all = jnp.pad(c_all, ((0, pad), (0, 0)))
    mod = _adaln(c_all, w_ada[l], b_ada[l]).reshape(c_all.shape[0], 6, 1, d)
    mod_p = [mod[:bp, i] for i in range(6)]
    mod_s = [mod[bp:bp + bs, i] for i in range(6)]

    w_in_bf = w_in[l].astype(BF16)
    w_pool_bf = w_pool[l].astype(BF16)
    w_glu_bf = w_glu[l].astype(BF16)
    w_out_bf = w_out[l].astype(BF16)
    w_router_t_bf = w_router[l].T.astype(BF16)
    wsg_bf, wsu_bf, wsd_bf = w_sh_gate[l].astype(BF16), w_sh_up[l].astype(BF16), w_sh_down[l].astype(BF16)
    pw_r, pw_i, bb_r, bb_i = _ssm_prep(A_re[l], A_im[l], log_dt[l], B_re[l], B_im[l], ROWS // SUBLANES)
    bbr_bd = _block_diag(jnp.swapaxes(bb_r, 1, 2), gpt).astype(BF16)
    bbi_bd = _block_diag(jnp.swapaxes(bb_i, 1, 2), gpt).astype(BF16)
    cr_bd = _block_diag(jnp.swapaxes(C_re[l], 1, 2), gpt).astype(BF16)
    nci_bd = _block_diag(jnp.swapaxes(-C_im[l], 1, 2), gpt).astype(BF16)
    d_skip = D_skip[l].reshape(-1)

    groups = [
        dict(x=x_prompt.reshape(tp, d), nb=bp, seq=lp, mod=mod_p, start=0, row_off=0,
             prefix=jnp.zeros((bp, POOL_HALO, c_pool), F32),
             h0r=jnp.zeros((bp, 1, nstate), F32), h0i=jnp.zeros((bp, 1, nstate), F32)),
        dict(x=x_sample.reshape(ts, d), nb=bs, seq=ls, mod=mod_s, start=PAST_LEN, row_off=tp,
             prefix=jnp.pad(state_pool[l], ((0, 0), (POOL_HALO - state_pool.shape[2], 0), (0, 0))),
             h0r=state_ssm_re[l].reshape(bs, nstate), h0i=state_ssm_im[l].reshape(bs, nstate)),
    ]

    h_all = None
    for gr in groups:
        sh1, sc1, g1, sh2, sc2, g2 = gr['mod']
        nb, seq = gr['nb'], gr['seq']
        perm = _scan_perm(ROWS // SUBLANES if seq >= ROWS else seq)
        z = _mix_in(gr['x'], sh1, sc1, w_in_bf, perm, nb, seq)
        ya = _pool(z, gr['prefix'], w_pool_bf, pool_scale[l], g_pool[l], nb, seq, gr['start'])
        ys, hfr, hfi = _ssm(z, gr['h0r'], gr['h0i'], bbr_bd, bbi_bd, cr_bd, nci_bd, d_skip, pw_r, pw_i,
                            w_glu_bf, b_glu[l], g_ssm[l], perm.T, nb, seq)
        x1, h_all, eidx_t, ew_t = _out_proj(gr['x'], ya, ys, g1, sh2, sc2, w_out_bf, ln1_g[l], ln1_b[l],
                                            w_router_t_bf, router_bias[l], nb, seq,
                                            h_all, t_all, gr['row_off'])
        gr.update(z=z, x1=x1, eidx_t=eidx_t, ew_t=ew_t, hfr=hfr, hfi=hfi)

    eidx_t = jnp.concatenate([gr['eidx_t'] for gr in groups], axis=1)
    wsel = jnp.concatenate([gr['ew_t'] for gr in groups], axis=1).T
    pairs, tok_blocks, dst_blocks = _dispatch(eidx_t, n_experts)
    y3 = _experts(h_all, w_e_gate[l], w_e_up[l], w_e_down[l], pairs, tok_blocks, dst_blocks)

    outs = []
    for gr in groups:
        outs.append(_combine(gr['x1'], h_all, y3, wsel, gr['mod'][5], wsg_bf, wsu_bf, wsd_bf,
                             ln2_g[l], ln2_b[l], gr['nb'], gr['seq'], gr['row_off']))
    y_prompt = outs[0].reshape(bp, lp, d)
    y_sample = outs[1].reshape(bs, ls, d)

    nbuf = state_pool.shape[2]
    zp = groups[0]['z'].reshape(bp, lp, d)[:, :, :c_pool]
    zs = groups[1]['z'].reshape(bs, ls, d)[:, :, :c_pool]
    pool_p = zp[:, lp - nbuf:, :][None]
    pool_s = jnp.concatenate([state_pool[l], zs], axis=1)[:, -nbuf:, :][None]
    st = lambda a, nb: a.reshape(nb, n_groups, n_state)[None]
    return (y_prompt, y_sample, pool_p, pool_s,
            st(groups[0]['hfr'], bp), st(groups[0]['hfi'], bp),
            st(groups[1]['hfr'], bs), st(groups[1]['hfi'], bs))
```

```python
import functools

import jax
import jax.numpy as jnp
from jax import lax
from jax.experimental import pallas as pl
from jax.experimental.pallas import tpu as pltpu

F32 = jnp.float32
BF16 = jnp.bfloat16
I32 = jnp.int32
U32 = jnp.uint32

DEPTH = 1
PAST_LEN = 16384
POOL_WINDOWS = (2, 4, 8, 16)
POOL_HALO = 16
SSM_P = 16
SSM_N = 64
N_EXPERT_GROUPS = 8
TOPK_GROUPS = 4
TOP_K = 8
ROUTED_SCALE = 2.5
LN_EPS = 1e-5
DN_ALPHA = (2.0 * DEPTH) ** 0.25

ROWS = 256
SUBLANES = 8
LANES = 128
SCAN_W = 512
MOE_BM = 256
PAIR_FIRST_OF_BLOCK, PAIR_NEW_EXPERT, PAIR_HAS_NEXT_EXPERT, PAIR_HAS_NEXT2_EXPERT, PAIR_SCATTERS = 1, 2, 4, 8, 16
PAIR_WEIGHT_SLOT = 32
WEIGHT_SLOTS = 3
GATHER_SLOTS = 3
COMB_ROWS = 256
VMEM_LIMIT = 56 * 1024 * 1024


def _cparams(*sem):
    return pltpu.CompilerParams(dimension_semantics=sem, vmem_limit_bytes=VMEM_LIMIT)


def _ln(x):
    xc = x - jnp.mean(x, axis=-1, keepdims=True)
    return xc * lax.rsqrt(jnp.mean(xc * xc, axis=-1, keepdims=True) + LN_EPS)


def _rows(m_ref, bb, r):
    m = m_ref[...]
    c = m.shape[-1]
    return jnp.broadcast_to(m, (bb, r, c)).reshape(bb * r, c)


def _sigmoid(x):
    return 1.0 / (1.0 + jnp.exp(-x))


def _bf16_bits(x):
    return lax.bitcast_convert_type(x.astype(BF16).astype(F32), U32)


def _pack_pairs(x):
    c = x.shape[-1] // 2
    return (_bf16_bits(x[:, :c]) >> 16) | _bf16_bits(x[:, c:])


def _unpack_pairs(w):
    return (lax.bitcast_convert_type(w << 16, F32),
            lax.bitcast_convert_type(w & jnp.uint32(0xFFFF0000), F32))


def _adaln_body(c_ref, w_ref, b_ref, o_ref):
    c = c_ref[...]
    s = (c * _sigmoid(c)).astype(BF16)
    o_ref[...] = jnp.dot(s, w_ref[...].astype(BF16), preferred_element_type=F32) + b_ref[...]


def _adaln(c_all, w_ada, b_ada):
    bc, d = c_all.shape
    n = w_ada.shape[1]
    tn = 1024
    return pl.pallas_call(
        _adaln_body,
        grid=(n // tn,),
        in_specs=[pl.BlockSpec((bc, d), lambda j: (0, 0)),
                  pl.BlockSpec((d, tn), lambda j: (0, j)),
                  pl.BlockSpec((1, tn), lambda j: (0, j))],
        out_specs=pl.BlockSpec((bc, tn), lambda j: (0, j)),
        out_shape=jax.ShapeDtypeStruct((bc, n), F32),
        compiler_params=_cparams("arbitrary"),
        name="adaln",
    )(c_all, w_ada, b_ada.reshape(1, n))


def _ssm_prep_body(ar_ref, ai_ref, dt_ref, ar16_ref, ai16_ref, dt16_ref, br_ref, bi_ref,
                   pr_ref, pi_ref, bbr_ref, bbi_ref):
    def zoh(a_r, a_i, dt):
        mag = jnp.exp(a_r * dt)
        ab_r, ab_i = mag * jnp.cos(a_i * dt), mag * jnp.sin(a_i * dt)
        den = a_r * a_r + a_i * a_i
        nr = ab_r - 1.0
        return ab_r, ab_i, (nr * a_r + ab_i * a_i) / den, (ab_i * a_r - nr * a_i) / den

    ab_r, ab_i, _, _ = zoh(ar_ref[...], ai_ref[...], jnp.exp(dt_ref[...]))
    p_r, p_i = ab_r, ab_i
    for k in range(pr_ref.shape[0]):
        pr_ref[k] = p_r
        pi_ref[k] = p_i
        p_r, p_i = p_r * ab_r - p_i * ab_i, p_r * ab_i + p_i * ab_r
    _, _, f_r, f_i = zoh(ar16_ref[...], ai16_ref[...], jnp.exp(dt16_ref[...]))
    b_r, b_i = br_ref[...], bi_ref[...]
    bbr_ref[...] = f_r * b_r - f_i * b_i
    bbi_ref[...] = f_r * b_i + f_i * b_r


def _ssm_prep(a_re, a_im, log_dt, b_re, b_im, npow):
    g, n = a_re.shape
    p = b_re.shape[-1]
    dt = jnp.broadcast_to(log_dt[:, None], (g, n))
    rep = lambda a: jnp.repeat(a, p, axis=-1)
    outs = pl.pallas_call(
        _ssm_prep_body,
        out_shape=(jax.ShapeDtypeStruct((npow, g, n), F32), jax.ShapeDtypeStruct((npow, g, n), F32),
                   jax.ShapeDtypeStruct((g, n * p), F32), jax.ShapeDtypeStruct((g, n * p), F32)),
        name="ssm_prep",
    )(a_re, a_im, dt, rep(a_re), rep(a_im), rep(dt), b_re.reshape(g, n * p), b_im.reshape(g, n * p))
    pw_r, pw_i, bb_r, bb_i = outs
    return (pw_r.reshape(npow, g * n), pw_i.reshape(npow, g * n),
            bb_r.reshape(g, n, p), bb_i.reshape(g, n, p))


def _block_diag(w, gpt):
    g, a, b = w.shape
    w4 = w.reshape(g // gpt, gpt, a, b)
    eye = jnp.eye(gpt, dtype=w.dtype)
    return jnp.einsum('jgab,gh->jgahb', w4, eye).reshape(g // gpt, gpt * a, gpt * b)


def _scan_perm(seglen):
    new = jnp.arange(ROWS)
    grp, rem = new // (SUBLANES * seglen), new % (SUBLANES * seglen)
    old = grp * (SUBLANES * seglen) + (rem % SUBLANES) * seglen + rem // SUBLANES
    return (old[:, None] == jnp.arange(ROWS)[None, :]).astype(BF16)


def _mix_in_body(x_ref, sh_ref, sc_ref, w_ref, perm_ref, z_ref, *, bb, r):
    c = z_ref.shape[-1] // 2
    u = (_ln(x_ref[...]) * (1.0 + _rows(sc_ref, bb, r)) + _rows(sh_ref, bb, r)).astype(BF16)
    z_ref[:, :c] = jnp.dot(u, w_ref[:, :c], preferred_element_type=F32)
    up = jnp.dot(perm_ref[...], u, preferred_element_type=F32).astype(BF16)
    z_ref[:, c:] = jnp.dot(up, w_ref[:, c:], preferred_element_type=F32)


def _seq_grid(nb, seq):
    if seq >= ROWS:
        bb, r, nl = 1, ROWS, seq // ROWS
        grid = (nb, nl)
        row_map = lambda b, l: (b * nl + l, 0)
    else:
        bb, r, nl = ROWS // seq, seq, 1
        grid = (nb // bb, 1)
        row_map = lambda b, l: (b, 0)
    seq_map = lambda b, l: (b, 0, 0)
    return bb, r, grid, row_map, seq_map


def _mix_in(x2, sh, sc, w_in_bf, perm, nb, seq):
    t, d = x2.shape
    bb, r, grid, row_map, seq_map = _seq_grid(nb, seq)
    const2 = lambda b, l: (0, 0)
    return pl.pallas_call(
        functools.partial(_mix_in_body, bb=bb, r=r),
        grid=grid,
        in_specs=[pl.BlockSpec((ROWS, d), row_map),
                  pl.BlockSpec((bb, 1, d), seq_map),
                  pl.BlockSpec((bb, 1, d), seq_map),
                  pl.BlockSpec((d, d), const2),
                  pl.BlockSpec((ROWS, ROWS), const2)],
        out_specs=pl.BlockSpec((ROWS, d), row_map),
        out_shape=jax.ShapeDtypeStruct((t, d), F32),
        compiler_params=_cparams("arbitrary", "arbitrary"),
        name="mix_in",
    )(x2, sh, sc, w_in_bf, perm)


def _pool_body(z_ref, pre_ref, wp_ref, ps_ref, gp_ref, o_ref, carry_ref, *, bb, r, start_pos):
    li = pl.program_id(1)
    c = z_ref.shape[-1]
    gw = c // len(POOL_WINDOWS)
    rp = POOL_HALO + r

    @pl.when(li == 0)
    def _():
        carry_ref[...] = pre_ref[...]

    za = z_ref[...].reshape(bb, r, c)
    xp3 = jnp.concatenate([carry_ref[...], za], axis=1)
    carry_ref[...] = xp3[:, r:, :]
    xp = xp3.reshape(bb * rp, c)
    pos1 = lax.broadcasted_iota(I32, (bb, r, gw), 1) + (start_pos + 1) + li * r
    outs = []
    ssq = jnp.zeros((bb * r, 1), F32)
    for gi, w in enumerate(POOL_WINDOWS):
        cols = slice(gi * gw, (gi + 1) * gw)
        s = xp[:, cols]
        sh = 1
        while sh < w:
            s = s + pltpu.roll(s, sh, 0)
            sh *= 2
        win = s.reshape(bb, rp, gw)[:, POOL_HALO:, :]
        cnt = jnp.minimum(pos1, w).astype(F32)
        d = (win / cnt - za[:, :, cols]).reshape(bb * r, gw)
        y = jnp.dot(d.astype(BF16), wp_ref[gi], preferred_element_type=F32) * ps_ref[:, cols]
        ssq = ssq + jnp.sum(y * y, axis=-1, keepdims=True)
        outs.append(y)
    scale = lax.rsqrt(ssq * (1.0 / c) + LN_EPS)
    for gi, y in enumerate(outs):
        cols = slice(gi * gw, (gi + 1) * gw)
        o_ref[:, cols] = (y * scale * gp_ref[:, cols]).astype(o_ref.dtype)


def _pool(z, prefix16, w_pool_bf, pool_scale, g_pool, nb, seq, start_pos):
    t = z.shape[0]
    c = pool_scale.shape[-1]
    bb, r, grid, row_map, seq_map = _seq_grid(nb, seq)
    const2 = lambda b, l: (0, 0)
    return pl.pallas_call(
        functools.partial(_pool_body, bb=bb, r=r, start_pos=start_pos),
        grid=grid,
        in_specs=[pl.BlockSpec((ROWS, c), row_map),
                  pl.BlockSpec((bb, POOL_HALO, c), seq_map),
                  pl.BlockSpec(w_pool_bf.shape, lambda b, l: (0, 0, 0)),
                  pl.BlockSpec((1, c), const2),
                  pl.BlockSpec((1, c), const2)],
        out_specs=pl.BlockSpec((ROWS, c), row_map),
        out_shape=jax.ShapeDtypeStruct((t, c), BF16),
        scratch_shapes=[pltpu.VMEM((bb, POOL_HALO, c), F32)],
        compiler_params=_cparams("arbitrary", "arbitrary"),
        name="pool",
    )(z, prefix16, w_pool_bf, pool_scale.reshape(1, c), g_pool.reshape(1, c))


def _cmul_add(x_r, x_i, m_r, m_i, y_r, y_i):
    return x_r + m_r * y_r - m_i * y_i, x_i + m_r * y_i + m_i * y_r


def _ssm_body(z_ref, h0r_ref, h0i_ref, bbr_ref, bbi_ref, cr_ref, nci_ref, dsk_ref, pr_ref, pi_ref,
              wg_ref, bg_ref, gs_ref, unperm_ref, o_ref, hfr_ref, hfi_ref, hr_s, hi_s, y_s, car_s, cai_s,
              *, seglen, chained):
    li = pl.program_id(1)
    c = z_ref.shape[-1]
    nstate = hr_s.shape[-1]
    ntile = nstate // SCAN_W
    cw = c // ntile
    u = z_ref[...]
    ub = u.astype(BF16)
    for j in range(ntile):
        sl = slice(j * SCAN_W, (j + 1) * SCAN_W)
        uj = ub[:, j * cw:(j + 1) * cw]
        hr_s[:, sl] = jnp.dot(uj, bbr_ref[j], preferred_element_type=F32)
        hi_s[:, sl] = jnp.dot(uj, bbi_ref[j], preferred_element_type=F32)

    if chained:
        @pl.when(li == 0)
        def _():
            car_s[...] = h0r_ref[0]
            cai_s[...] = h0i_ref[0]

    grp_rows = SUBLANES * seglen
    bc8 = lambda v: jnp.broadcast_to(v, (SUBLANES, SCAN_W))
    for j in range(ntile):
        sl = slice(j * SCAN_W, (j + 1) * SCAN_W)
        a_r, a_i = bc8(pr_ref[0:1, sl]), bc8(pi_ref[0:1, sl])

        def step(t, h, base, sl=sl, a_r=a_r, a_i=a_i):
            off = pl.multiple_of(base + t * SUBLANES, SUBLANES)
            h_r, h_i = _cmul_add(hr_s[pl.ds(off, SUBLANES), sl], hi_s[pl.ds(off, SUBLANES), sl],
                                 a_r, a_i, h[0], h[1])
            hr_s[pl.ds(off, SUBLANES), sl] = h_r
            hi_s[pl.ds(off, SUBLANES), sl] = h_i
            return h_r, h_i

        if chained:
            zero = jnp.zeros((SUBLANES, SCAN_W), F32)
            e_r, e_i = lax.fori_loop(0, seglen, functools.partial(step, base=0), (zero, zero), unroll=4)
            al_r, al_i = pr_ref[seglen - 1:seglen, sl], pi_ref[seglen - 1:seglen, sl]
            s_r, s_i = car_s[:, sl], cai_s[:, sl]
            ent_r, ent_i = [s_r], [s_i]
            for i in range(SUBLANES):
                s_r, s_i = _cmul_add(e_r[i:i + 1], e_i[i:i + 1], al_r, al_i, s_r, s_i)
                if i + 1 < SUBLANES:
                    ent_r.append(s_r)
                    ent_i.append(s_i)
            car_s[:, sl] = s_r
            cai_s[:, sl] = s_i
            hfr_ref[0, :, sl] = s_r
            hfi_ref[0, :, sl] = s_i
            ent_r, ent_i = jnp.concatenate(ent_r, axis=0), jnp.concatenate(ent_i, axis=0)

            def fix(t, _, sl=sl, ent_r=ent_r, ent_i=ent_i):
                off = pl.multiple_of(t * SUBLANES, SUBLANES)
                x_r, x_i = _cmul_add(hr_s[pl.ds(off, SUBLANES), sl], hi_s[pl.ds(off, SUBLANES), sl],
                                     bc8(pr_ref[pl.ds(t, 1), sl]), bc8(pi_ref[pl.ds(t, 1), sl]), ent_r, ent_i)
                hr_s[pl.ds(off, SUBLANES), sl] = x_r
                hi_s[pl.ds(off, SUBLANES), sl] = x_i
                return 0

            lax.fori_loop(0, seglen, fix, 0, unroll=4)
        else:
            for g in range(z_ref.shape[0] // grp_rows):
                rs = slice(g * SUBLANES, (g + 1) * SUBLANES)
                e_r, e_i = lax.fori_loop(0, seglen, functools.partial(step, base=g * grp_rows),
                                         (h0r_ref[rs, sl], h0i_ref[rs, sl]), unroll=True)
                hfr_ref[rs, sl] = e_r
                hfi_ref[rs, sl] = e_i

    for j in range(ntile):
        sl = slice(j * SCAN_W, (j + 1) * SCAN_W)
        cs = slice(j * cw, (j + 1) * cw)
        y_s[:, cs] = (jnp.dot(hr_s[:, sl].astype(BF16), cr_ref[j], preferred_element_type=F32)
                      + jnp.dot(hi_s[:, sl].astype(BF16), nci_ref[j], preferred_element_type=F32)
                      + dsk_ref[:, cs] * u[:, cs])
    y = y_s[...]
    g = 0.5 * y * (1.0 + jnp.tanh(0.7978845608028654 * (y + 0.044715 * (y * y * y))))
    gate = jnp.dot(g.astype(BF16), wg_ref[...], preferred_element_type=F32) + bg_ref[...]
    out = g * _sigmoid(gate)
    scale = lax.rsqrt(jnp.mean(out * out, axis=-1, keepdims=True) + LN_EPS)
    outp = (out * scale * gs_ref[...]).astype(BF16)
    o_ref[...] = jnp.dot(unperm_ref[...], outp, preferred_element_type=F32).astype(o_ref.dtype)


def _ssm(z, h0r, h0i, bbr_bd, bbi_bd, cr_bd, nci_bd, d_skip, pw_r, pw_i, w_glu_bf, b_glu, g_ssm, unperm,
         nb, seq):
    t = z.shape[0]
    c = d_skip.shape[-1]
    nstate = pw_r.shape[-1]
    bb, r, grid, row_map, seq_map = _seq_grid(nb, seq)
    chained = bb == 1
    seglen = r // SUBLANES if chained else r
    const2 = lambda b, l: (0, 0)
    const3 = lambda b, l: (0, 0, 0)
    full = lambda a: pl.BlockSpec(a.shape, const3 if a.ndim == 3 else const2)
    right_half = (lambda b, l: (row_map(b, l)[0], 1))
    if chained:
        st_spec = pl.BlockSpec((1, 1, nstate), seq_map)
        st_shape = jax.ShapeDtypeStruct((nb, 1, nstate), F32)
    else:
        st_spec = pl.BlockSpec((bb, nstate), lambda b, l: (b, 0))
        st_shape = jax.ShapeDtypeStruct((nb, nstate), F32)
    return pl.pallas_call(
        functools.partial(_ssm_body, seglen=seglen, chained=chained),
        grid=grid,
        in_specs=[pl.BlockSpec((ROWS, c), right_half),
                  st_spec, st_spec,
                  full(bbr_bd), full(bbi_bd), full(cr_bd), full(nci_bd),
                  pl.BlockSpec((1, c), const2),
                  full(pw_r), full(pw_i),
                  full(w_glu_bf),
                  pl.BlockSpec((1, c), const2),
                  pl.BlockSpec((1, c), const2),
                  pl.BlockSpec((ROWS, ROWS), const2)],
        out_specs=[pl.BlockSpec((ROWS, c), row_map), st_spec, st_spec],
        out_shape=[jax.ShapeDtypeStruct((t, c), BF16), st_shape, st_shape],
        scratch_shapes=[pltpu.VMEM((ROWS, nstate), F32), pltpu.VMEM((ROWS, nstate), F32),
                        pltpu.VMEM((ROWS, c), F32),
                        pltpu.VMEM((1, nstate), F32), pltpu.VMEM((1, nstate), F32)],
        compiler_params=_cparams("arbitrary", "arbitrary"),
        name="ssm",
    )(z, h0r, h0i, bbr_bd, bbi_bd, cr_bd, nci_bd, d_skip.reshape(1, c), pw_r, pw_i,
      w_glu_bf, b_glu.reshape(1, c), g_ssm.reshape(1, c), unperm)


def _route(s_t, bias_t):
    e, tn = s_t.shape
    per = e // N_EXPERT_GROUPS
    neg = -jnp.inf
    sb = s_t + bias_t
    rowl = lax.broadcasted_iota(I32, (per, tn), 0)
    gscore = []
    for g in range(N_EXPERT_GROUPS):
        blk = sb[g * per:(g + 1) * per]
        m1 = jnp.max(blk, axis=0, keepdims=True)
        i1 = jnp.min(jnp.where(blk == m1, rowl, per), axis=0, keepdims=True)
        m2 = jnp.max(jnp.where(rowl == i1, neg, blk), axis=0, keepdims=True)
        gscore.append(m1 + m2)
    cur = jnp.concatenate(gscore, axis=0)
    rowg = lax.broadcasted_iota(I32, cur.shape, 0)
    gsel = jnp.zeros(cur.shape, F32)
    for _ in range(TOPK_GROUPS):
        m = jnp.max(cur, axis=0, keepdims=True)
        hit = rowg == jnp.min(jnp.where(cur == m, rowg, N_EXPERT_GROUPS), axis=0, keepdims=True)
        gsel = jnp.where(hit, 1.0, gsel)
        cur = jnp.where(hit, neg, cur)
    cur = jnp.concatenate(
        [jnp.where(jnp.broadcast_to(gsel[g:g + 1], (per, tn)) > 0.0, sb[g * per:(g + 1) * per], neg)
         for g in range(N_EXPERT_GROUPS)], axis=0)
    rowe = lax.broadcasted_iota(I32, (e, tn), 0)
    idxs, vals = [], []
    for _ in range(TOP_K):
        m = jnp.max(cur, axis=0, keepdims=True)
        idx = jnp.min(jnp.where(cur == m, rowe, e), axis=0, keepdims=True)
        hit = rowe == idx
        idxs.append(idx)
        vals.append(jnp.sum(jnp.where(hit, s_t, 0.0), axis=0, keepdims=True))
        cur = jnp.where(hit, neg, cur)
    w = jnp.concatenate(vals, axis=0)
    w = w / jnp.sum(w, axis=0, keepdims=True) * ROUTED_SCALE
    return jnp.concatenate(idxs, axis=0), w


def _out_proj_body(x_ref, ya_ref, ys_ref, g1_ref, sh2_ref, sc2_ref, wo_ref, lg_ref, lb_ref, wrt_ref, rb_ref,
                   x1_ref, h_ref, ei_ref, ew_ref, *, bb, r):
    ca = ya_ref.shape[-1]
    m = (jnp.dot(ya_ref[...], wo_ref[:ca, :], preferred_element_type=F32)
         + jnp.dot(ys_ref[...], wo_ref[ca:, :], preferred_element_type=F32))
    res = DN_ALPHA * x_ref[...] + (1.0 + _rows(g1_ref, bb, r)) * m
    x1 = _ln(res) * lg_ref[...] + lb_ref[...]
    x1_ref[...] = x1
    h = _ln(x1) * (1.0 + _rows(sc2_ref, bb, r)) + _rows(sh2_ref, bb, r)
    h_ref[...] = _pack_pairs(h)
    logit_t = lax.dot_general(wrt_ref[...], h.astype(BF16), (((1,), (1,)), ((), ())),
                              preferred_element_type=F32)
    idx, w = _route(_sigmoid(logit_t), rb_ref[...])
    ei_ref[...] = idx
    ew_ref[...] = w


def _out_proj(x2, ya, ys, g1, sh2, sc2, w_out_bf, ln_g, ln_b, w_router_t_bf, router_bias, nb, seq,
              h_all, t_all, row_off):
    t, d = x2.shape
    ca = ya.shape[-1]
    e = w_router_t_bf.shape[0]
    bb, r, grid, row_map, seq_map = _seq_grid(nb, seq)
    const2 = lambda b, l: (0, 0)
    tok_map = lambda b, l: (0, row_map(b, l)[0])
    blk_off = row_off // ROWS
    h_map = lambda b, l: (row_map(b, l)[0] + blk_off, 0)
    in_specs = [pl.BlockSpec((ROWS, d), row_map),
                pl.BlockSpec((ROWS, ca), row_map),
                pl.BlockSpec((ROWS, ca), row_map),
                pl.BlockSpec((bb, 1, d), seq_map),
                pl.BlockSpec((bb, 1, d), seq_map),
                pl.BlockSpec((bb, 1, d), seq_map),
                pl.BlockSpec((d, d), const2),
                pl.BlockSpec((1, d), const2),
                pl.BlockSpec((1, d), const2),
                pl.BlockSpec((e, d), const2),
                pl.BlockSpec((e, 1), const2)]
    args = [x2, ya, ys, g1, sh2, sc2, w_out_bf, ln_g.reshape(1, d), ln_b.reshape(1, d),
            w_router_t_bf, router_bias.reshape(e, 1)]
    body = functools.partial(_out_proj_body, bb=bb, r=r)
    aliases = {}
    if h_all is not None:
        in_specs.append(pl.BlockSpec(memory_space=pl.ANY))
        args.append(h_all)
        aliases = {len(args) - 1: 1}
        inner = body
        body = lambda *refs: inner(*refs[:11], *refs[12:])
    return pl.pallas_call(
        body,
        grid=grid,
        in_specs=in_specs,
        out_specs=[pl.BlockSpec((ROWS, d), row_map),
                   pl.BlockSpec((ROWS, d // 2), h_map),
                   pl.BlockSpec((TOP_K, ROWS), tok_map),
                   pl.BlockSpec((TOP_K, ROWS), tok_map)],
        out_shape=[jax.ShapeDtypeStruct((t, d), F32),
                   jax.ShapeDtypeStruct((t_all, d // 2), U32),
                   jax.ShapeDtypeStruct((TOP_K, t), I32),
                   jax.ShapeDtypeStruct((TOP_K, t), F32)],
        input_output_aliases=aliases,
        compiler_params=_cparams("arbitrary", "arbitrary"),
        name="out_proj",
    )(*args)


def _experts_body(pb_ref, pe_ref, plo_ref, pfl_ref, pne_ref, pne2_ref, np_ref, tokc_ref, tokn_ref, tok2_ref,
                  dstp_ref, dstc_ref,
                  h_hbm, wg_hbm, wu_hbm, wd_hbm, y_hbm, xbuf, ybuf, xlo_s, xhi_s, wg_s, wu_s, wd_s,
                  gsem, ssem, wsem):
    p = pl.program_id(0)
    n_pairs = np_ref[0]
    valid = p < n_pairs
    b = pb_ref[p]
    slot = b % 2
    xslot = b % GATHER_SLOTS
    flags = pfl_ref[p]
    first = (flags & PAIR_FIRST_OF_BLOCK) != 0
    scatters = (flags & PAIR_SCATTERS) != 0
    new_expert = (flags & PAIR_NEW_EXPERT) != 0
    has_next = (flags & PAIR_HAS_NEXT_EXPERT) != 0
    has_next2 = (flags & PAIR_HAS_NEXT2_EXPERT) != 0
    wslot = flags // PAIR_WEIGHT_SLOT
    bm, c = xbuf.shape[1], xbuf.shape[2]
    weights = ((wg_hbm, wg_s), (wu_hbm, wu_s), (wd_hbm, wd_s))

    def weights_start(e, s):
        for w_hbm, w_s in weights:
            pltpu.make_async_copy(w_hbm.at[e], w_s.at[s], wsem.at[s]).start(priority=1)

    def weights_wait(s):
        for w_hbm, w_s in weights:
            pltpu.make_async_copy(w_hbm.at[0], w_s.at[s], wsem.at[s]).wait()

    def gather_start(tok_ref, s):
        for i in range(bm):
            pltpu.make_async_copy(h_hbm.at[pl.ds(tok_ref[0, 0, i], 1), :],
                                  xbuf.at[s, pl.ds(i, 1), :], gsem.at[s]).start(priority=i % 2)

    def gather_wait(s):
        pltpu.make_async_copy(h_hbm.at[pl.ds(0, bm), :], xbuf.at[s], gsem.at[s]).wait()

    def scatter_start(dst_ref, s):
        for i in range(bm):
            pltpu.make_async_copy(ybuf.at[s, pl.ds(i, 1), :],
                                  y_hbm.at[pl.ds(dst_ref[0, 0, i], 1), :], ssem.at[0]).start(priority=i % 2)

    def scatter_wait():
        pltpu.make_async_copy(ybuf.at[0], y_hbm.at[pl.ds(0, bm), :], ssem.at[0]).wait()

    def expert():
        def proj(w_s):
            return (jnp.dot(xlo_s[...], w_s[wslot, :c, :].astype(BF16), preferred_element_type=F32)
                    + jnp.dot(xhi_s[...], w_s[wslot, c:, :].astype(BF16), preferred_element_type=F32))

        g = proj(wg_s)
        u = proj(wu_s)
        a = (g * _sigmoid(g) * u).astype(BF16)
        return _pack_pairs(jnp.dot(a, wd_s[wslot].astype(BF16), preferred_element_type=F32))

    @pl.when(p == 0)
    def _():
        weights_start(pe_ref[0], 0)

        @pl.when(has_next)
        def _():
            weights_start(pne_ref[0], 1)

        gather_start(tokc_ref, 0)
        gather_start(tokn_ref, 1)
        ybuf[1] = jnp.zeros(ybuf.shape[1:], ybuf.dtype)

    @pl.when(jnp.logical_and(valid, new_expert))
    def _():
        weights_wait(wslot)

        @pl.when(has_next2)
        def _():
            weights_start(pne2_ref[p], (wslot + 2) % WEIGHT_SLOTS)

    def first_pair(with_scatter):
        @pl.when(b > 0)
        def _():
            scatter_wait()

        gather_wait(xslot)
        x_lo, x_hi = _unpack_pairs(xbuf[xslot])
        xlo_s[...] = x_lo.astype(BF16)
        xhi_s[...] = x_hi.astype(BF16)
        gather_start(tok2_ref, (b + 2) % GATHER_SLOTS)
        if with_scatter:
            scatter_start(dstp_ref, 1 - slot)
        ybuf[slot] = expert()

    def later_pair(with_scatter):
        if with_scatter:
            scatter_start(dstp_ref, 1 - slot)
        y = expert()
        mine = lax.broadcasted_iota(I32, y.shape, 0) >= plo_ref[p]
        ybuf[slot] = jnp.where(mine, y, ybuf[slot])

    for is_first, body in ((True, first_pair), (False, later_pair)):
        for with_scatter in (False, True):
            cond = jnp.logical_and(valid, jnp.logical_and(first == is_first, scatters == with_scatter))
            pl.when(cond)(functools.partial(body, with_scatter))

    @pl.when(p == n_pairs - 1)
    def _():
        scatter_wait()
        scatter_start(dstc_ref, slot)
        scatter_wait()
        gather_wait((b + 1) % GATHER_SLOTS)
        gather_wait((b + 2) % GATHER_SLOTS)


def _experts(h_pack, w_gate, w_up, w_down, pairs, tok_blocks, dst_blocks):
    t, c = h_pack.shape
    d = 2 * c
    e, _, de = w_gate.shape
    nblk = tok_blocks.shape[0]
    bm = MOE_BM
    pb, pe, plo, pfl, pne, pne2, n_pairs = pairs
    ws = WEIGHT_SLOTS
    cur = lambda p, pb, *_: (pb[p], 0, 0)
    nxt = lambda p, pb, *_: (jnp.minimum(pb[p] + 1, nblk - 1), 0, 0)
    nx2 = lambda p, pb, *_: (jnp.minimum(pb[p] + 2, nblk - 1), 0, 0)
    prv = lambda p, pb, *_: (jnp.maximum(pb[p] - 1, 0), 0, 0)
    smem_blk = lambda m: pl.BlockSpec((1, 1, bm), m, memory_space=pltpu.SMEM)
    hbm = pl.BlockSpec(memory_space=pl.ANY)
    grid_spec = pltpu.PrefetchScalarGridSpec(
        num_scalar_prefetch=7,
        grid=(pb.shape[0],),
        in_specs=[smem_blk(cur), smem_blk(nxt), smem_blk(nx2), smem_blk(prv), smem_blk(cur), hbm, hbm, hbm, hbm],
        out_specs=hbm,
        scratch_shapes=[pltpu.VMEM((GATHER_SLOTS, bm, c), U32), pltpu.VMEM((2, bm, c), U32),
                        pltpu.VMEM((bm, c), BF16), pltpu.VMEM((bm, c), BF16),
                        pltpu.VMEM((ws, d, de), F32), pltpu.VMEM((ws, d, de), F32), pltpu.VMEM((ws, de, d), F32),
                        pltpu.SemaphoreType.DMA((GATHER_SLOTS,)), pltpu.SemaphoreType.DMA((1,)),
                        pltpu.SemaphoreType.DMA((ws,))],
    )
    return pl.pallas_call(
        _experts_body,
        grid_spec=grid_spec,
        out_shape=jax.ShapeDtypeStruct((nblk * bm, c), U32),
        compiler_params=_cparams("arbitrary"),
        name="experts",
    )(pb, pe, plo, pfl, pne, pne2, n_pairs, tok_blocks, tok_blocks, tok_blocks, dst_blocks, dst_blocks,
      h_pack, w_gate, w_up, w_down)


def _dispatch(eidx_t, n_experts):
    k, t = eidx_t.shape
    a = t * k
    bm = MOE_BM
    nb = a // bm
    flat_e = eidx_t.reshape(a)
    id_bits = (a - 1).bit_length()
    assert id_bits + (n_experts - 1).bit_length() < 32
    keyed = lax.sort(flat_e * (1 << id_bits) + lax.iota(I32, a))
    se, order = keyed >> id_bits, keyed & ((1 << id_bits) - 1)
    tok = order % t
    ex = lax.iota(I32, n_experts)
    starts = jnp.sum((se[None, :] < ex[:, None]).astype(I32), axis=1)
    seb = se.reshape(nb, bm)
    e_lo, e_hi = seb[:, 0], seb[:, bm - 1]
    npair = e_hi - e_lo + 1
    cum = jnp.cumsum(npair)
    off = cum - npair
    p = lax.iota(I32, nb + n_experts)
    pb = jnp.minimum(jnp.sum((cum[None, :] <= p[:, None]).astype(I32), axis=1), nb - 1)
    pe = jnp.minimum(e_lo[pb] + p - off[pb], e_hi[pb])
    plo = jnp.clip(starts[pe] - pb * bm, 0, bm)
    valid = p < cum[-1]
    new_e = jnp.logical_and(valid, jnp.concatenate([jnp.ones((1,), bool), pe[1:] != pe[:-1]]))
    wslot = (jnp.cumsum(new_e.astype(I32)) - 1) % WEIGHT_SLOTS

    def next_used(e):
        later = jnp.logical_and(pe[None, :] > e[:, None], valid[None, :])
        return jnp.min(jnp.where(later, pe[None, :], n_experts), axis=1)

    pne = next_used(pe)
    pne2 = next_used(pne)
    j = p - off[pb]
    scatters = jnp.logical_or(j == 1, jnp.logical_and(j == 0, npair[pb] == 1))
    pfl = (PAIR_FIRST_OF_BLOCK * (j == 0) + PAIR_NEW_EXPERT * new_e + PAIR_HAS_NEXT_EXPERT * (pne < n_experts)
           + PAIR_HAS_NEXT2_EXPERT * (pne2 < n_experts) + PAIR_WEIGHT_SLOT * wslot
           + PAIR_SCATTERS * scatters).astype(I32)
    pne, pne2 = jnp.minimum(pne, n_experts - 1), jnp.minimum(pne2, n_experts - 1)
    pairs = (pb.astype(I32), pe.astype(I32), plo.astype(I32), pfl, pne.astype(I32), pne2.astype(I32),
             cum[-1:].astype(I32))
    return pairs, tok.reshape(nb, 1, bm), order.reshape(nb, 1, bm)


def _combine_body(x1_ref, h_ref, w_ref, g2_ref, wsg_ref, wsu_ref, wsd_ref, lg_ref, lb_ref, *rest, bb, r):
    y_refs, o_ref = rest[:TOP_K], rest[TOP_K]
    c = h_ref.shape[-1]
    h_lo, h_hi = _unpack_pairs(h_ref[...])
    h_lo, h_hi = h_lo.astype(BF16), h_hi.astype(BF16)

    def proj(w_ref):
        return (jnp.dot(h_lo, w_ref[:c, :], preferred_element_type=F32)
                + jnp.dot(h_hi, w_ref[c:, :], preferred_element_type=F32))

    g = proj(wsg_ref)
    u = proj(wsu_ref)
    shared = jnp.dot((g * _sigmoid(g) * u).astype(BF16), wsd_ref[...], preferred_element_type=F32)
    w = w_ref[...]
    r_lo = r_hi = None
    for k in range(TOP_K):
        y_lo, y_hi = _unpack_pairs(y_refs[k][...])
        wk = w[:, k:k + 1]
        r_lo = wk * y_lo if r_lo is None else r_lo + wk * y_lo
        r_hi = wk * y_hi if r_hi is None else r_hi + wk * y_hi
    routed = jnp.concatenate([r_lo, r_hi], axis=-1)
    res = DN_ALPHA * x1_ref[...] + (1.0 + _rows(g2_ref, bb, r)) * (routed + shared)
    o_ref[...] = _ln(res) * lg_ref[...] + lb_ref[...]


def _combine(x1, h_pack, y_pack, wsel, g2, wsg_bf, wsu_bf, wsd_bf, ln_g, ln_b, nb, seq, row_off):
    t, d = x1.shape
    t_all, c = h_pack.shape
    ds_ = wsg_bf.shape[1]
    rows = COMB_ROWS
    if seq >= rows:
        bb, r, nl = 1, rows, seq // rows
        grid = (nb, nl)
        row_map = lambda b, l: (b * nl + l, 0)
    else:
        bb, r, nl = rows // seq, seq, 1
        grid = (nb // bb, 1)
        row_map = lambda b, l: (b, 0)
    seq_map = lambda b, l: (b, 0, 0)
    off = row_off // rows
    all_map = lambda b, l: (row_map(b, l)[0] + off, 0)
    const2 = lambda b, l: (0, 0)

    def slot_map(k):
        return lambda b, l: (row_map(b, l)[0] + off + k * (t_all // rows), 0)

    return pl.pallas_call(
        functools.partial(_combine_body, bb=bb, r=r),
        grid=grid,
        in_specs=[pl.BlockSpec((rows, d), row_map),
                  pl.BlockSpec((rows, c), all_map),
                  pl.BlockSpec((rows, TOP_K), all_map),
                  pl.BlockSpec((bb, 1, d), seq_map),
                  pl.BlockSpec((d, ds_), const2),
                  pl.BlockSpec((d, ds_), const2),
                  pl.BlockSpec((ds_, d), const2),
                  pl.BlockSpec((1, d), const2),
                  pl.BlockSpec((1, d), const2)]
                 + [pl.BlockSpec((rows, c), slot_map(k)) for k in range(TOP_K)],
        out_specs=pl.BlockSpec((rows, d), row_map),
        out_shape=jax.ShapeDtypeStruct((t, d), F32),
        compiler_params=_cparams("arbitrary", "arbitrary"),
        name="combine",
    )(x1, h_pack, wsel, g2, wsg_bf, wsu_bf, wsd_bf, ln_g.reshape(1, d), ln_b.reshape(1, d),
      *([y_pack] * TOP_K))


def kernel(x_prompt, x_sample, state_pool, state_ssm_re, state_ssm_im, c_prompt, c_sample, w_ada, b_ada, w_in, w_pool, pool_scale, A_re, A_im, log_dt, B_re, B_im, C_re, C_im, D_skip, w_glu, b_glu, g_pool, g_ssm, w_out, ln1_g, ln1_b, w_router, router_bias, w_e_gate, w_e_up, w_e_down, w_sh_gate, w_sh_up, w_sh_down, ln2_g, ln2_b):
    bp, lp, d = x_prompt.shape
    bs, ls, _ = x_sample.shape
    depth = w_ada.shape[0]
    assert depth == DEPTH == 1
    tp, ts = bp * lp, bs * ls
    t_all = tp + ts
    l = 0
    c_pool = pool_scale.shape[-1]
    n_groups, n_state = A_re.shape[1], A_re.shape[2]
    nstate = n_groups * n_state
    n_experts = w_router.shape[-1]
    gpt = SCAN_W // n_state

    c_all = jnp.concatenate([c_prompt, c_sample], axis=0)
    pad = (-c_all.shape[0]) % SUBLANES
    c_all = jnp.pad(c_all, ((0, pad), (0, 0)))
    mod = _adaln(c_all, w_ada[l], b_ada[l]).reshape(c_all.shape[0], 6, 1, d)
    mod_p = [mod[:bp, i] for i in range(6)]
    mod_s = [mod[bp:bp + bs, i] for i in range(6)]

    w_in_bf = w_in[l].astype(BF16)
    w_pool_bf = w_pool[l].astype(BF16)
    w_glu_bf = w_glu[l].astype(BF16)
    w_out_bf = w_out[l].astype(BF16)
    w_router_t_bf = w_router[l].T.astype(BF16)
    wsg_bf, wsu_bf, wsd_bf = w_sh_gate[l].astype(BF16), w_sh_up[l].astype(BF16), w_sh_down[l].astype(BF16)
    pw_r, pw_i, bb_r, bb_i = _ssm_prep(A_re[l], A_im[l], log_dt[l], B_re[l], B_im[l], ROWS // SUBLANES)
    bbr_bd = _block_diag(jnp.swapaxes(bb_r, 1, 2), gpt).astype(BF16)
    bbi_bd = _block_diag(jnp.swapaxes(bb_i, 1, 2), gpt).astype(BF16)
    cr_bd = _block_diag(jnp.swapaxes(C_re[l], 1, 2), gpt).astype(BF16)
    nci_bd = _block_diag(jnp.swapaxes(-C_im[l], 1, 2), gpt).astype(BF16)
    d_skip = D_skip[l].reshape(-1)

    groups = [
        dict(x=x_prompt.reshape(tp, d), nb=bp, seq=lp, mod=mod_p, start=0, row_off=0,
             prefix=jnp.zeros((bp, POOL_HALO, c_pool), F32),
             h0r=jnp.zeros((bp, 1, nstate), F32), h0i=jnp.zeros((bp, 1, nstate), F32)),
        dict(x=x_sample.reshape(ts, d), nb=bs, seq=ls, mod=mod_s, start=PAST_LEN, row_off=tp,
             prefix=jnp.pad(state_pool[l], ((0, 0), (POOL_HALO - state_pool.shape[2], 0), (0, 0))),
             h0r=state_ssm_re[l].reshape(bs, nstate), h0i=state_ssm_im[l].reshape(bs, nstate)),
    ]

    h_all = None
    for gr in groups:
        sh1, sc1, g1, sh2, sc2, g2 = gr['mod']
        nb, seq = gr['nb'], gr['seq']
        perm = _scan_perm(ROWS // SUBLANES if seq >= ROWS else seq)
        z = _mix_in(gr['x'], sh1, sc1, w_in_bf, perm, nb, seq)
        ya = _pool(z, gr['prefix'], w_pool_bf, pool_scale[l], g_pool[l], nb, seq, gr['start'])
        ys, hfr, hfi = _ssm(z, gr['h0r'], gr['h0i'], bbr_bd, bbi_bd, cr_bd, nci_bd, d_skip, pw_r, pw_i,
                            w_glu_bf, b_glu[l], g_ssm[l], perm.T, nb, seq)
        x1, h_all, eidx_t, ew_t = _out_proj(gr['x'], ya, ys, g1, sh2, sc2, w_out_bf, ln1_g[l], ln1_b[l],
                                            w_router_t_bf, router_bias[l], nb, seq,
                                            h_all, t_all, gr['row_off'])
        gr.update(z=z, x1=x1, eidx_t=eidx_t, ew_t=ew_t, hfr=hfr, hfi=hfi)

    eidx_t = jnp.concatenate([gr['eidx_t'] for gr in groups], axis=1)
    wsel = jnp.concatenate([gr['ew_t'] for gr in groups], axis=1).T
    pairs, tok_blocks, dst_blocks = _dispatch(eidx_t, n_experts)
    y3 = _experts(h_all, w_e_gate[l], w_e_up[l], w_e_down[l], pairs, tok_blocks, dst_blocks)

    outs = []
    for gr in groups:
        outs.append(_combine(gr['x1'], h_all, y3, wsel, gr['mod'][5], wsg_bf, wsu_bf, wsd_bf,
                             ln2_g[l], ln2_b[l], gr['nb'], gr['seq'], gr['row_off']))
    y_prompt = outs[0].reshape(bp, lp, d)
    y_sample = outs[1].reshape(bs, ls, d)

    nbuf = state_pool.shape[2]
    zp = groups[0]['z'].reshape(bp, lp, d)[:, :, :c_pool]
    zs = groups[1]['z'].reshape(bs, ls, d)[:, :, :c_pool]
    pool_p = zp[:, lp - nbuf:, :][None]
    pool_s = jnp.concatenate([state_pool[l], zs], axis=1)[:, -nbuf:, :][None]
    st = lambda a, nb: a.reshape(nb, n_groups, n_state)[None]
    return (y_prompt, y_sample, pool_p, pool_s,
            st(groups[0]['hfr'], bp), st(groups[0]['hfi'], bp),
            st(groups[1]['hfr'], bs), st(groups[1]['hfi'], bs))
```
